```python
import math
import jax, jax.numpy as jnp
from jax import lax
import numpy as np

D_MODEL = 2048
BATCH = 4
SEQ = 4096
DEPTH = 1

CHUNK = 64
N_META = 16
D_MIX = D_MODEL
FOX_HEADS = 8
FOX_HEAD_DIM = 128
D_FOX = FOX_HEADS * FOX_HEAD_DIM
POOL_WINDOWS = (2, 4, 8, 16)
N_POOL_GROUPS = len(POOL_WINDOWS)
D_POOL = D_MIX - D_FOX
POOL_GROUP_DIM = D_POOL // N_POOL_GROUPS
D_IN_PROJ = 3 * D_FOX + FOX_HEADS + D_POOL
Q_BLOCK = 128
N_GROUPS = 4
EXPERTS_PER_GROUP = 8
N_EXPERTS = N_GROUPS * EXPERTS_PER_GROUP
TOP_K = 2
D_EXPERT = 1024
ROW_BLOCK = 128
LN_EPS = 1e-5
DEEPNORM_ALPHA = (2.0 * DEPTH) ** 0.25
DEEPNORM_BETA = (8.0 * DEPTH) ** -0.25

kernel_name = "fox_pool_hier_moe_deepnorm_trunk"


def layer_norm(x, g, b):
    xf = x.astype(jnp.float32)
    mu = jnp.mean(xf, axis=-1, keepdims=True)
    var = jnp.mean(jnp.square(xf - mu), axis=-1, keepdims=True)
    return ((xf - mu) * lax.rsqrt(var + LN_EPS) * g + b).astype(x.dtype)


def forgetting_attention(q, k, v, log_f):
    B_, H, L, hd = q.shape
    Lp = ((L + Q_BLOCK - 1) // Q_BLOCK) * Q_BLOCK
    pad = Lp - L
    q = jnp.pad(q, ((0, 0), (0, 0), (0, pad), (0, 0)))
    k = jnp.pad(k, ((0, 0), (0, 0), (0, pad), (0, 0)))
    v = jnp.pad(v, ((0, 0), (0, 0), (0, pad), (0, 0)))
    c = jnp.cumsum(jnp.pad(log_f, ((0, 0), (0, 0), (0, pad))), axis=-1)
    nq = Lp // Q_BLOCK
    q_blocks = q.reshape(B_, H, nq, Q_BLOCK, hd).transpose(2, 0, 1, 3, 4)
    c_blocks = c.reshape(B_, H, nq, Q_BLOCK).transpose(2, 0, 1, 3)
    starts = jnp.arange(nq, dtype=jnp.int32) * Q_BLOCK
    key_pos = jnp.arange(Lp, dtype=jnp.int32)
    scale = hd ** -0.5

    def query_block(args):
        q_i, c_i, start = args
        s = jnp.einsum('bhqd,bhkd->bhqk', q_i, k,
                       preferred_element_type=jnp.float32) * scale
        s = s + c_i[..., :, None] - c[:, :, None, :]
        causal = (start + jnp.arange(Q_BLOCK, dtype=jnp.int32))[:, None] >= key_pos[None, :]
        s = jnp.where(causal, s, -jnp.inf)
        p = jax.nn.softmax(s, axis=-1)
        return jnp.einsum('bhqk,bhkd->bhqd', p.astype(v.dtype), v)

    o = lax.map(query_block, (q_blocks, c_blocks, starts))
    o = o.transpose(1, 0, 3, 2, 4).reshape(B_, Lp, H * hd)
    return o[:, :L]


def multiscale_pool(u, pool_w, pool_scale):
    B_, L, _ = u.shape
    uf = u.astype(jnp.float32).reshape(B_, L, N_POOL_GROUPS, POOL_GROUP_DIM)
    cs = jnp.cumsum(uf, axis=1)
    t = jnp.arange(L)
    outs = []
    for g, w in enumerate(POOL_WINDOWS):
        cs_g = cs[:, :, g]
        prev = jnp.pad(cs_g, ((0, 0), (w, 0), (0, 0)))[:, :L]
        count = jnp.minimum(t + 1, w).astype(jnp.float32)[None, :, None]
        outs.append((cs_g - prev) / count - uf[:, :, g])
    pooled = jnp.stack(outs, axis=2).astype(u.dtype)
    mixed = jnp.einsum('blgc,gcd->blgd', pooled, pool_w)
    return mixed.reshape(B_, L, D_POOL) * pool_scale


def hybrid_mixer(h, w_in, b_f, pool_w, pool_scale, w_out):
    B_, L, _ = h.shape
    proj = h @ w_in
    q, k, v, f_logit, u = jnp.split(
        proj, [D_FOX, 2 * D_FOX, 3 * D_FOX, 3 * D_FOX + FOX_HEADS], axis=-1)

    def to_heads(t):
        return t.reshape(B_, L, FOX_HEADS, FOX_HEAD_DIM).transpose(0, 2, 1, 3)

    log_f = jax.nn.log_sigmoid((f_logit + b_f).astype(jnp.float32)).transpose(0, 2, 1)
    o_fox = forgetting_attention(to_heads(q), to_heads(k), to_heads(v), log_f)
    o_pool = multiscale_pool(u, pool_w, pool_scale)
    return jnp.concatenate([o_fox.astype(h.dtype), o_pool], axis=-1) @ w_out


def hierarchical_moe(h, w_rg, b_rg, w_re, b_re, w13, w2):
    B_, L, D = h.shape
    T = B_ * L
    xt = h.reshape(T, D)
    logits_g = (xt @ w_rg).astype(jnp.float32) + b_rg
    g_idx = jnp.argmax(logits_g, axis=-1)
    p_g = jnp.take_along_axis(jax.nn.softmax(logits_g, axis=-1), g_idx[:, None], axis=1)[:, 0]
    logits_e = ((xt @ w_re).astype(jnp.float32) + b_re).reshape(T, N_GROUPS, EXPERTS_PER_GROUP)
    le = jnp.take_along_axis(logits_e, g_idx[:, None, None], axis=1)[:, 0]
    top_v, top_i = lax.top_k(le, TOP_K)
    gates = (jax.nn.softmax(top_v, axis=-1) * p_g[:, None]).astype(xt.dtype)
    expert = (g_idx[:, None] * EXPERTS_PER_GROUP + top_i).astype(jnp.int32)

    A = T * TOP_K
    flat_e = expert.reshape(A)
    order = jnp.argsort(flat_e)
    e_sorted = flat_e[order]
    tok_sorted = (order // TOP_K).astype(jnp.int32)
    counts = jnp.bincount(flat_e, length=N_EXPERTS).astype(jnp.int32)
    start = jnp.cumsum(counts) - counts
    padded = ((counts + ROW_BLOCK - 1) // ROW_BLOCK) * ROW_BLOCK
    pend = jnp.cumsum(padded)
    pstart = pend - padded
    rank = jnp.arange(A, dtype=jnp.int32) - start[e_sorted]
    dest_sorted = pstart[e_sorted] + rank
    n_blocks = (A + N_EXPERTS * (ROW_BLOCK - 1) + ROW_BLOCK - 1) // ROW_BLOCK
    rows_tok = jnp.full((n_blocks * ROW_BLOCK,), T, jnp.int32).at[dest_sorted].set(tok_sorted)
    block_expert = jnp.minimum(
        jnp.searchsorted(pend, jnp.arange(n_blocks, dtype=jnp.int32) * ROW_BLOCK, side='right'),
        N_EXPERTS - 1).astype(jnp.int32)
    x_ext = jnp.concatenate([xt, jnp.zeros((1, D), xt.dtype)], axis=0)

    def expert_block(args):
        tok_idx, e = args
        xb = x_ext[tok_idx]
        gate, up = jnp.split(xb @ w13[e], 2, axis=-1)
        return (jax.nn.silu(gate) * up) @ w2[e]

    out = lax.map(expert_block, (rows_tok.reshape(n_blocks, ROW_BLOCK), block_expert))
    out = out.reshape(n_blocks * ROW_BLOCK, D)
    dest = jnp.zeros((A,), jnp.int32).at[order].set(dest_sorted)
    y = jnp.sum(out[dest].reshape(T, TOP_K, D) * gates[..., None], axis=1)
    return y.reshape(B_, L, D)


def setup_inputs(seed: int = 0) -> dict:
    key = jax.random.key(seed)
    ks = jax.random.split(key, 20)
    f32 = jnp.float32
    nrm = lambda k, s: jax.random.normal(k, s, f32)
    d = D_MODEL
    x = nrm(ks[0], (BATCH, SEQ, d))
    meta = nrm(ks[1], (N_META, d))
    ln_in_g = 1.0 + 0.02 * nrm(ks[2], (d,))
    ln_in_b = 0.02 * nrm(ks[3], (d,))
    w_in = nrm(ks[4], (DEPTH, d, D_IN_PROJ)) * d ** -0.5
    b_f = jnp.linspace(1.0, 6.0, FOX_HEADS, dtype=f32)[None, :] + 0.1 * nrm(ks[5], (DEPTH, FOX_HEADS))
    pool_w = nrm(ks[6], (DEPTH, N_POOL_GROUPS, POOL_GROUP_DIM, POOL_GROUP_DIM)) * POOL_GROUP_DIM ** -0.5
    pool_scale = 1.0 + 0.02 * nrm(ks[7], (DEPTH, D_POOL))
    w_out = nrm(ks[8], (DEPTH, D_MIX, d)) * (D_MIX ** -0.5 * DEEPNORM_BETA)
    ln_mix_g = 1.0 + 0.02 * nrm(ks[9], (DEPTH, d))
    ln_mix_b = 0.02 * nrm(ks[10], (DEPTH, d))
    w_router_g = nrm(ks[11], (DEPTH, d, N_GROUPS)) * d ** -0.5
    b_router_g = 0.01 * nrm(ks[12], (DEPTH, N_GROUPS))
    w_router_e = nrm(ks[13], (DEPTH, d, N_EXPERTS)) * d ** -0.5
    b_router_e = 0.01 * nrm(ks[14], (DEPTH, N_EXPERTS))
    w13 = nrm(ks[15], (DEPTH, N_EXPERTS, d, 2 * D_EXPERT)) * d ** -0.5
    w2 = nrm(ks[16], (DEPTH, N_EXPERTS, D_EXPERT, d)) * (D_EXPERT ** -0.5 * DEEPNORM_BETA)
    ln_ffn_g = 1.0 + 0.02 * nrm(ks[17], (DEPTH, d))
    ln_ffn_b = 0.02 * nrm(ks[18], (DEPTH, d))
    return {"x": x, "meta": meta, "ln_in_g": ln_in_g, "ln_in_b": ln_in_b,
            "w_in": w_in, "b_f": b_f, "pool_w": pool_w, "pool_scale": pool_scale,
            "w_out": w_out, "ln_mix_g": ln_mix_g, "ln_mix_b": ln_mix_b,
            "w_router_g": w_router_g, "b_router_g": b_router_g,
            "w_router_e": w_router_e, "b_router_e": b_router_e,
            "w13": w13, "w2": w2, "ln_ffn_g": ln_ffn_g, "ln_ffn_b": ln_ffn_b}


def reference(x, meta, ln_in_g, ln_in_b, w_in, b_f, pool_w, pool_scale, w_out,
              ln_mix_g, ln_mix_b, w_router_g, b_router_g, w_router_e, b_router_e,
              w13, w2, ln_ffn_g, ln_ffn_b):
    B_ = x.shape[0]
    meta_b = jnp.broadcast_to(meta[None].astype(x.dtype), (B_, N_META, D_MODEL))
    h = jnp.concatenate([meta_b, x], axis=1)
    h = layer_norm(h, ln_in_g, ln_in_b)
    for l in range(DEPTH):
        mix = hybrid_mixer(h, w_in[l], b_f[l], pool_w[l], pool_scale[l], w_out[l])
        h = layer_norm(DEEPNORM_ALPHA * h + mix, ln_mix_g[l], ln_mix_b[l])
        ffn = hierarchical_moe(h, w_router_g[l], b_router_g[l], w_router_e[l], b_router_e[l],
                               w13[l], w2[l])
        h = layer_norm(DEEPNORM_ALPHA * h + ffn, ln_ffn_g[l], ln_ffn_b[l])
    return h[:, N_META:]
```

```python
import functools

import jax
import jax.numpy as jnp
from jax import lax
from jax.experimental import pallas as pl
from jax.experimental.pallas import tpu as pltpu

F32 = jnp.float32
BF16 = jnp.bfloat16

N_META = 16
FOX_HEADS = 8
HEAD_DIM = 128
D_FOX = FOX_HEADS * HEAD_DIM
POOL_WINDOWS = (2, 4, 8, 16)
POOL_GROUP_DIM = 256
D_POOL = len(POOL_WINDOWS) * POOL_GROUP_DIM
N_GROUPS = 4
EXPERTS_PER_GROUP = 8
N_EXPERTS = N_GROUPS * EXPERTS_PER_GROUP
D_EXPERT = 1024
LN_EPS = 1e-5
DEEPNORM_ALPHA = 2.0 ** 0.25

LANES = 128
META_PAD = 128
MASKED_BIAS = 1e30
VMEM_LIMIT = 56 * 1024 * 1024


def _params(n_axes, vmem=VMEM_LIMIT):
    return pltpu.CompilerParams(dimension_semantics=("arbitrary",) * n_axes, vmem_limit_bytes=vmem)


def _layer_norm(x, g, b):
    mu = jnp.mean(x, axis=-1, keepdims=True)
    xc = x - mu
    var = jnp.mean(xc * xc, axis=-1, keepdims=True)
    return xc * lax.rsqrt(var + LN_EPS) * g + b


def _proj_kernel(x_ref, g_ref, b_ref, w_ref, bf_ref, qkv_ref, u_ref, d_ref, carry_ref, *, tiles_per_seq, chunk):
    i = pl.program_id(0)
    tm = x_ref.shape[0]
    xn = _layer_norm(x_ref[...], g_ref[...], b_ref[...]).astype(BF16)
    scale = HEAD_DIM ** -0.5
    for c in range(0, 3 * D_FOX, chunk):
        acc = jnp.dot(xn, w_ref[:, c:c + chunk], preferred_element_type=F32)
        if c < D_FOX:
            acc = acc * scale
        qkv_ref[:, c:c + chunk] = acc.astype(BF16)
    for c in range(0, D_POOL, chunk):
        u_ref[:, c:c + chunk] = jnp.dot(xn, w_ref[:, 3 * D_FOX + c:3 * D_FOX + c + chunk],
                                        preferred_element_type=F32)
    fl = jnp.dot(xn, w_ref[:, 3 * D_FOX + D_POOL:], preferred_element_type=F32) + bf_ref[...]
    lf = jnp.minimum(fl, 0.0) - jnp.log1p(jnp.exp(-jnp.abs(fl)))

    @pl.when(i % tiles_per_seq == 0)
    def _():
        carry_ref[...] = jnp.zeros_like(carry_ref)

    row = lax.broadcasted_iota(jnp.int32, lf.shape, 0)
    acc = lf
    k = 1
    while k < tm:
        acc = acc + jnp.where(row >= k, pltpu.roll(acc, k, 0), 0.0)
        k *= 2
    d = acc + carry_ref[...]
    d_ref[...] = d
    carry_ref[...] = d[tm - 1:tm, :]


def _proj(x2, g, b, w, bf, *, tm, tiles_per_seq):
    rows, d_model = x2.shape
    n_proj = w.shape[1]
    kern = functools.partial(_proj_kernel, tiles_per_seq=tiles_per_seq, chunk=512)
    return pl.pallas_call(
        kern,
        grid=(rows // tm,),
        in_specs=[
            pl.BlockSpec((tm, d_model), lambda i: (i, 0)),
            pl.BlockSpec((1, d_model), lambda i: (0, 0)),
            pl.BlockSpec((1, d_model), lambda i: (0, 0)),
            pl.BlockSpec((d_model, n_proj), lambda i: (0, 0)),
            pl.BlockSpec((1, LANES), lambda i: (0, 0)),
        ],
        out_specs=[
            pl.BlockSpec((tm, 3 * D_FOX), lambda i: (i, 0)),
            pl.BlockSpec((tm, D_POOL), lambda i: (i, 0)),
            pl.BlockSpec((tm, LANES), lambda i: (i, 0)),
        ],
        out_shape=[
            jax.ShapeDtypeStruct((rows, 3 * D_FOX), BF16),
            jax.ShapeDtypeStruct((rows, D_POOL), F32),
            jax.ShapeDtypeStruct((rows, LANES), F32),
        ],
        scratch_shapes=[pltpu.VMEM((1, LANES), F32)],
        compiler_params=_params(1),
        name="proj",
    )(x2, g, b, w, bf)


def _pool_kernel(u_ref, halo_ref, um_ref, pw_ref, ps_ref, o_ref, ext_ref):
    i = pl.program_id(1)
    tp = u_ref.shape[0]
    halo = jnp.where(i == 0, um_ref[...], halo_ref[...])
    ext_ref[0:N_META, :] = halo
    ext_ref[N_META:, :] = u_ref[...]
    for g, w in enumerate(POOL_WINDOWS):
        cols = slice(g * POOL_GROUP_DIM, (g + 1) * POOL_GROUP_DIM)
        tok = ext_ref[N_META:N_META + tp, cols]
        acc = tok
        for j in range(1, w):
            acc = acc + ext_ref[N_META - j:N_META - j + tp, cols]
        pooled = acc * (1.0 / w) - tok
        mixed = jnp.dot(pooled.astype(BF16), pw_ref[g], preferred_element_type=F32)
        o_ref[:, cols] = (mixed * ps_ref[:, cols]).astype(BF16)


def _pool(u, um, pw, ps, *, batch, seq, tp):
    tiles = seq // tp
    halo_blocks = tp // N_META
    return pl.pallas_call(
        _pool_kernel,
        grid=(batch, tiles),
        in_specs=[
            pl.BlockSpec((tp, D_POOL), lambda b, i: (b * tiles + i, 0)),
            pl.BlockSpec((N_META, D_POOL), lambda b, i: (jnp.maximum((b * tiles + i) * halo_blocks - 1, 0), 0)),
            pl.BlockSpec((N_META, D_POOL), lambda b, i: (0, 0)),
            pl.BlockSpec(pw.shape, lambda b, i: (0, 0, 0)),
            pl.BlockSpec((1, D_POOL), lambda b, i: (0, 0)),
        ],
        out_specs=pl.BlockSpec((tp, D_POOL), lambda b, i: (b * tiles + i, 0)),
        out_shape=jax.ShapeDtypeStruct((batch * seq, D_POOL), BF16),
        scratch_shapes=[pltpu.VMEM((N_META + tp, D_POOL), F32)],
        compiler_params=_params(2),
        name="pool",
    )(u, u, um, pw, ps)


def _attn_kernel(q_ref, k_ref, v_ref, km_ref, vm_ref, d_ref, dm_ref, o_ref, *, tk):
    qi = pl.program_id(2)
    tq = q_ref.shape[0]
    q = q_ref[...]
    nt = (((1,), (1,)), ((), ()))

    s = lax.dot_general(q, km_ref[...], nt, preferred_element_type=F32) - dm_ref[0]
    m = jnp.max(s, axis=-1, keepdims=True)
    p = jnp.exp(s - m)
    l = jnp.sum(p, axis=-1, keepdims=True)
    acc = jnp.dot(p.astype(BF16), vm_ref[...], preferred_element_type=F32)

    def step(j, carry, masked):
        m, l, acc = carry
        ks = pl.multiple_of(j * tk, tk)
        k = k_ref[pl.ds(ks, tk), :]
        v = v_ref[pl.ds(ks, tk), :]
        s = lax.dot_general(q, k, nt, preferred_element_type=F32) - d_ref[0, 0, :, pl.ds(ks, tk)]
        if masked:
            row = lax.broadcasted_iota(jnp.int32, s.shape, 0)
            col = lax.broadcasted_iota(jnp.int32, s.shape, 1)
            s = jnp.where(col <= row, s, -jnp.inf)
        m_new = jnp.maximum(m, jnp.max(s, axis=-1, keepdims=True))
        a = jnp.exp(m - m_new)
        p = jnp.exp(s - m_new)
        l = a * l + jnp.sum(p, axis=-1, keepdims=True)
        acc = a * acc + jnp.dot(p.astype(BF16), v, preferred_element_type=F32)
        return m_new, l, acc

    carry = lax.fori_loop(0, qi, lambda j, c: step(j, c, False), (m, l, acc))
    m, l, acc = step(qi, carry, True)
    o_ref[...] = (acc / l).astype(BF16)


def _attention(qkv, qkv_m, d_t, d_m, *, batch, seq, tq):
    nq = seq // tq
    kern = functools.partial(_attn_kernel, tk=tq)
    return pl.pallas_call(
        kern,
        grid=(batch, FOX_HEADS, nq),
        in_specs=[
            pl.BlockSpec((tq, HEAD_DIM), lambda b, h, i: (b * nq + i, h)),
            pl.BlockSpec((seq, HEAD_DIM), lambda b, h, i: (b, FOX_HEADS + h)),
            pl.BlockSpec((seq, HEAD_DIM), lambda b, h, i: (b, 2 * FOX_HEADS + h)),
            pl.BlockSpec((META_PAD, HEAD_DIM), lambda b, h, i: (0, FOX_HEADS + h)),
            pl.BlockSpec((META_PAD, HEAD_DIM), lambda b, h, i: (0, 2 * FOX_HEADS + h)),
            pl.BlockSpec((1, 1, 1, seq), lambda b, h, i: (b, h, 0, 0)),
            pl.BlockSpec((1, 1, META_PAD), lambda b, h, i: (h, 0, 0)),
        ],
        out_specs=pl.BlockSpec((tq, HEAD_DIM), lambda b, h, i: (b * nq + i, h)),
        out_shape=jax.ShapeDtypeStruct((batch * seq, D_FOX), BF16),
        compiler_params=_params(3),
        name="attn",
    )(qkv, qkv, qkv, qkv_m, qkv_m, d_t, d_m)


def _mix_kernel(x_ref, of_ref, op_ref, wo_ref, gi_ref, bi_ref, gm_ref, bm_ref, wr_ref, br_ref,
                h_ref, ri_ref, rg_ref, cnt_ref, carry_ref, *, chunk):
    i = pl.program_id(0)
    tm = x_ref.shape[0]
    d_model = x_ref.shape[1]
    h0 = _layer_norm(x_ref[...], gi_ref[...], bi_ref[...])
    of = of_ref[...]
    op = op_ref[...]
    for c in range(0, d_model, chunk):
        mix = jnp.dot(of, wo_ref[0:D_FOX, c:c + chunk], preferred_element_type=F32)
        mix = mix + jnp.dot(op, wo_ref[D_FOX:, c:c + chunk], preferred_element_type=F32)
        h_ref[:, c:c + chunk] = DEEPNORM_ALPHA * h0[:, c:c + chunk] + mix
    h1 = _layer_norm(h_ref[...], gm_ref[...], bm_ref[...])
    h_ref[...] = h1

    logits = jnp.dot(h1, wr_ref[...], preferred_element_type=F32, precision=lax.Precision.HIGHEST) + br_ref[...]
    lane = lax.broadcasted_iota(jnp.int32, logits.shape, 1)
    neg = -jnp.inf
    gl = jnp.where(lane < N_GROUPS, logits, neg)
    gmax = jnp.max(gl, axis=-1, keepdims=True)
    g_idx = jnp.min(jnp.where(gl == gmax, lane, LANES), axis=-1, keepdims=True)
    p_g = 1.0 / jnp.sum(jnp.exp(gl - gmax), axis=-1, keepdims=True)
    lo = N_GROUPS + g_idx * EXPERTS_PER_GROUP
    el = jnp.where((lane >= lo) & (lane < lo + EXPERTS_PER_GROUP), logits, neg)
    v1 = jnp.max(el, axis=-1, keepdims=True)
    i1 = jnp.min(jnp.where(el == v1, lane, LANES), axis=-1, keepdims=True)
    el2 = jnp.where(lane == i1, neg, el)
    v2 = jnp.max(el2, axis=-1, keepdims=True)
    i2 = jnp.min(jnp.where(el2 == v2, lane, LANES), axis=-1, keepdims=True)
    t = jnp.exp(v2 - v1)
    gate1 = p_g / (1.0 + t)
    gate2 = gate1 * t
    e1 = i1 - N_GROUPS
    e2 = i2 - N_GROUPS

    @pl.when(i == 0)
    def _():
        carry_ref[...] = jnp.zeros_like(carry_ref)

    r = lax.broadcasted_iota(jnp.int32, (tm, tm), 0)
    c = lax.broadcasted_iota(jnp.int32, (tm, tm), 1)
    lower = (c < r).astype(BF16)
    oh1 = (lane == e1).astype(F32)
    oh2 = (lane == e2).astype(F32)
    pre1 = jnp.dot(lower, oh1.astype(BF16), preferred_element_type=F32)
    pre2 = jnp.dot(lower, oh2.astype(BF16), preferred_element_type=F32)
    cnt1 = jnp.sum(oh1, axis=0, keepdims=True)
    cnt2 = jnp.sum(oh2, axis=0, keepdims=True)
    base = carry_ref[...]
    rank1 = jnp.sum((pre1 + base) * oh1, axis=-1, keepdims=True)
    rank2 = jnp.sum((pre2 + base + cnt1) * oh2, axis=-1, keepdims=True)
    total = base + cnt1 + cnt2
    carry_ref[...] = total
    cnt_ref[...] = total

    ri_ref[...] = jnp.where(lane == 0, e1, jnp.where(lane == 1, e2, jnp.where(
        lane == 2, rank1.astype(jnp.int32), jnp.where(lane == 3, rank2.astype(jnp.int32), 0))))
    rg_ref[...] = jnp.where(lane == 0, gate1, jnp.where(lane == 1, gate2, 0.0))


def _mix(x2, o_fox, o_pool, wo, gi, bi, gm, bm, wr, br, *, tm):
    rows, d_model = x2.shape
    kern = functools.partial(_mix_kernel, chunk=512)
    row_spec = lambda w: pl.BlockSpec((tm, w), lambda i: (i, 0))
    vec_spec = lambda w: pl.BlockSpec((1, w), lambda i: (0, 0))
    return pl.pallas_call(
        kern,
        grid=(rows // tm,),
        in_specs=[
            row_spec(d_model), row_spec(D_FOX), row_spec(D_POOL),
            pl.BlockSpec(wo.shape, lambda i: (0, 0)),
            vec_spec(d_model), vec_spec(d_model), vec_spec(d_model), vec_spec(d_model),
            pl.BlockSpec(wr.shape, lambda i: (0, 0)),
            vec_spec(LANES),
        ],
        out_specs=[row_spec(d_model), row_spec(LANES), row_spec(LANES), vec_spec(LANES)],
        out_shape=[
            jax.ShapeDtypeStruct((rows, d_model), F32),
            jax.ShapeDtypeStruct((rows, LANES), jnp.int32),
            jax.ShapeDtypeStruct((rows, LANES), F32),
            jax.ShapeDtypeStruct((1, LANES), F32),
        ],
        scratch_shapes=[pltpu.VMEM((1, LANES), F32)],
        compiler_params=_params(1),
        name="mix",
    )(x2, o_fox, o_pool, wo, gi, bi, gm, bm, wr, br)


def _moe_kernel(be_ref, na_ref, tok_ref, dst_ref, h_hbm, w13_ref, w2_ref, y_hbm, xbuf, obuf, gsem, ssem, *, chunk):
    i = pl.program_id(0)
    te = xbuf.shape[0]

    def gather_copy(r):
        return pltpu.make_async_copy(h_hbm.at[pl.ds(tok_ref[0, 0, r], 1)], xbuf.at[pl.ds(r, 1)], gsem)

    def scatter_copy(r):
        return pltpu.make_async_copy(obuf.at[pl.ds(r, 1)], y_hbm.at[pl.ds(dst_ref[0, 0, r], 1)], ssem)

    @pl.when(i < na_ref[0])
    def _():
        def g_start(r, _):
            gather_copy(r).start()
            return 0

        def g_wait(r, _):
            gather_copy(r).wait()
            return 0

        lax.fori_loop(0, te, g_start, 0)
        lax.fori_loop(0, te, g_wait, 0)

        x = xbuf[...].astype(BF16)
        out = None
        for c in range(0, D_EXPERT, chunk):
            gate = jnp.dot(x, w13_ref[0, :, c:c + chunk], preferred_element_type=F32)
            up = jnp.dot(x, w13_ref[0, :, D_EXPERT + c:D_EXPERT + c + chunk], preferred_element_type=F32)
            hid = (gate / (1.0 + jnp.exp(-gate)) * up).astype(BF16)
            part = jnp.dot(hid, w2_ref[0, c:c + chunk, :], preferred_element_type=F32)
            out = part if out is None else out + part
        obuf[...] = out

        def s_start(r, _):
            @pl.when(dst_ref[0, 0, r] >= 0)
            def _():
                scatter_copy(r).start()
            return 0

        def s_wait(r, _):
            @pl.when(dst_ref[0, 0, r] >= 0)
            def _():
                scatter_copy(r).wait()
            return 0

        lax.fori_loop(0, te, s_start, 0)
        lax.fori_loop(0, te, s_wait, 0)


def _moe(blk_e, n_act, rows_tok, rows_dst, h1, w13, w2, *, te):
    rows, d_model = h1.shape
    nb = rows_tok.shape[0]
    kern = functools.partial(_moe_kernel, chunk=256)
    smem_spec = pl.BlockSpec((1, 1, te), lambda i, be, na: (i, 0, 0), memory_space=pltpu.SMEM)
    return pl.pallas_call(
        kern,
        grid_spec=pltpu.PrefetchScalarGridSpec(
            num_scalar_prefetch=2,
            grid=(nb,),
            in_specs=[
                smem_spec, smem_spec,
                pl.BlockSpec(memory_space=pl.ANY),
                pl.BlockSpec((1, d_model, 2 * D_EXPERT), lambda i, be, na: (be[i], 0, 0)),
                pl.BlockSpec((1, D_EXPERT, d_model), lambda i, be, na: (be[i], 0, 0)),
            ],
            out_specs=pl.BlockSpec(memory_space=pl.ANY),
            scratch_shapes=[
                pltpu.VMEM((te, d_model), F32),
                pltpu.VMEM((te, d_model), F32),
                pltpu.SemaphoreType.DMA(()),
                pltpu.SemaphoreType.DMA(()),
            ],
        ),
        out_shape=jax.ShapeDtypeStruct((2 * rows, d_model), F32),
        compiler_params=_params(1),
        name="moe",
    )(blk_e, n_act, rows_tok, rows_dst, h1, w13, w2)


def _final_kernel(h_ref, y0_ref, y1_ref, rg_ref, g_ref, b_ref, o_ref):
    rg = rg_ref[...]
    ffn = y0_ref[...] * rg[:, 0:1] + y1_ref[...] * rg[:, 1:2]
    o_ref[...] = _layer_norm(DEEPNORM_ALPHA * h_ref[...] + ffn, g_ref[...], b_ref[...])


def _final(h1, y, rg, g, b, *, tm):
    rows, d_model = h1.shape
    nt = rows // tm
    return pl.pallas_call(
        _final_kernel,
        grid=(nt,),
        in_specs=[
            pl.BlockSpec((tm, d_model), lambda i: (i, 0)),
            pl.BlockSpec((tm, d_model), lambda i: (i, 0)),
            pl.BlockSpec((tm, d_model), lambda i: (nt + i, 0)),
            pl.BlockSpec((tm, LANES), lambda i: (i, 0)),
            pl.BlockSpec((1, d_model), lambda i: (0, 0)),
            pl.BlockSpec((1, d_model), lambda i: (0, 0)),
        ],
        out_specs=pl.BlockSpec((tm, d_model), lambda i: (i, 0)),
        out_shape=jax.ShapeDtypeStruct((rows, d_model), F32),
        compiler_params=_params(1),
        name="final",
    )(h1, y, y, rg, g, b)


def _dispatch_tables(ri, cnt, *, rows, te):
    experts = ri[:, 0:2]
    rank = ri[:, 2:4]
    counts = cnt[0, :N_EXPERTS].astype(jnp.int32)
    padded = ((counts + te - 1) // te) * te
    pend = jnp.cumsum(padded)
    pstart = pend - padded
    dest = (pstart[experts] + rank).reshape(-1)
    nb = (2 * rows + N_EXPERTS * (te - 1) + te - 1) // te
    tok = jnp.repeat(jnp.arange(rows, dtype=jnp.int32), 2)
    slot_row = tok + jnp.tile(jnp.array([0, rows], jnp.int32), rows)
    rows_tok = jnp.zeros((nb * te,), jnp.int32).at[dest].set(tok)
    rows_dst = jnp.full((nb * te,), -1, jnp.int32).at[dest].set(slot_row)
    blk_e = jnp.minimum(jnp.searchsorted(pend, jnp.arange(nb, dtype=jnp.int32) * te, side="right"),
                        N_EXPERTS - 1).astype(jnp.int32)
    n_act = (pend[-1] // te).astype(jnp.int32).reshape(1)
    return blk_e, n_act, rows_tok.reshape(nb, 1, te), rows_dst.reshape(nb, 1, te)


def kernel(x, meta, ln_in_g, ln_in_b, w_in, b_f, pool_w, pool_scale, w_out, ln_mix_g, ln_mix_b,
           w_router_g, b_router_g, w_router_e, b_router_e, w13, w2, ln_ffn_g, ln_ffn_b):
    batch, seq, d_model = x.shape
    rows = batch * seq
    tile = min(512, seq)
    assert seq % tile == 0 and tile % N_META == 0
    assert w_in.shape[0] == 1, "depth-1 trunk"

    x2 = x.reshape(rows, d_model)
    row = lambda v: v.reshape(1, -1).astype(F32)

    wi = w_in[0]
    w_f = jnp.pad(wi[:, 3 * D_FOX:3 * D_FOX + FOX_HEADS], ((0, 0), (0, LANES - FOX_HEADS)))
    w_proj = jnp.concatenate([wi[:, :3 * D_FOX], wi[:, 3 * D_FOX + FOX_HEADS:], w_f], axis=1).astype(BF16)
    bf = jnp.pad(b_f[0], (0, LANES - FOX_HEADS)).reshape(1, LANES)
    gi, bi = row(ln_in_g), row(ln_in_b)

    meta_pad = jnp.pad(meta.astype(F32), ((0, META_PAD - N_META), (0, 0)))
    qkv_m, u_m, c_m = _proj(meta_pad, gi, bi, w_proj, bf, tm=META_PAD, tiles_per_seq=1)
    d_meta = c_m[:N_META, :FOX_HEADS] - c_m[N_META - 1:N_META, :FOX_HEADS]
    d_m = jnp.pad(d_meta.T, ((0, 0), (0, META_PAD - N_META)), constant_values=MASKED_BIAS)
    d_m = d_m.reshape(FOX_HEADS, 1, META_PAD)

    qkv, u, d = _proj(x2, gi, bi, w_proj, bf, tm=tile, tiles_per_seq=seq // tile)
    d_t = d.reshape(batch, seq, LANES)[:, :, :FOX_HEADS].transpose(0, 2, 1).reshape(batch, FOX_HEADS, 1, seq)

    o_pool = _pool(u, u_m[:N_META], pool_w[0].astype(BF16), row(pool_scale[0]), batch=batch, seq=seq, tp=tile)
    o_fox = _attention(qkv, qkv_m, d_t, d_m, batch=batch, seq=seq, tq=tile)

    w_r = jnp.pad(jnp.concatenate([w_router_g[0], w_router_e[0]], axis=1),
                  ((0, 0), (0, LANES - N_GROUPS - N_EXPERTS)))
    b_r = jnp.pad(jnp.concatenate([b_router_g[0], b_router_e[0]]), (0, LANES - N_GROUPS - N_EXPERTS)).reshape(1, LANES)
    h1, ri, rg, cnt = _mix(x2, o_fox, o_pool, w_out[0].astype(BF16), gi, bi, row(ln_mix_g[0]), row(ln_mix_b[0]),
                           w_r, b_r, tm=min(256, seq))

    te = 256
    blk_e, n_act, rows_tok, rows_dst = _dispatch_tables(ri, cnt, rows=rows, te=te)
    y = _moe(blk_e, n_act, rows_tok, rows_dst, h1, w13[0].astype(BF16), w2[0].astype(BF16), te=te)

    out = _final(h1, y, rg, row(ln_ffn_g[0]), row(ln_ffn_b[0]), tm=tile)
    return out.reshape(batch, seq, d_model)
```

```python
import functools

import jax
import jax.numpy as jnp
from jax import lax
from jax.experimental import pallas as pl
from jax.experimental.pallas import tpu as pltpu

F32 = jnp.float32
BF16 = jnp.bfloat16

N_META = 16
FOX_HEADS = 8
HEAD_DIM = 128
D_FOX = FOX_HEADS * HEAD_DIM
POOL_WINDOWS = (2, 4, 8, 16)
POOL_GROUP_DIM = 256
D_POOL = len(POOL_WINDOWS) * POOL_GROUP_DIM
N_GROUPS = 4
EXPERTS_PER_GROUP = 8
N_EXPERTS = N_GROUPS * EXPERTS_PER_GROUP
D_EXPERT = 1024
LN_EPS = 1e-5
DEEPNORM_ALPHA = 2.0 ** 0.25

LANES = 128
META_PAD = 128
MASKED_BIAS = 1e30
VMEM_LIMIT = 56 * 1024 * 1024


def _params(n_axes, vmem=VMEM_LIMIT):
    return pltpu.CompilerParams(dimension_semantics=("arbitrary",) * n_axes, vmem_limit_bytes=vmem)


def _layer_norm(x, g, b):
    mu = jnp.mean(x, axis=-1, keepdims=True)
    xc = x - mu
    var = jnp.mean(xc * xc, axis=-1, keepdims=True)
    return xc * lax.rsqrt(var + LN_EPS) * g + b


def _proj_kernel(x_ref, g_ref, b_ref, w_ref, bf_ref, qkv_ref, u_ref, d_ref, carry_ref, *, tiles_per_seq, chunk):
    i = pl.program_id(0)
    tm = x_ref.shape[0]
    xn = _layer_norm(x_ref[...], g_ref[...], b_ref[...]).astype(BF16)
    scale = HEAD_DIM ** -0.5
    for c in range(0, 3 * D_FOX, chunk):
        acc = jnp.dot(xn, w_ref[:, c:c + chunk], preferred_element_type=F32)
        if c < D_FOX:
            acc = acc * scale
        qkv_ref[:, c:c + chunk] = acc.astype(BF16)
    for c in range(0, D_POOL, chunk):
        u_ref[:, c:c + chunk] = jnp.dot(xn, w_ref[:, 3 * D_FOX + c:3 * D_FOX + c + chunk],
                                        preferred_element_type=F32)
    fl = jnp.dot(xn, w_ref[:, 3 * D_FOX + D_POOL:], preferred_element_type=F32) + bf_ref[...]
    lf = jnp.minimum(fl, 0.0) - jnp.log1p(jnp.exp(-jnp.abs(fl)))

    @pl.when(i % tiles_per_seq == 0)
    def _():
        carry_ref[...] = jnp.zeros_like(carry_ref)

    row = lax.broadcasted_iota(jnp.int32, lf.shape, 0)
    acc = lf
    k = 1
    while k < tm:
        acc = acc + jnp.where(row >= k, pltpu.roll(acc, k, 0), 0.0)
        k *= 2
    d = acc + carry_ref[...]
    d_ref[...] = d
    carry_ref[...] = d[tm - 1:tm, :]


def _proj(x2, g, b, w, bf, *, tm, tiles_per_seq):
    rows, d_model = x2.shape
    n_proj = w.shape[1]
    kern = functools.partial(_proj_kernel, tiles_per_seq=tiles_per_seq, chunk=512)
    return pl.pallas_call(
        kern,
        grid=(rows // tm,),
        in_specs=[
            pl.BlockSpec((tm, d_model), lambda i: (i, 0)),
            pl.BlockSpec((1, d_model), lambda i: (0, 0)),
            pl.BlockSpec((1, d_model), lambda i: (0, 0)),
            pl.BlockSpec((d_model, n_proj), lambda i: (0, 0)),
            pl.BlockSpec((1, LANES), lambda i: (0, 0)),
        ],
        out_specs=[
            pl.BlockSpec((tm, 3 * D_FOX), lambda i: (i, 0)),
            pl.BlockSpec((tm, D_POOL), lambda i: (i, 0)),
            pl.BlockSpec((tm, LANES), lambda i: (i, 0)),
        ],
        out_shape=[
            jax.ShapeDtypeStruct((rows, 3 * D_FOX), BF16),
            jax.ShapeDtypeStruct((rows, D_POOL), F32),
            jax.ShapeDtypeStruct((rows, LANES), F32),
        ],
        scratch_shapes=[pltpu.VMEM((1, LANES), F32)],
        compiler_params=_params(1),
        name="proj",
    )(x2, g, b, w, bf)


def _pool_kernel(u_ref, halo_ref, um_ref, pw_ref, ps_ref, o_ref, ext_ref):
    i = pl.program_id(1)
    tp = u_ref.shape[0]
    halo = jnp.where(i == 0, um_ref[...], halo_ref[...])
    ext_ref[0:N_META, :] = halo
    ext_ref[N_META:, :] = u_ref[...]
    for g, w in enumerate(POOL_WINDOWS):
        cols = slice(g * POOL_GROUP_DIM, (g + 1) * POOL_GROUP_DIM)
        tok = ext_ref[N_META:N_META + tp, cols]
        acc = tok
        for j in range(1, w):
            acc = acc + ext_ref[N_META - j:N_META - j + tp, cols]
        pooled = acc * (1.0 / w) - tok
        mixed = jnp.dot(pooled.astype(BF16), pw_ref[g], preferred_element_type=F32)
        o_ref[:, cols] = (mixed * ps_ref[:, cols]).astype(BF16)


def _pool(u, um, pw, ps, *, batch, seq, tp):
    tiles = seq // tp
    halo_blocks = tp // N_META
    return pl.pallas_call(
        _pool_kernel,
        grid=(batch, tiles),
        in_specs=[
            pl.BlockSpec((tp, D_POOL), lambda b, i: (b * tiles + i, 0)),
            pl.BlockSpec((N_META, D_POOL), lambda b, i: (jnp.maximum((b * tiles + i) * halo_blocks - 1, 0), 0)),
            pl.BlockSpec((N_META, D_POOL), lambda b, i: (0, 0)),
            pl.BlockSpec(pw.shape, lambda b, i: (0, 0, 0)),
            pl.BlockSpec((1, D_POOL), lambda b, i: (0, 0)),
        ],
        out_specs=pl.BlockSpec((tp, D_POOL), lambda b, i: (b * tiles + i, 0)),
        out_shape=jax.ShapeDtypeStruct((batch * seq, D_POOL), BF16),
        scratch_shapes=[pltpu.VMEM((N_META + tp, D_POOL), F32)],
        compiler_params=_params(2),
        name="pool",
    )(u, u, um, pw, ps)


def _attn_kernel(q_ref, k_ref, v_ref, km_ref, vm_ref, d_ref, dm_ref, o_ref, *, tk):
    qi = pl.program_id(2)
    tq = q_ref.shape[0]
    q = q_ref[...]
    nt = (((1,), (1,)), ((), ()))

    s = lax.dot_general(q, km_ref[...], nt, preferred_element_type=F32) - dm_ref[0]
    m = jnp.max(s, axis=-1, keepdims=True)
    p = jnp.exp(s - m)
    l = jnp.sum(p, axis=-1, keepdims=True)
    acc = jnp.dot(p.astype(BF16), vm_ref[...], preferred_element_type=F32)

    def step(j, carry, masked):
        m, l, acc = carry
        ks = pl.multiple_of(j * tk, tk)
        k = k_ref[pl.ds(ks, tk), :]
        v = v_ref[pl.ds(ks, tk), :]
        s = lax.dot_general(q, k, nt, preferred_element_type=F32) - d_ref[0, 0, :, pl.ds(ks, tk)]
        if masked:
            row = lax.broadcasted_iota(jnp.int32, s.shape, 0)
            col = lax.broadcasted_iota(jnp.int32, s.shape, 1)
            s = jnp.where(col <= row, s, -jnp.inf)
        m_new = jnp.maximum(m, jnp.max(s, axis=-1, keepdims=True))
        a = jnp.exp(m - m_new)
        p = jnp.exp(s - m_new)
        l = a * l + jnp.sum(p, axis=-1, keepdims=True)
        acc = a * acc + jnp.dot(p.astype(BF16), v, preferred_element_type=F32)
        return m_new, l, acc

    carry = lax.fori_loop(0, qi, lambda j, c: step(j, c, False), (m, l, acc))
    m, l, acc = step(qi, carry, True)
    o_ref[...] = (acc / l).astype(BF16)


def _attention(qkv, qkv_m, d_t, d_m, *, batch, seq, tq):
    nq = seq // tq
    kern = functools.partial(_attn_kernel, tk=tq)
    return pl.pallas_call(
        kern,
        grid=(batch, FOX_HEADS, nq),
        in_specs=[
            pl.BlockSpec((tq, HEAD_DIM), lambda b, h, i: (b * nq + i, h)),
            pl.BlockSpec((seq, HEAD_DIM), lambda b, h, i: (b, FOX_HEADS + h)),
            pl.BlockSpec((seq, HEAD_DIM), lambda b, h, i: (b, 2 * FOX_HEADS + h)),
            pl.BlockSpec((META_PAD, HEAD_DIM), lambda b, h, i: (0, FOX_HEADS + h)),
            pl.BlockSpec((META_PAD, HEAD_DIM), lambda b, h, i: (0, 2 * FOX_HEADS + h)),
            pl.BlockSpec((1, 1, 1, seq), lambda b, h, i: (b, h, 0, 0)),
            pl.BlockSpec((1, 1, META_PAD), lambda b, h, i: (h, 0, 0)),
        ],
        out_specs=pl.BlockSpec((tq, HEAD_DIM), lambda b, h, i: (b * nq + i, h)),
        out_shape=jax.ShapeDtypeStruct((batch * seq, D_FOX), BF16),
        compiler_params=_params(3),
        name="attn",
    )(qkv, qkv, qkv, qkv_m, qkv_m, d_t, d_m)


def _mix_kernel(x_ref, of_ref, op_ref, wo_ref, gi_ref, bi_ref, gm_ref, bm_ref, wr_ref, br_ref,
                h_ref, ri_ref, rg_ref, cnt_ref, carry_ref, *, chunk):
    i = pl.program_id(0)
    tm = x_ref.shape[0]
    d_model = x_ref.shape[1]
    h0 = _layer_norm(x_ref[...], gi_ref[...], bi_ref[...])
    of = of_ref[...]
    op = op_ref[...]
    for c in range(0, d_model, chunk):
        mix = jnp.dot(of, wo_ref[0:D_FOX, c:c + chunk], preferred_element_type=F32)
        mix = mix + jnp.dot(op, wo_ref[D_FOX:, c:c + chunk], preferred_element_type=F32)
        h_ref[:, c:c + chunk] = DEEPNORM_ALPHA * h0[:, c:c + chunk] + mix
    h1 = _layer_norm(h_ref[...], gm_ref[...], bm_ref[...])
    h_ref[...] = h1

    logits = jnp.dot(h1, wr_ref[...], preferred_element_type=F32, precision=lax.Precision.HIGHEST) + br_ref[...]
    lane = lax.broadcasted_iota(jnp.int32, logits.shape, 1)
    neg = -jnp.inf
    gl = jnp.where(lane < N_GROUPS, logits, neg)
    gmax = jnp.max(gl, axis=-1, keepdims=True)
    g_idx = jnp.min(jnp.where(gl == gmax, lane, LANES), axis=-1, keepdims=True)
    p_g = 1.0 / jnp.sum(jnp.exp(gl - gmax), axis=-1, keepdims=True)
    lo = N_GROUPS + g_idx * EXPERTS_PER_GROUP
    el = jnp.where((lane >= lo) & (lane < lo + EXPERTS_PER_GROUP), logits, neg)
    v1 = jnp.max(el, axis=-1, keepdims=True)
    i1 = jnp.min(jnp.where(el == v1, lane, LANES), axis=-1, keepdims=True)
    el2 = jnp.where(lane == i1, neg, el)
    v2 = jnp.max(el2, axis=-1, keepdims=True)
    i2 = jnp.min(jnp.where(el2 == v2, lane, LANES), axis=-1, keepdims=True)
    t = jnp.exp(v2 - v1)
    gate1 = p_g / (1.0 + t)
    gate2 = gate1 * t
    e1 = i1 - N_GROUPS
    e2 = i2 - N_GROUPS

    @pl.when(i == 0)
    def _():
        carry_ref[...] = jnp.zeros_like(carry_ref)

    r = lax.broadcasted_iota(jnp.int32, (tm, tm), 0)
    c = lax.broadcasted_iota(jnp.int32, (tm, tm), 1)
    lower = (c < r).astype(BF16)
    oh1 = (lane == e1).astype(F32)
    oh2 = (lane == e2).astype(F32)
    pre1 = jnp.dot(lower, oh1.astype(BF16), preferred_element_type=F32)
    pre2 = jnp.dot(lower, oh2.astype(BF16), preferred_element_type=F32)
    cnt1 = jnp.sum(oh1, axis=0, keepdims=True)
    cnt2 = jnp.sum(oh2, axis=0, keepdims=True)
    base = carry_ref[...]
    rank1 = jnp.sum((pre1 + base) * oh1, axis=-1, keepdims=True)
    rank2 = jnp.sum((pre2 + base + cnt1) * oh2, axis=-1, keepdims=True)
    total = base + cnt1 + cnt2
    carry_ref[...] = total
    cnt_ref[...] = total

    ri_ref[...] = jnp.where(lane == 0, e1, jnp.where(lane == 1, e2, jnp.where(
        lane == 2, rank1.astype(jnp.int32), jnp.where(lane == 3, rank2.astype(jnp.int32), 0))))
    rg_ref[...] = jnp.where(lane == 0, gate1, jnp.where(lane == 1, gate2, 0.0))


def _mix(x2, o_fox, o_pool, wo, gi, bi, gm, bm, wr, br, *, tm):
    rows, d_model = x2.shape
    kern = functools.partial(_mix_kernel, chunk=512)
    row_spec = lambda w: pl.BlockSpec((tm, w), lambda i: (i, 0))
    vec_spec = lambda w: pl.BlockSpec((1, w), lambda i: (0, 0))
    return pl.pallas_call(
        kern,
        grid=(rows // tm,),
        in_specs=[
            row_spec(d_model), row_spec(D_FOX), row_spec(D_POOL),
            pl.BlockSpec(wo.shape, lambda i: (0, 0)),
            vec_spec(d_model), vec_spec(d_model), vec_spec(d_model), vec_spec(d_model),
            pl.BlockSpec(wr.shape, lambda i: (0, 0)),
            vec_spec(LANES),
        ],
        out_specs=[row_spec(d_model), row_spec(LANES), row_spec(LANES), vec_spec(LANES)],
        out_shape=[
            jax.ShapeDtypeStruct((rows, d_model), F32),
            jax.ShapeDtypeStruct((rows, LANES), jnp.int32),
            jax.ShapeDtypeStruct((rows, LANES), F32),
            jax.ShapeDtypeStruct((1, LANES), F32),
        ],
        scratch_shapes=[pltpu.VMEM((1, LANES), F32)],
        compiler_params=_params(1),
        name="mix",
    )(x2, o_fox, o_pool, wo, gi, bi, gm, bm, wr, br)


def _moe_kernel(be_ref, na_ref, tok_ref, tokn_ref, dst_ref, h_hbm, w13_ref, w2_ref, y_hbm, xbuf, obuf, gsem, ssem,
                *, chunk, unroll):
    i = pl.program_id(0)
    n_act = na_ref[0]
    te = xbuf.shape[1]
    slot = i % 2

    def start_gather(idx_ref, s):
        def body(r, _):
            pltpu.make_async_copy(h_hbm.at[pl.ds(idx_ref[0, 0, r], 1)], xbuf.at[s, pl.ds(r, 1)], gsem.at[s]).start()
            return 0
        lax.fori_loop(0, te, body, 0, unroll=unroll)

    def wait_gather(s):
        pltpu.make_async_copy(h_hbm.at[pl.ds(0, te)], xbuf.at[s], gsem.at[s]).wait()

    def start_scatter(s):
        def body(r, _):
            pltpu.make_async_copy(obuf.at[s, pl.ds(r, 1)], y_hbm.at[pl.ds(dst_ref[0, 0, r], 1)], ssem.at[s]).start()
            return 0
        lax.fori_loop(0, te, body, 0, unroll=unroll)

    def wait_scatter(s):
        pltpu.make_async_copy(obuf.at[s], y_hbm.at[pl.ds(0, te)], ssem.at[s]).wait()

    @pl.when(i < n_act)
    def _():
        @pl.when(i == 0)
        def _():
            start_gather(tok_ref, slot)
            spare = y_hbm.shape[0] - 2 * te
            obuf[...] = jnp.zeros_like(obuf)
            for s in range(2):
                pltpu.make_async_copy(obuf.at[s], y_hbm.at[pl.ds(spare + s * te, te)], ssem.at[s]).start()
            for s in range(2):
                pltpu.make_async_copy(obuf.at[s], y_hbm.at[pl.ds(spare + s * te, te)], ssem.at[s]).wait()

        wait_gather(slot)

        @pl.when(i + 1 < n_act)
        def _():
            start_gather(tokn_ref, 1 - slot)

        @pl.when(i >= 2)
        def _():
            wait_scatter(slot)

        x = xbuf[slot].astype(BF16)
        out = None
        for c in range(0, D_EXPERT, chunk):
            gate = jnp.dot(x, w13_ref[0, :, c:c + chunk], preferred_element_type=F32)
            up = jnp.dot(x, w13_ref[0, :, D_EXPERT + c:D_EXPERT + c + chunk], preferred_element_type=F32)
            hid = (gate / (1.0 + jnp.exp(-gate)) * up).astype(BF16)
            part = jnp.dot(hid, w2_ref[0, c:c + chunk, :], preferred_element_type=F32)
            out = part if out is None else out + part
        obuf[slot] = out
        start_scatter(slot)

        @pl.when(i == n_act - 1)
        def _():
            @pl.when(i >= 1)
            def _():
                wait_scatter(1 - slot)
            wait_scatter(slot)


def _moe(blk_e, n_act, rows_tok, rows_dst, h1, w13, w2, *, te):
    rows, d_model = h1.shape
    nb = rows_tok.shape[0]
    kern = functools.partial(_moe_kernel, chunk=256, unroll=8)
    smem_spec = lambda off: pl.BlockSpec((1, 1, te), lambda i, be, na: (jnp.minimum(i + off, nb - 1), 0, 0),
                                         memory_space=pltpu.SMEM)
    return pl.pallas_call(
        kern,
        grid_spec=pltpu.PrefetchScalarGridSpec(
            num_scalar_prefetch=2,
            grid=(nb,),
            in_specs=[
                smem_spec(0), smem_spec(1), smem_spec(0),
                pl.BlockSpec(memory_space=pl.ANY),
                pl.BlockSpec((1, d_model, 2 * D_EXPERT), lambda i, be, na: (be[i], 0, 0)),
                pl.BlockSpec((1, D_EXPERT, d_model), lambda i, be, na: (be[i], 0, 0)),
            ],
            out_specs=pl.BlockSpec(memory_space=pl.ANY),
            scratch_shapes=[
                pltpu.VMEM((2, te, d_model), F32),
                pltpu.VMEM((2, te, d_model), F32),
                pltpu.SemaphoreType.DMA((2,)),
                pltpu.SemaphoreType.DMA((2,)),
            ],
        ),
        out_shape=jax.ShapeDtypeStruct((2 * rows + 2 * te, d_model), F32),
        compiler_params=_params(1),
        name="moe",
    )(blk_e, n_act, rows_tok, rows_tok, rows_dst, h1, w13, w2)


def _final_kernel(h_ref, y0_ref, y1_ref, rg_ref, g_ref, b_ref, o_ref):
    rg = rg_ref[...]
    ffn = y0_ref[...] * rg[:, 0:1] + y1_ref[...] * rg[:, 1:2]
    o_ref[...] = _layer_norm(DEEPNORM_ALPHA * h_ref[...] + ffn, g_ref[...], b_ref[...])


def _final(h1, y, rg, g, b, *, tm):
    rows, d_model = h1.shape
    nt = rows // tm
    return pl.pallas_call(
        _final_kernel,
        grid=(nt,),
        in_specs=[
            pl.BlockSpec((tm, d_model), lambda i: (i, 0)),
            pl.BlockSpec((tm, d_model), lambda i: (i, 0)),
            pl.BlockSpec((tm, d_model), lambda i: (nt + i, 0)),
            pl.BlockSpec((tm, LANES), lambda i: (i, 0)),
            pl.BlockSpec((1, d_model), lambda i: (0, 0)),
            pl.BlockSpec((1, d_model), lambda i: (0, 0)),
        ],
        out_specs=pl.BlockSpec((tm, d_model), lambda i: (i, 0)),
        out_shape=jax.ShapeDtypeStruct((rows, d_model), F32),
        compiler_params=_params(1),
        name="final",
    )(h1, y, y, rg, g, b)


def _dispatch_tables(ri, cnt, *, rows, te):
    experts = ri[:, 0:2]
    rank = ri[:, 2:4]
    counts = cnt[0, :N_EXPERTS].astype(jnp.int32)
    padded = ((counts + te - 1) // te) * te
    pend = jnp.cumsum(padded)
    pstart = pend - padded
    dest = (pstart[experts] + rank).reshape(-1)
    nb = (2 * rows + N_EXPERTS * (te - 1) + te - 1) // te
    tok = jnp.repeat(jnp.arange(rows, dtype=jnp.int32), 2)
    out_row = tok + jnp.tile(jnp.array([0, rows], jnp.int32), rows)
    pos = jnp.arange(nb * te, dtype=jnp.int32)
    rows_dst = (2 * rows + pos % (2 * te)).at[dest].set(out_row)
    rows_tok = jnp.where(rows_dst < rows, rows_dst, jnp.where(rows_dst < 2 * rows, rows_dst - rows, 0))
    blk_e = jnp.minimum(jnp.sum(pend[None, :] <= (jnp.arange(nb, dtype=jnp.int32) * te)[:, None], axis=1),
                        N_EXPERTS - 1).astype(jnp.int32)
    n_act = (pend[-1] // te).astype(jnp.int32).reshape(1)
    return blk_e, n_act, rows_tok.reshape(nb, 1, te), rows_dst.reshape(nb, 1, te)


def kernel(x, meta, ln_in_g, ln_in_b, w_in, b_f, pool_w, pool_scale, w_out, ln_mix_g, ln_mix_b,
           w_router_g, b_router_g, w_router_e, b_router_e, w13, w2, ln_ffn_g, ln_ffn_b):
    batch, seq, d_model = x.shape
    rows = batch * seq
    tile = min(512, seq)
    assert seq % tile == 0 and tile % N_META == 0
    assert w_in.shape[0] == 1, "depth-1 trunk"

    x2 = x.reshape(rows, d_model)
    row = lambda v: v.reshape(1, -1).astype(F32)

    wi = w_in[0]
    w_f = jnp.pad(wi[:, 3 * D_FOX:3 * D_FOX + FOX_HEADS], ((0, 0), (0, LANES - FOX_HEADS)))
    w_proj = jnp.concatenate([wi[:, :3 * D_FOX], wi[:, 3 * D_FOX + FOX_HEADS:], w_f], axis=1).astype(BF16)
    bf = jnp.pad(b_f[0], (0, LANES - FOX_HEADS)).reshape(1, LANES)
    gi, bi = row(ln_in_g), row(ln_in_b)

    meta_pad = jnp.pad(meta.astype(F32), ((0, META_PAD - N_META), (0, 0)))
    qkv_m, u_m, c_m = _proj(meta_pad, gi, bi, w_proj, bf, tm=META_PAD, tiles_per_seq=1)
    d_meta = c_m[:N_META, :FOX_HEADS] - c_m[N_META - 1:N_META, :FOX_HEADS]
    d_m = jnp.pad(d_meta.T, ((0, 0), (0, META_PAD - N_META)), constant_values=MASKED_BIAS)
    d_m = d_m.reshape(FOX_HEADS, 1, META_PAD)

    qkv, u, d = _proj(x2, gi, bi, w_proj, bf, tm=tile, tiles_per_seq=seq // tile)
    d_t = d.reshape(batch, seq, LANES)[:, :, :FOX_HEADS].transpose(0, 2, 1).reshape(batch, FOX_HEADS, 1, seq)

    o_pool = _pool(u, u_m[:N_META], pool_w[0].astype(BF16), row(pool_scale[0]), batch=batch, seq=seq, tp=tile)
    o_fox = _attention(qkv, qkv_m, d_t, d_m, batch=batch, seq=seq, tq=tile)

    w_r = jnp.pad(jnp.concatenate([w_router_g[0], w_router_e[0]], axis=1),
                  ((0, 0), (0, LANES - N_GROUPS - N_EXPERTS)))
    b_r = jnp.pad(jnp.concatenate([b_router_g[0], b_router_e[0]]), (0, LANES - N_GROUPS - N_EXPERTS)).reshape(1, LANES)
    h1, ri, rg, cnt = _mix(x2, o_fox, o_pool, w_out[0].astype(BF16), gi, bi, row(ln_mix_g[0]), row(ln_mix_b[0]),
                           w_r, b_r, tm=min(256, seq))

    te = 256
    blk_e, n_act, rows_tok, rows_dst = _dispatch_tables(ri, cnt, rows=rows, te=te)
    y = _moe(blk_e, n_act, rows_tok, rows_dst, h1, w13[0].astype(BF16), w2[0].astype(BF16), te=te)

    out = _final(h1, y, rg, row(ln_ffn_g[0]), row(ln_ffn_b[0]), tm=tile)
    return out.reshape(batch, seq, d_model)
```

```python
import functools

import jax
import jax.numpy as jnp
from jax import lax
from jax.experimental import pallas as pl
from jax.experimental.pallas import tpu as pltpu

F32 = jnp.float32
BF16 = jnp.bfloat16

N_META = 16
FOX_HEADS = 8
HEAD_DIM = 128
D_FOX = FOX_HEADS * HEAD_DIM
POOL_WINDOWS = (2, 4, 8, 16)
POOL_GROUP_DIM = 256
D_POOL = len(POOL_WINDOWS) * POOL_GROUP_DIM
N_GROUPS = 4
EXPERTS_PER_GROUP = 8
N_EXPERTS = N_GROUPS * EXPERTS_PER_GROUP
D_EXPERT = 1024
LN_EPS = 1e-5
DEEPNORM_ALPHA = 2.0 ** 0.25

LANES = 128
META_PAD = 128
MASKED_BIAS = 1e30
LOG2E = 1.4426950408889634
BIAS_TERMS = 3
VMEM_LIMIT = 56 * 1024 * 1024


def _params(n_axes, vmem=VMEM_LIMIT):
    return pltpu.CompilerParams(dimension_semantics=("arbitrary",) * n_axes, vmem_limit_bytes=vmem)


def _layer_norm(x, g, b):
    mu = jnp.mean(x, axis=-1, keepdims=True)
    xc = x - mu
    var = jnp.mean(xc * xc, axis=-1, keepdims=True)
    return xc * lax.rsqrt(var + LN_EPS) * g + b


def _proj_kernel(x_ref, g_ref, b_ref, w_ref, bf_ref, sel_ref, qkv_ref, u_ref, d_ref, ka_ref, carry_ref,
                 *, tiles_per_seq, chunk):
    i = pl.program_id(0)
    tm = x_ref.shape[0]
    xn = _layer_norm(x_ref[...], g_ref[...], b_ref[...]).astype(BF16)
    scale = HEAD_DIM ** -0.5 * LOG2E
    for c in range(0, 3 * D_FOX, chunk):
        acc = jnp.dot(xn, w_ref[:, c:c + chunk], preferred_element_type=F32)
        if c < D_FOX:
            acc = acc * scale
        qkv_ref[:, c:c + chunk] = acc.astype(BF16)
    for c in range(0, D_POOL, chunk):
        u_ref[:, c:c + chunk] = jnp.dot(xn, w_ref[:, 3 * D_FOX + c:3 * D_FOX + c + chunk],
                                        preferred_element_type=F32)
    fl = jnp.dot(xn, w_ref[:, 3 * D_FOX + D_POOL:], preferred_element_type=F32) + bf_ref[...]
    lf = jnp.minimum(fl, 0.0) - jnp.log1p(jnp.exp(-jnp.abs(fl)))

    @pl.when(i % tiles_per_seq == 0)
    def _():
        carry_ref[...] = jnp.zeros_like(carry_ref)

    row = lax.broadcasted_iota(jnp.int32, lf.shape, 0)
    acc = lf
    k = 1
    while k < tm:
        acc = acc + jnp.where(row >= k, pltpu.roll(acc, k, 0), 0.0)
        k *= 2
    d = acc + carry_ref[...]
    d_ref[...] = d
    carry_ref[...] = d[tm - 1:tm, :]

    nd = d * (-LOG2E)
    hi = nd.astype(BF16)
    r1 = nd - hi.astype(F32)
    mid = r1.astype(BF16)
    lo = (r1 - mid.astype(F32)).astype(BF16)
    split = jnp.concatenate([hi, mid, lo], axis=1)
    ka_ref[...] = jnp.dot(split, sel_ref[...], preferred_element_type=F32).astype(BF16)


def _bias_placement():
    t, h = jnp.meshgrid(jnp.arange(BIAS_TERMS), jnp.arange(FOX_HEADS), indexing="ij")
    sel = jnp.zeros((BIAS_TERMS * LANES, D_FOX), F32)
    return sel.at[(t * LANES + h).ravel(), (h * HEAD_DIM + t).ravel()].set(1.0).astype(BF16)


def _proj(x2, g, b, w, bf, sel, *, tm, tiles_per_seq):
    rows, d_model = x2.shape
    n_proj = w.shape[1]
    kern = functools.partial(_proj_kernel, tiles_per_seq=tiles_per_seq, chunk=512)
    return pl.pallas_call(
        kern,
        grid=(rows // tm,),
        in_specs=[
            pl.BlockSpec((tm, d_model), lambda i: (i, 0)),
            pl.BlockSpec((1, d_model), lambda i: (0, 0)),
            pl.BlockSpec((1, d_model), lambda i: (0, 0)),
            pl.BlockSpec((d_model, n_proj), lambda i: (0, 0)),
            pl.BlockSpec((1, LANES), lambda i: (0, 0)),
            pl.BlockSpec(sel.shape, lambda i: (0, 0)),
        ],
        out_specs=[
            pl.BlockSpec((tm, 3 * D_FOX), lambda i: (i, 0)),
            pl.BlockSpec((tm, D_POOL), lambda i: (i, 0)),
            pl.BlockSpec((tm, LANES), lambda i: (i, 0)),
            pl.BlockSpec((tm, D_FOX), lambda i: (i, 0)),
        ],
        out_shape=[
            jax.ShapeDtypeStruct((rows, 3 * D_FOX), BF16),
            jax.ShapeDtypeStruct((rows, D_POOL), F32),
            jax.ShapeDtypeStruct((rows, LANES), F32),
            jax.ShapeDtypeStruct((rows, D_FOX), BF16),
        ],
        scratch_shapes=[pltpu.VMEM((1, LANES), F32)],
        compiler_params=_params(1),
        name="proj",
    )(x2, g, b, w, bf, sel)


def _pool_kernel(u_ref, halo_ref, um_ref, pw_ref, ps_ref, o_ref, ext_ref):
    i = pl.program_id(1)
    tp = u_ref.shape[0]
    halo = jnp.where(i == 0, um_ref[...], halo_ref[...])
    ext_ref[0:N_META, :] = halo
    ext_ref[N_META:, :] = u_ref[...]
    for g, w in enumerate(POOL_WINDOWS):
        cols = slice(g * POOL_GROUP_DIM, (g + 1) * POOL_GROUP_DIM)
        tok = ext_ref[N_META:N_META + tp, cols]
        acc = tok
        for j in range(1, w):
            acc = acc + ext_ref[N_META - j:N_META - j + tp, cols]
        pooled = acc * (1.0 / w) - tok
        mixed = jnp.dot(pooled.astype(BF16), pw_ref[g], preferred_element_type=F32)
        o_ref[:, cols] = (mixed * ps_ref[:, cols]).astype(BF16)


def _pool(u, um, pw, ps, *, batch, seq, tp):
    tiles = seq // tp
    halo_blocks = tp // N_META
    return pl.pallas_call(
        _pool_kernel,
        grid=(batch, tiles),
        in_specs=[
            pl.BlockSpec((tp, D_POOL), lambda b, i: (b * tiles + i, 0)),
            pl.BlockSpec((N_META, D_POOL), lambda b, i: (jnp.maximum((b * tiles + i) * halo_blocks - 1, 0), 0)),
            pl.BlockSpec((N_META, D_POOL), lambda b, i: (0, 0)),
            pl.BlockSpec(pw.shape, lambda b, i: (0, 0, 0)),
            pl.BlockSpec((1, D_POOL), lambda b, i: (0, 0)),
        ],
        out_specs=pl.BlockSpec((tp, D_POOL), lambda b, i: (b * tiles + i, 0)),
        out_shape=jax.ShapeDtypeStruct((batch * seq, D_POOL), BF16),
        scratch_shapes=[pltpu.VMEM((N_META + tp, D_POOL), F32)],
        compiler_params=_params(2),
        name="pool",
    )(u, u, um, pw, ps)


def _attn_kernel(q_ref, k_ref, ka_ref, v_ref, km_ref, vm_ref, dm_ref, o_ref, sa_ref, sb_ref, *, th, tk):
    qi = pl.program_id(2)
    nt = (((1,), (1,)), ((), ()))
    lane = lax.broadcasted_iota(jnp.int32, (th, HEAD_DIM), 1)
    ones = jnp.where(lane < BIAS_TERMS, 1.0, 0.0).astype(BF16)
    km = km_ref[...]
    vm = vm_ref[...]

    def start(half):
        q = q_ref[half * th:(half + 1) * th, :]
        s = lax.dot_general(q, km, nt, preferred_element_type=F32) - dm_ref[0]
        m = jnp.max(s, axis=-1, keepdims=True)
        p = jnp.exp2(s - m)
        l = jnp.sum(p, axis=-1, keepdims=True)
        acc = jnp.dot(p.astype(BF16), vm, preferred_element_type=F32)
        return jnp.concatenate([q, ones], axis=1), (m, l, acc)

    def scores(j, half):
        ks = pl.multiple_of(j * tk, tk)
        ka = jnp.concatenate([k_ref[pl.ds(ks, tk), :], ka_ref[pl.ds(ks, tk), :]], axis=1)
        return lax.dot_general(qa[half], ka, nt, preferred_element_type=F32)

    def update(read_s, j, carry, masked=False):
        m, l, acc = carry
        ks = pl.multiple_of(j * tk, tk)
        if masked:
            row = lax.broadcasted_iota(jnp.int32, (th, tk), 0)
            col = lax.broadcasted_iota(jnp.int32, (th, tk), 1)
            read = lambda: jnp.where(col <= row, read_s(), -jnp.inf)
        else:
            read = read_s
        m_new = jnp.maximum(m, jnp.max(read(), axis=-1, keepdims=True))
        a = jnp.exp2(m - m_new)
        p = jnp.exp2(read() - m_new)
        l = a * l + jnp.sum(p, axis=-1, keepdims=True)
        acc = a * acc + jnp.dot(p.astype(BF16), v_ref[pl.ds(ks, tk), :], preferred_element_type=F32)
        return m_new, l, acc

    qa, carry = zip(start(0), start(1))

    def fill(buf, j):
        for h in range(2):
            buf[h] = scores(j, h)

    def drain(buf, j, carry):
        return tuple(update(lambda h=h: buf[h], j, carry[h]) for h in range(2))

    def body(jj, carry):
        j = 2 * jj
        fill(sb_ref, j + 1)
        carry = drain(sa_ref, j, carry)
        fill(sa_ref, j + 2)
        return drain(sb_ref, j + 1, carry)

    first = 2 * qi
    fill(sa_ref, 0)
    c0, c1 = lax.fori_loop(0, qi, body, tuple(carry))
    s11 = scores(first + 1, 1)
    c0 = update(lambda: sa_ref[0], first, c0, masked=True)
    c1 = update(lambda: sa_ref[1], first, c1)
    c1 = update(lambda: s11, first + 1, c1, masked=True)
    for h, (m, l, acc) in enumerate((c0, c1)):
        o_ref[h * th:(h + 1) * th, :] = (acc / l).astype(BF16)


def _attention(qkv, kaug, qkv_m, d_m, *, batch, seq, th, tk):
    assert th == tk
    tq = 2 * th
    nq = seq // tq
    kern = functools.partial(_attn_kernel, th=th, tk=tk)
    return pl.pallas_call(
        kern,
        grid=(batch, FOX_HEADS, nq),
        in_specs=[
            pl.BlockSpec((tq, HEAD_DIM), lambda b, h, i: (b * nq + i, h)),
            pl.BlockSpec((seq, HEAD_DIM), lambda b, h, i: (b, FOX_HEADS + h)),
            pl.BlockSpec((seq, HEAD_DIM), lambda b, h, i: (b, h)),
            pl.BlockSpec((seq, HEAD_DIM), lambda b, h, i: (b, 2 * FOX_HEADS + h)),
            pl.BlockSpec((META_PAD, HEAD_DIM), lambda b, h, i: (0, FOX_HEADS + h)),
            pl.BlockSpec((META_PAD, HEAD_DIM), lambda b, h, i: (0, 2 * FOX_HEADS + h)),
            pl.BlockSpec((1, 1, META_PAD), lambda b, h, i: (h, 0, 0)),
        ],
        out_specs=pl.BlockSpec((tq, HEAD_DIM), lambda b, h, i: (b * nq + i, h)),
        out_shape=jax.ShapeDtypeStruct((batch * seq, D_FOX), BF16),
        scratch_shapes=[pltpu.VMEM((2, th, tk), F32), pltpu.VMEM((2, th, tk), F32)],
        compiler_params=_params(3),
        name="attn",
    )(qkv, qkv, kaug, qkv, qkv_m, qkv_m, d_m)


def _mix_kernel(x_ref, of_ref, op_ref, wo_ref, gi_ref, bi_ref, gm_ref, bm_ref, wr_ref, br_ref,
                h_ref, ri_ref, rg_ref, cnt_ref, carry_ref, *, chunk):
    i = pl.program_id(0)
    tm = x_ref.shape[0]
    d_model = x_ref.shape[1]
    h0 = _layer_norm(x_ref[...], gi_ref[...], bi_ref[...])
    of = of_ref[...]
    op = op_ref[...]
    for c in range(0, d_model, chunk):
        mix = jnp.dot(of, wo_ref[0:D_FOX, c:c + chunk], preferred_element_type=F32)
        mix = mix + jnp.dot(op, wo_ref[D_FOX:, c:c + chunk], preferred_element_type=F32)
        h_ref[:, c:c + chunk] = DEEPNORM_ALPHA * h0[:, c:c + chunk] + mix
    h1 = _layer_norm(h_ref[...], gm_ref[...], bm_ref[...])
    h_ref[...] = h1

    logits = jnp.dot(h1, wr_ref[...], preferred_element_type=F32, precision=lax.Precision.HIGHEST) + br_ref[...]
    lane = lax.broadcasted_iota(jnp.int32, logits.shape, 1)
    neg = -jnp.inf
    gl = jnp.where(lane < N_GROUPS, logits, neg)
    gmax = jnp.max(gl, axis=-1, keepdims=True)
    g_idx = jnp.min(jnp.where(gl == gmax, lane, LANES), axis=-1, keepdims=True)
    p_g = 1.0 / jnp.sum(jnp.exp(gl - gmax), axis=-1, keepdims=True)
    lo = N_GROUPS + g_idx * EXPERTS_PER_GROUP
    el = jnp.where((lane >= lo) & (lane < lo + EXPERTS_PER_GROUP), logits, neg)
    v1 = jnp.max(el, axis=-1, keepdims=True)
    i1 = jnp.min(jnp.where(el == v1, lane, LANES), axis=-1, keepdims=True)
    el2 = jnp.where(lane == i1, neg, el)
    v2 = jnp.max(el2, axis=-1, keepdims=True)
    i2 = jnp.min(jnp.where(el2 == v2, lane, LANES), axis=-1, keepdims=True)
    t = jnp.exp(v2 - v1)
    gate1 = p_g / (1.0 + t)
    gate2 = gate1 * t
    e1 = i1 - N_GROUPS
    e2 = i2 - N_GROUPS

    @pl.when(i == 0)
    def _():
        carry_ref[...] = jnp.zeros_like(carry_ref)

    r = lax.broadcasted_iota(jnp.int32, (tm, tm), 0)
    c = lax.broadcasted_iota(jnp.int32, (tm, tm), 1)
    lower = (c < r).astype(BF16)
    oh1 = (lane == e1).astype(F32)
    oh2 = (lane == e2).astype(F32)
    pre1 = jnp.dot(lower, oh1.astype(BF16), preferred_element_type=F32)
    pre2 = jnp.dot(lower, oh2.astype(BF16), preferred_element_type=F32)
    cnt1 = jnp.sum(oh1, axis=0, keepdims=True)
    cnt2 = jnp.sum(oh2, axis=0, keepdims=True)
    base = carry_ref[...]
    rank1 = jnp.sum((pre1 + base) * oh1, axis=-1, keepdims=True)
    rank2 = jnp.sum((pre2 + base + cnt1) * oh2, axis=-1, keepdims=True)
    total = base + cnt1 + cnt2
    carry_ref[...] = total
    cnt_ref[...] = total

    ri_ref[...] = jnp.where(lane == 0, e1, jnp.where(lane == 1, e2, jnp.where(
        lane == 2, rank1.astype(jnp.int32), jnp.where(lane == 3, rank2.astype(jnp.int32), 0))))
    rg_ref[...] = jnp.where(lane == 0, gate1, jnp.where(lane == 1, gate2, 0.0))


def _mix(x2, o_fox, o_pool, wo, gi, bi, gm, bm, wr, br, *, tm):
    rows, d_model = x2.shape
    kern = functools.partial(_mix_kernel, chunk=512)
    row_spec = lambda w: pl.BlockSpec((tm, w), lambda i: (i, 0))
    vec_spec = lambda w: pl.BlockSpec((1, w), lambda i: (0, 0))
    return pl.pallas_call(
        kern,
        grid=(rows // tm,),
        in_specs=[
            row_spec(d_model), row_spec(D_FOX), row_spec(D_POOL),
            pl.BlockSpec(wo.shape, lambda i: (0, 0)),
            vec_spec(d_model), vec_spec(d_model), vec_spec(d_model), vec_spec(d_model),
            pl.BlockSpec(wr.shape, lambda i: (0, 0)),
            vec_spec(LANES),
        ],
        out_specs=[row_spec(d_model), row_spec(LANES), row_spec(LANES), vec_spec(LANES)],
        out_shape=[
            jax.ShapeDtypeStruct((rows, d_model), F32),
            jax.ShapeDtypeStruct((rows, LANES), jnp.int32),
            jax.ShapeDtypeStruct((rows, LANES), F32),
            jax.ShapeDtypeStruct((1, LANES), F32),
        ],
        scratch_shapes=[pltpu.VMEM((1, LANES), F32)],
        compiler_params=_params(1),
        name="mix",
    )(x2, o_fox, o_pool, wo, gi, bi, gm, bm, wr, br)


def _moe_kernel(be_ref, na_ref, tok_ref, tokn_ref, dst_ref, h_hbm, w13_ref, w2_ref, y_hbm, xbuf, obuf, gsem, ssem,
                *, chunk, unroll):
    i = pl.program_id(0)
    n_act = na_ref[0]
    te = xbuf.shape[1]
    slot = i % 2

    def start_gather(idx_ref, s):
        for r in range(te):
            pltpu.make_async_copy(h_hbm.at[pl.ds(idx_ref[0, 0, r], 1)], xbuf.at[s, pl.ds(r, 1)], gsem.at[s]).start()

    def wait_gather(s):
        pltpu.make_async_copy(h_hbm.at[pl.ds(0, te)], xbuf.at[s], gsem.at[s]).wait()

    def start_scatter(s):
        for r in range(te):
            pltpu.make_async_copy(obuf.at[s, pl.ds(r, 1)], y_hbm.at[pl.ds(dst_ref[0, 0, r], 1)], ssem.at[s]).start()

    def wait_scatter(s):
        pltpu.make_async_copy(obuf.at[s], y_hbm.at[pl.ds(0, te)], ssem.at[s]).wait()

    @pl.when(i < n_act)
    def _():
        @pl.when(i == 0)
        def _():
            start_gather(tok_ref, slot)
            spare = y_hbm.shape[0] - 2 * te
            obuf[...] = jnp.zeros_like(obuf)
            for s in range(2):
                pltpu.make_async_copy(obuf.at[s], y_hbm.at[pl.ds(spare + s * te, te)], ssem.at[s]).start()
            for s in range(2):
                pltpu.make_async_copy(obuf.at[s], y_hbm.at[pl.ds(spare + s * te, te)], ssem.at[s]).wait()

        wait_gather(slot)

        @pl.when(i + 1 < n_act)
        def _():
            start_gather(tokn_ref, 1 - slot)

        @pl.when(i >= 2)
        def _():
            wait_scatter(slot)

        x = xbuf[slot].astype(BF16)
        out = None
        for c in range(0, D_EXPERT, chunk):
            gate = jnp.dot(x, w13_ref[0, :, c:c + chunk], preferred_element_type=F32)
            up = jnp.dot(x, w13_ref[0, :, D_EXPERT + c:D_EXPERT + c + chunk], preferred_element_type=F32)
            hid = (gate / (1.0 + jnp.exp(-gate)) * up).astype(BF16)
            part = jnp.dot(hid, w2_ref[0, c:c + chunk, :], preferred_element_type=F32)
            out = part if out is None else out + part
        obuf[slot] = out
        start_scatter(slot)

        @pl.when(i == n_act - 1)
        def _():
            @pl.when(i >= 1)
            def _():
                wait_scatter(1 - slot)
            wait_scatter(slot)


def _moe(blk_e, n_act, rows_tok, rows_dst, h1, w13, w2, *, te):
    rows, d_model = h1.shape
    nb = rows_tok.shape[0]
    kern = functools.partial(_moe_kernel, chunk=256, unroll=8)
    smem_spec = lambda off: pl.BlockSpec((1, 1, te), lambda i, be, na: (jnp.minimum(i + off, nb - 1), 0, 0),
                                         memory_space=pltpu.SMEM)
    return pl.pallas_call(
        kern,
        grid_spec=pltpu.PrefetchScalarGridSpec(
            num_scalar_prefetch=2,
            grid=(nb,),
            in_specs=[
                smem_spec(0), smem_spec(1), smem_spec(0),
                pl.BlockSpec(memory_space=pl.ANY),
                pl.BlockSpec((1, d_model, 2 * D_EXPERT), lambda i, be, na: (be[i], 0, 0)),
                pl.BlockSpec((1, D_EXPERT, d_model), lambda i, be, na: (be[i], 0, 0)),
            ],
            out_specs=pl.BlockSpec(memory_space=pl.ANY),
            scratch_shapes=[
                pltpu.VMEM((2, te, d_model), F32),
                pltpu.VMEM((2, te, d_model), F32),
                pltpu.SemaphoreType.DMA((2,)),
                pltpu.SemaphoreType.DMA((2,)),
            ],
        ),
        out_shape=jax.ShapeDtypeStruct((2 * rows + 2 * te, d_model), F32),
        compiler_params=_params(1),
        name="moe",
    )(blk_e, n_act, rows_tok, rows_tok, rows_dst, h1, w13, w2)


def _final_kernel(h_ref, y0_ref, y1_ref, rg_ref, g_ref, b_ref, o_ref):
    rg = rg_ref[...]
    ffn = y0_ref[...] * rg[:, 0:1] + y1_ref[...] * rg[:, 1:2]
    o_ref[...] = _layer_norm(DEEPNORM_ALPHA * h_ref[...] + ffn, g_ref[...], b_ref[...])


def _final(h1, y, rg, g, b, *, tm):
    rows, d_model = h1.shape
    nt = rows // tm
    return pl.pallas_call(
        _final_kernel,
        grid=(nt,),
        in_specs=[
            pl.BlockSpec((tm, d_model), lambda i: (i, 0)),
            pl.BlockSpec((tm, d_model), lambda i: (i, 0)),
            pl.BlockSpec((tm, d_model), lambda i: (nt + i, 0)),
            pl.BlockSpec((tm, LANES), lambda i: (i, 0)),
            pl.BlockSpec((1, d_model), lambda i: (0, 0)),
            pl.BlockSpec((1, d_model), lambda i: (0, 0)),
        ],
        out_specs=pl.BlockSpec((tm, d_model), lambda i: (i, 0)),
        out_shape=jax.ShapeDtypeStruct((rows, d_model), F32),
        compiler_params=_params(1),
        name="final",
    )(h1, y, y, rg, g, b)


def _dispatch_tables(ri, cnt, *, rows, te):
    experts = ri[:, 0:2]
    rank = ri[:, 2:4]
    counts = cnt[0, :N_EXPERTS].astype(jnp.int32)
    padded = ((counts + te - 1) // te) * te
    pend = jnp.cumsum(padded)
    pstart = pend - padded
    dest = (pstart[experts] + rank).reshape(-1)
    nb = (2 * rows + N_EXPERTS * (te - 1) + te - 1) // te
    tok = jnp.repeat(jnp.arange(rows, dtype=jnp.int32), 2)
    out_row = tok + jnp.tile(jnp.array([0, rows], jnp.int32), rows)
    pos = jnp.arange(nb * te, dtype=jnp.int32)
    rows_dst = (2 * rows + pos % (2 * te)).at[dest].set(out_row)
    rows_tok = jnp.where(rows_dst < rows, rows_dst, jnp.where(rows_dst < 2 * rows, rows_dst - rows, 0))
    blk_e = jnp.minimum(jnp.sum(pend[None, :] <= (jnp.arange(nb, dtype=jnp.int32) * te)[:, None], axis=1),
                        N_EXPERTS - 1).astype(jnp.int32)
    n_act = (pend[-1] // te).astype(jnp.int32).reshape(1)
    return blk_e, n_act, rows_tok.reshape(nb, 1, te), rows_dst.reshape(nb, 1, te)


def kernel(x, meta, ln_in_g, ln_in_b, w_in, b_f, pool_w, pool_scale, w_out, ln_mix_g, ln_mix_b,
           w_router_g, b_router_g, w_router_e, b_router_e, w13, w2, ln_ffn_g, ln_ffn_b):
    batch, seq, d_model = x.shape
    rows = batch * seq
    tile = min(512, seq)
    assert seq % (2 * tile) == 0 and tile % N_META == 0
    assert w_in.shape[0] == 1, "depth-1 trunk"

    x2 = x.reshape(rows, d_model)
    row = lambda v: v.reshape(1, -1).astype(F32)

    wi = w_in[0]
    w_f = jnp.pad(wi[:, 3 * D_FOX:3 * D_FOX + FOX_HEADS], ((0, 0), (0, LANES - FOX_HEADS)))
    w_proj = jnp.concatenate([wi[:, :3 * D_FOX], wi[:, 3 * D_FOX + FOX_HEADS:], w_f], axis=1).astype(BF16)
    bf = jnp.pad(b_f[0], (0, LANES - FOX_HEADS)).reshape(1, LANES)
    gi, bi = row(ln_in_g), row(ln_in_b)

    meta_pad = jnp.pad(meta.astype(F32), ((0, META_PAD - N_META), (0, 0)))
    sel = _bias_placement()
    qkv_m, u_m, c_m, _ = _proj(meta_pad, gi, bi, w_proj, bf, sel, tm=META_PAD, tiles_per_seq=1)
    d_meta = (c_m[:N_META, :FOX_HEADS] - c_m[N_META - 1:N_META, :FOX_HEADS]) * LOG2E
    d_m = jnp.pad(d_meta.T, ((0, 0), (0, META_PAD - N_META)), constant_values=MASKED_BIAS)
    d_m = d_m.reshape(FOX_HEADS, 1, META_PAD)

    qkv, u, _, kaug = _proj(x2, gi, bi, w_proj, bf, sel, tm=tile, tiles_per_seq=seq // tile)

    o_pool = _pool(u, u_m[:N_META], pool_w[0].astype(BF16), row(pool_scale[0]), batch=batch, seq=seq, tp=tile)
    o_fox = _attention(qkv, kaug, qkv_m, d_m, batch=batch, seq=seq, th=tile, tk=tile)

    w_r = jnp.pad(jnp.concatenate([w_router_g[0], w_router_e[0]], axis=1),
                  ((0, 0), (0, LANES - N_GROUPS - N_EXPERTS)))
    b_r = jnp.pad(jnp.concatenate([b_router_g[0], b_router_e[0]]), (0, LANES - N_GROUPS - N_EXPERTS)).reshape(1, LANES)
    h1, ri, rg, cnt = _mix(x2, o_fox, o_pool, w_out[0].astype(BF16), gi, bi, row(ln_mix_g[0]), row(ln_mix_b[0]),
                           w_r, b_r, tm=min(256, seq))

    te = 256
    blk_e, n_act, rows_tok, rows_dst = _dispatch_tables(ri, cnt, rows=rows, te=te)
    y = _moe(blk_e, n_act, rows_tok, rows_dst, h1, w13[0].astype(BF16), w2[0].astype(BF16), te=te)

    out = _final(h1, y, rg, row(ln_ffn_g[0]), row(ln_ffn_b[0]), tm=tile)
    return out.reshape(batch, seq, d_model)
```

```python
import functools

import jax
import jax.numpy as jnp
from jax import lax
from jax.experimental import pallas as pl
from jax.experimental.pallas import tpu as pltpu

F32 = jnp.float32
BF16 = jnp.bfloat16

N_META = 16
FOX_HEADS = 8
HEAD_DIM = 128
D_FOX = FOX_HEADS * HEAD_DIM
POOL_WINDOWS = (2, 4, 8, 16)
POOL_GROUP_DIM = 256
D_POOL = len(POOL_WINDOWS) * POOL_GROUP_DIM
N_GROUPS = 4
EXPERTS_PER_GROUP = 8
N_EXPERTS = N_GROUPS * EXPERTS_PER_GROUP
D_EXPERT = 1024
LN_EPS = 1e-5
DEEPNORM_ALPHA = 2.0 ** 0.25

LANES = 128
META_PAD = 128
MASKED_BIAS = 1e30
LOG2E = 1.4426950408889634
BIAS_TERMS = 3
VMEM_LIMIT = 56 * 1024 * 1024


def _params(n_axes, vmem=VMEM_LIMIT):
    return pltpu.CompilerParams(dimension_semantics=("arbitrary",) * n_axes, vmem_limit_bytes=vmem)


def _layer_norm(x, g, b):
    mu = jnp.mean(x, axis=-1, keepdims=True)
    xc = x - mu
    var = jnp.mean(xc * xc, axis=-1, keepdims=True)
    return xc * lax.rsqrt(var + LN_EPS) * g + b


def _pack_halves(x):
    n = x.shape[1] // 2
    bits = lambda v: lax.bitcast_convert_type(v.astype(BF16).astype(F32), jnp.uint32)
    return (bits(x[:, :n]) >> 16) | (bits(x[:, n:]) & jnp.uint32(0xFFFF0000))


def _unpack_halves(w):
    lo = lax.bitcast_convert_type(w << 16, F32)
    hi = lax.bitcast_convert_type(w & jnp.uint32(0xFFFF0000), F32)
    return jnp.concatenate([lo, hi], axis=1)


def _proj_kernel(x_ref, g_ref, b_ref, w_ref, bf_ref, sel_ref, qkv_ref, u_ref, d_ref, ka_ref, carry_ref,
                 *, tiles_per_seq, chunk):
    i = pl.program_id(0)
    tm = x_ref.shape[0]
    xn = _layer_norm(x_ref[...], g_ref[...], b_ref[...]).astype(BF16)
    scale = HEAD_DIM ** -0.5 * LOG2E
    for c in range(0, 3 * D_FOX, chunk):
        acc = jnp.dot(xn, w_ref[:, c:c + chunk], preferred_element_type=F32)
        if c < D_FOX:
            acc = acc * scale
        qkv_ref[:, c:c + chunk] = acc.astype(BF16)
    for c in range(0, D_POOL, chunk):
        u_ref[:, c:c + chunk] = jnp.dot(xn, w_ref[:, 3 * D_FOX + c:3 * D_FOX + c + chunk],
                                        preferred_element_type=F32)
    fl = jnp.dot(xn, w_ref[:, 3 * D_FOX + D_POOL:], preferred_element_type=F32) + bf_ref[...]
    lf = jnp.minimum(fl, 0.0) - jnp.log1p(jnp.exp(-jnp.abs(fl)))

    @pl.when(i % tiles_per_seq == 0)
    def _():
        carry_ref[...] = jnp.zeros_like(carry_ref)

    row = lax.broadcasted_iota(jnp.int32, lf.shape, 0)
    acc = lf
    k = 1
    while k < tm:
        acc = acc + jnp.where(row >= k, pltpu.roll(acc, k, 0), 0.0)
        k *= 2
    d = acc + carry_ref[...]
    d_ref[...] = d
    carry_ref[...] = d[tm - 1:tm, :]

    nd = d * (-LOG2E)
    hi = nd.astype(BF16)
    r1 = nd - hi.astype(F32)
    mid = r1.astype(BF16)
    lo = (r1 - mid.astype(F32)).astype(BF16)
    split = jnp.concatenate([hi, mid, lo], axis=1)
    ka_ref[...] = jnp.dot(split, sel_ref[...], preferred_element_type=F32).astype(BF16)


def _bias_placement():
    t, h = jnp.meshgrid(jnp.arange(BIAS_TERMS), jnp.arange(FOX_HEADS), indexing="ij")
    sel = jnp.zeros((BIAS_TERMS * LANES, D_FOX), F32)
    return sel.at[(t * LANES + h).ravel(), (h * HEAD_DIM + t).ravel()].set(1.0).astype(BF16)


def _proj(x2, g, b, w, bf, sel, *, tm, tiles_per_seq):
    rows, d_model = x2.shape
    n_proj = w.shape[1]
    kern = functools.partial(_proj_kernel, tiles_per_seq=tiles_per_seq, chunk=512)
    return pl.pallas_call(
        kern,
        grid=(rows // tm,),
        in_specs=[
            pl.BlockSpec((tm, d_model), lambda i: (i, 0)),
            pl.BlockSpec((1, d_model), lambda i: (0, 0)),
            pl.BlockSpec((1, d_model), lambda i: (0, 0)),
            pl.BlockSpec((d_model, n_proj), lambda i: (0, 0)),
            pl.BlockSpec((1, LANES), lambda i: (0, 0)),
            pl.BlockSpec(sel.shape, lambda i: (0, 0)),
        ],
        out_specs=[
            pl.BlockSpec((tm, 3 * D_FOX), lambda i: (i, 0)),
            pl.BlockSpec((tm, D_POOL), lambda i: (i, 0)),
            pl.BlockSpec((tm, LANES), lambda i: (i, 0)),
            pl.BlockSpec((tm, D_FOX), lambda i: (i, 0)),
        ],
        out_shape=[
            jax.ShapeDtypeStruct((rows, 3 * D_FOX), BF16),
            jax.ShapeDtypeStruct((rows, D_POOL), F32),
            jax.ShapeDtypeStruct((rows, LANES), F32),
            jax.ShapeDtypeStruct((rows, D_FOX), BF16),
        ],
        scratch_shapes=[pltpu.VMEM((1, LANES), F32)],
        compiler_params=_params(1),
        name="proj",
    )(x2, g, b, w, bf, sel)


def _pool_kernel(u_ref, halo_ref, um_ref, pw_ref, ps_ref, o_ref, ext_ref):
    i = pl.program_id(1)
    tp = u_ref.shape[0]
    halo = jnp.where(i == 0, um_ref[...], halo_ref[...])
    ext_ref[0:N_META, :] = halo
    ext_ref[N_META:, :] = u_ref[...]
    for g, w in enumerate(POOL_WINDOWS):
        cols = slice(g * POOL_GROUP_DIM, (g + 1) * POOL_GROUP_DIM)
        tok = ext_ref[N_META:N_META + tp, cols]
        acc = tok
        for j in range(1, w):
            acc = acc + ext_ref[N_META - j:N_META - j + tp, cols]
        pooled = acc * (1.0 / w) - tok
        mixed = jnp.dot(pooled.astype(BF16), pw_ref[g], preferred_element_type=F32)
        o_ref[:, cols] = (mixed * ps_ref[:, cols]).astype(BF16)


def _pool(u, um, pw, ps, *, batch, seq, tp):
    tiles = seq // tp
    halo_blocks = tp // N_META
    return pl.pallas_call(
        _pool_kernel,
        grid=(batch, tiles),
        in_specs=[
            pl.BlockSpec((tp, D_POOL), lambda b, i: (b * tiles + i, 0)),
            pl.BlockSpec((N_META, D_POOL), lambda b, i: (jnp.maximum((b * tiles + i) * halo_blocks - 1, 0), 0)),
            pl.BlockSpec((N_META, D_POOL), lambda b, i: (0, 0)),
            pl.BlockSpec(pw.shape, lambda b, i: (0, 0, 0)),
            pl.BlockSpec((1, D_POOL), lambda b, i: (0, 0)),
        ],
        out_specs=pl.BlockSpec((tp, D_POOL), lambda b, i: (b * tiles + i, 0)),
        out_shape=jax.ShapeDtypeStruct((batch * seq, D_POOL), BF16),
        scratch_shapes=[pltpu.VMEM((N_META + tp, D_POOL), F32)],
        compiler_params=_params(2),
        name="pool",
    )(u, u, um, pw, ps)


def _attn_kernel(q_ref, k_ref, ka_ref, v_ref, km_ref, vm_ref, dm_ref, o_ref, sa_ref, sb_ref, *, th, tk):
    qi = pl.program_id(2)
    nt = (((1,), (1,)), ((), ()))
    lane = lax.broadcasted_iota(jnp.int32, (th, HEAD_DIM), 1)
    ones = jnp.where(lane < BIAS_TERMS, 1.0, 0.0).astype(BF16)
    km = km_ref[...]
    vm = vm_ref[...]

    def start(half):
        q = q_ref[half * th:(half + 1) * th, :]
        s = lax.dot_general(q, km, nt, preferred_element_type=F32) - dm_ref[0]
        m = jnp.max(s, axis=-1, keepdims=True)
        p = jnp.exp2(s - m)
        l = jnp.sum(p, axis=-1, keepdims=True)
        acc = jnp.dot(p.astype(BF16), vm, preferred_element_type=F32)
        return jnp.concatenate([q, ones], axis=1), (m, l, acc)

    def scores(j, half):
        ks = pl.multiple_of(j * tk, tk)
        ka = jnp.concatenate([k_ref[pl.ds(ks, tk), :], ka_ref[pl.ds(ks, tk), :]], axis=1)
        return lax.dot_general(qa[half], ka, nt, preferred_element_type=F32)

    def update(read_s, j, carry, masked=False):
        m, l, acc = carry
        ks = pl.multiple_of(j * tk, tk)
        if masked:
            row = lax.broadcasted_iota(jnp.int32, (th, tk), 0)
            col = lax.broadcasted_iota(jnp.int32, (th, tk), 1)
            read = lambda: jnp.where(col <= row, read_s(), -jnp.inf)
        else:
            read = read_s
        m_new = jnp.maximum(m, jnp.max(read(), axis=-1, keepdims=True))
        a = jnp.exp2(m - m_new)
        p = jnp.exp2(read() - m_new)
        l = a * l + jnp.sum(p, axis=-1, keepdims=True)
        acc = a * acc + jnp.dot(p.astype(BF16), v_ref[pl.ds(ks, tk), :], preferred_element_type=F32)
        return m_new, l, acc

    qa, carry = zip(start(0), start(1))

    def fill(buf, j):
        for h in range(2):
            buf[h] = scores(j, h)

    def drain(buf, j, carry):
        return tuple(update(lambda h=h: buf[h], j, carry[h]) for h in range(2))

    def body(jj, carry):
        j = 2 * jj
        fill(sb_ref, j + 1)
        carry = drain(sa_ref, j, carry)
        fill(sa_ref, j + 2)
        return drain(sb_ref, j + 1, carry)

    first = 2 * qi
    fill(sa_ref, 0)
    c0, c1 = lax.fori_loop(0, qi, body, tuple(carry))
    s11 = scores(first + 1, 1)
    c0 = update(lambda: sa_ref[0], first, c0, masked=True)
    c1 = update(lambda: sa_ref[1], first, c1)
    c1 = update(lambda: s11, first + 1, c1, masked=True)
    for h, (m, l, acc) in enumerate((c0, c1)):
        o_ref[h * th:(h + 1) * th, :] = (acc / l).astype(BF16)


def _attention(qkv, kaug, qkv_m, d_m, *, batch, seq, th, tk):
    assert th == tk
    tq = 2 * th
    nq = seq // tq
    kern = functools.partial(_attn_kernel, th=th, tk=tk)
    return pl.pallas_call(
        kern,
        grid=(batch, FOX_HEADS, nq),
        in_specs=[
            pl.BlockSpec((tq, HEAD_DIM), lambda b, h, i: (b * nq + i, h)),
            pl.BlockSpec((seq, HEAD_DIM), lambda b, h, i: (b, FOX_HEADS + h)),
            pl.BlockSpec((seq, HEAD_DIM), lambda b, h, i: (b, h)),
            pl.BlockSpec((seq, HEAD_DIM), lambda b, h, i: (b, 2 * FOX_HEADS + h)),
            pl.BlockSpec((META_PAD, HEAD_DIM), lambda b, h, i: (0, FOX_HEADS + h)),
            pl.BlockSpec((META_PAD, HEAD_DIM), lambda b, h, i: (0, 2 * FOX_HEADS + h)),
            pl.BlockSpec((1, 1, META_PAD), lambda b, h, i: (h, 0, 0)),
        ],
        out_specs=pl.BlockSpec((tq, HEAD_DIM), lambda b, h, i: (b * nq + i, h)),
        out_shape=jax.ShapeDtypeStruct((batch * seq, D_FOX), BF16),
        scratch_shapes=[pltpu.VMEM((2, th, tk), F32), pltpu.VMEM((2, th, tk), F32)],
        compiler_params=_params(3),
        name="attn",
    )(qkv, qkv, kaug, qkv, qkv_m, qkv_m, d_m)


def _mix_kernel(x_ref, of_ref, op_ref, wo_ref, gi_ref, bi_ref, gm_ref, bm_ref, wr_ref, br_ref, tri_ref,
                h_ref, hp_ref, ri_ref, rg_ref, cnt_ref, carry_ref, *, chunk):
    i = pl.program_id(0)
    tm = x_ref.shape[0]
    d_model = x_ref.shape[1]
    h0 = _layer_norm(x_ref[...], gi_ref[...], bi_ref[...])
    of = of_ref[...]
    op = op_ref[...]
    for c in range(0, d_model, chunk):
        mix = jnp.dot(of, wo_ref[0:D_FOX, c:c + chunk], preferred_element_type=F32)
        mix = mix + jnp.dot(op, wo_ref[D_FOX:, c:c + chunk], preferred_element_type=F32)
        h_ref[:, c:c + chunk] = DEEPNORM_ALPHA * h0[:, c:c + chunk] + mix
    h1 = _layer_norm(h_ref[...], gm_ref[...], bm_ref[...])
    h_ref[...] = h1
    hp_ref[...] = _pack_halves(h1)

    h_hi = h1.astype(BF16)
    h_mid = (h1 - h_hi.astype(F32)).astype(BF16)
    hh = jnp.dot(h_hi, wr_ref[...], preferred_element_type=F32)
    mh = jnp.dot(h_mid, wr_ref[:, :LANES], preferred_element_type=F32)
    logits = hh[:, :LANES] + hh[:, LANES:] + mh + br_ref[...]
    lane = lax.broadcasted_iota(jnp.int32, logits.shape, 1)
    neg = -jnp.inf
    gl = jnp.where(lane < N_GROUPS, logits, neg)
    gmax = jnp.max(gl, axis=-1, keepdims=True)
    g_idx = jnp.min(jnp.where(gl == gmax, lane, LANES), axis=-1, keepdims=True)
    p_g = 1.0 / jnp.sum(jnp.exp(gl - gmax), axis=-1, keepdims=True)
    lo = N_GROUPS + g_idx * EXPERTS_PER_GROUP
    el = jnp.where((lane >= lo) & (lane < lo + EXPERTS_PER_GROUP), logits, neg)
    v1 = jnp.max(el, axis=-1, keepdims=True)
    i1 = jnp.min(jnp.where(el == v1, lane, LANES), axis=-1, keepdims=True)
    el2 = jnp.where(lane == i1, neg, el)
    v2 = jnp.max(el2, axis=-1, keepdims=True)
    i2 = jnp.min(jnp.where(el2 == v2, lane, LANES), axis=-1, keepdims=True)
    t = jnp.exp(v2 - v1)
    gate1 = p_g / (1.0 + t)
    gate2 = gate1 * t
    e1 = i1 - N_GROUPS
    e2 = i2 - N_GROUPS

    @pl.when(i == 0)
    def _():
        carry_ref[...] = jnp.zeros_like(carry_ref)

    lower = tri_ref[...]
    oh1 = (lane == e1).astype(F32)
    oh2 = (lane == e2).astype(F32)
    pre1 = jnp.dot(lower, oh1.astype(BF16), preferred_element_type=F32)
    pre2 = jnp.dot(lower, oh2.astype(BF16), preferred_element_type=F32)
    cnt1 = jnp.sum(oh1, axis=0, keepdims=True)
    cnt2 = jnp.sum(oh2, axis=0, keepdims=True)
    base = carry_ref[...]
    rank1 = jnp.sum((pre1 + base) * oh1, axis=-1, keepdims=True)
    rank2 = jnp.sum((pre2 + base + cnt1) * oh2, axis=-1, keepdims=True)
    total = base + cnt1 + cnt2
    carry_ref[...] = total
    cnt_ref[...] = total

    ri_ref[...] = jnp.where(lane == 0, e1, jnp.where(lane == 1, e2, jnp.where(
        lane == 2, rank1.astype(jnp.int32), jnp.where(lane == 3, rank2.astype(jnp.int32), 0))))
    rg_ref[...] = jnp.where(lane == 0, gate1, jnp.where(lane == 1, gate2, 0.0))


def _mix(x2, o_fox, o_pool, wo, gi, bi, gm, bm, wr, br, *, tm):
    rows, d_model = x2.shape
    idx = jnp.arange(tm, dtype=jnp.int32)
    tri = (idx[None, :] < idx[:, None]).astype(BF16)
    kern = functools.partial(_mix_kernel, chunk=512)
    row_spec = lambda w: pl.BlockSpec((tm, w), lambda i: (i, 0))
    vec_spec = lambda w: pl.BlockSpec((1, w), lambda i: (0, 0))
    return pl.pallas_call(
        kern,
        grid=(rows // tm,),
        in_specs=[
            row_spec(d_model), row_spec(D_FOX), row_spec(D_POOL),
            pl.BlockSpec(wo.shape, lambda i: (0, 0)),
            vec_spec(d_model), vec_spec(d_model), vec_spec(d_model), vec_spec(d_model),
            pl.BlockSpec(wr.shape, lambda i: (0, 0)),
            vec_spec(LANES),
            pl.BlockSpec((tm, tm), lambda i: (0, 0)),
        ],
        out_specs=[row_spec(d_model), row_spec(d_model // 2), row_spec(LANES), row_spec(LANES), vec_spec(LANES)],
        out_shape=[
            jax.ShapeDtypeStruct((rows, d_model), F32),
            jax.ShapeDtypeStruct((rows, d_model // 2), jnp.uint32),
            jax.ShapeDtypeStruct((rows, LANES), jnp.int32),
            jax.ShapeDtypeStruct((rows, LANES), F32),
            jax.ShapeDtypeStruct((1, LANES), F32),
        ],
        scratch_shapes=[pltpu.VMEM((1, LANES), F32)],
        compiler_params=_params(1),
        name="mix",
    )(x2, o_fox, o_pool, wo, gi, bi, gm, bm, wr, br, tri)


def _moe_kernel(be_ref, na_ref, tok_ref, tokn_ref, dst_ref, h_hbm, w13_ref, w2_ref, y_hbm, xbuf, obuf, gsem, ssem,
                *, chunk, unroll):
    i = pl.program_id(0)
    n_act = na_ref[0]
    te = xbuf.shape[1]
    slot = i % 2

    def start_gather(idx_ref, s):
        for r in range(te):
            pltpu.make_async_copy(h_hbm.at[pl.ds(idx_ref[0, 0, r], 1)], xbuf.at[s, pl.ds(r, 1)], gsem.at[s]).start()

    def wait_gather(s):
        pltpu.make_async_copy(h_hbm.at[pl.ds(0, te)], xbuf.at[s], gsem.at[s]).wait()

    def start_scatter(s):
        for r in range(te):
            pltpu.make_async_copy(obuf.at[s, pl.ds(r, 1)], y_hbm.at[pl.ds(dst_ref[0, 0, r], 1)], ssem.at[s]).start()

    def wait_scatter(s):
        pltpu.make_async_copy(obuf.at[s], y_hbm.at[pl.ds(0, te)], ssem.at[s]).wait()

    @pl.when(i < n_act)
    def _():
        @pl.when(i == 0)
        def _():
            start_gather(tok_ref, slot)
            spare = y_hbm.shape[0] - 2 * te
            obuf[...] = jnp.zeros_like(obuf)
            for s in range(2):
                pltpu.make_async_copy(obuf.at[s], y_hbm.at[pl.ds(spare + s * te, te)], ssem.at[s]).start()
            for s in range(2):
                pltpu.make_async_copy(obuf.at[s], y_hbm.at[pl.ds(spare + s * te, te)], ssem.at[s]).wait()

        wait_gather(slot)

        @pl.when(i + 1 < n_act)
        def _():
            start_gather(tokn_ref, 1 - slot)

        @pl.when(i >= 2)
        def _():
            wait_scatter(slot)

        x = _unpack_halves(xbuf[slot]).astype(BF16)
        out = None
        for c in range(0, D_EXPERT, chunk):
            gate = jnp.dot(x, w13_ref[0, :, c:c + chunk], preferred_element_type=F32)
            up = jnp.dot(x, w13_ref[0, :, D_EXPERT + c:D_EXPERT + c + chunk], preferred_element_type=F32)
            hid = (gate / (1.0 + jnp.exp(-gate)) * up).astype(BF16)
            part = jnp.dot(hid, w2_ref[0, c:c + chunk, :], preferred_element_type=F32)
            out = part if out is None else out + part
        obuf[slot] = _pack_halves(out)
        start_scatter(slot)

        @pl.when(i == n_act - 1)
        def _():
            @pl.when(i >= 1)
            def _():
                wait_scatter(1 - slot)
            wait_scatter(slot)


def _moe(blk_e, n_act, rows_tok, rows_dst, h1p, w13, w2, *, te):
    rows, d_model = h1p.shape[0], 2 * h1p.shape[1]
    nb = rows_tok.shape[0]
    kern = functools.partial(_moe_kernel, chunk=256, unroll=8)
    smem_spec = lambda off: pl.BlockSpec((1, 1, te), lambda i, be, na: (jnp.minimum(i + off, nb - 1), 0, 0),
                                         memory_space=pltpu.SMEM)
    return pl.pallas_call(
        kern,
        grid_spec=pltpu.PrefetchScalarGridSpec(
            num_scalar_prefetch=2,
            grid=(nb,),
            in_specs=[
                smem_spec(0), smem_spec(1), smem_spec(0),
                pl.BlockSpec(memory_space=pl.ANY),
                pl.BlockSpec((1, d_model, 2 * D_EXPERT), lambda i, be, na: (be[i], 0, 0)),
                pl.BlockSpec((1, D_EXPERT, d_model), lambda i, be, na: (be[i], 0, 0)),
            ],
            out_specs=pl.BlockSpec(memory_space=pl.ANY),
            scratch_shapes=[
                pltpu.VMEM((2, te, d_model // 2), jnp.uint32),
                pltpu.VMEM((2, te, d_model // 2), jnp.uint32),
                pltpu.SemaphoreType.DMA((2,)),
                pltpu.SemaphoreType.DMA((2,)),
            ],
        ),
        out_shape=jax.ShapeDtypeStruct((2 * rows + 2 * te, d_model // 2), jnp.uint32),
        compiler_params=_params(1),
        name="moe",
    )(blk_e, n_act, rows_tok, rows_tok, rows_dst, h1p, w13, w2)


def _final_kernel(h_ref, y0_ref, y1_ref, rg_ref, g_ref, b_ref, o_ref):
    rg = rg_ref[...]
    ffn = _unpack_halves(y0_ref[...]) * rg[:, 0:1] + _unpack_halves(y1_ref[...]) * rg[:, 1:2]
    o_ref[...] = _layer_norm(DEEPNORM_ALPHA * h_ref[...] + ffn, g_ref[...], b_ref[...])


def _final(h1, y, rg, g, b, *, tm):
    rows, d_model = h1.shape
    nt = rows // tm
    return pl.pallas_call(
        _final_kernel,
        grid=(nt,),
        in_specs=[
            pl.BlockSpec((tm, d_model), lambda i: (i, 0)),
            pl.BlockSpec((tm, d_model // 2), lambda i: (i, 0)),
            pl.BlockSpec((tm, d_model // 2), lambda i: (nt + i, 0)),
            pl.BlockSpec((tm, LANES), lambda i: (i, 0)),
            pl.BlockSpec((1, d_model), lambda i: (0, 0)),
            pl.BlockSpec((1, d_model), lambda i: (0, 0)),
        ],
        out_specs=pl.BlockSpec((tm, d_model), lambda i: (i, 0)),
        out_shape=jax.ShapeDtypeStruct((rows, d_model), F32),
        compiler_params=_params(1),
        name="final",
    )(h1, y, y, rg, g, b)


def _dispatch_tables(ri, cnt, *, rows, te):
    experts = ri[:, 0:2]
    rank = ri[:, 2:4]
    counts = cnt[0, :N_EXPERTS].astype(jnp.int32)
    padded = ((counts + te - 1) // te) * te
    pend = jnp.cumsum(padded)
    pstart = pend - padded
    dest = (pstart[experts] + rank).reshape(-1)
    nb = (2 * rows + N_EXPERTS * (te - 1) + te - 1) // te
    tok = jnp.repeat(jnp.arange(rows, dtype=jnp.int32), 2)
    out_row = tok + jnp.tile(jnp.array([0, rows], jnp.int32), rows)
    pos = jnp.arange(nb * te, dtype=jnp.int32)
    rows_dst = (2 * rows + pos % (2 * te)).at[dest].set(out_row)
    rows_tok = jnp.where(rows_dst < rows, rows_dst, jnp.where(rows_dst < 2 * rows, rows_dst - rows, 0))
    blk_e = jnp.minimum(jnp.sum(pend[None, :] <= (jnp.arange(nb, dtype=jnp.int32) * te)[:, None], axis=1),
                        N_EXPERTS - 1).astype(jnp.int32)
    n_act = (pend[-1] // te).astype(jnp.int32).reshape(1)
    return blk_e, n_act, rows_tok.reshape(nb, 1, te), rows_dst.reshape(nb, 1, te)


def kernel(x, meta, ln_in_g, ln_in_b, w_in, b_f, pool_w, pool_scale, w_out, ln_mix_g, ln_mix_b,
           w_router_g, b_router_g, w_router_e, b_router_e, w13, w2, ln_ffn_g, ln_ffn_b):
    batch, seq, d_model = x.shape
    rows = batch * seq
    tile = min(512, seq)
    assert seq % (2 * tile) == 0 and tile % N_META == 0
    assert w_in.shape[0] == 1, "depth-1 trunk"

    x2 = x.reshape(rows, d_model)
    row = lambda v: v.reshape(1, -1).astype(F32)

    wi = w_in[0]
    w_f = jnp.pad(wi[:, 3 * D_FOX:3 * D_FOX + FOX_HEADS], ((0, 0), (0, LANES - FOX_HEADS)))
    w_proj = jnp.concatenate([wi[:, :3 * D_FOX], wi[:, 3 * D_FOX + FOX_HEADS:], w_f], axis=1).astype(BF16)
    bf = jnp.pad(b_f[0], (0, LANES - FOX_HEADS)).reshape(1, LANES)
    gi, bi = row(ln_in_g), row(ln_in_b)

    meta_pad = jnp.pad(meta.astype(F32), ((0, META_PAD - N_META), (0, 0)))
    sel = _bias_placement()
    qkv_m, u_m, c_m, _ = _proj(meta_pad, gi, bi, w_proj, bf, sel, tm=META_PAD, tiles_per_seq=1)
    d_meta = (c_m[:N_META, :FOX_HEADS] - c_m[N_META - 1:N_META, :FOX_HEADS]) * LOG2E
    d_m = jnp.pad(d_meta.T, ((0, 0), (0, META_PAD - N_META)), constant_values=MASKED_BIAS)
    d_m = d_m.reshape(FOX_HEADS, 1, META_PAD)

    qkv, u, _, kaug = _proj(x2, gi, bi, w_proj, bf, sel, tm=tile, tiles_per_seq=seq // tile)

    o_pool = _pool(u, u_m[:N_META], pool_w[0].astype(BF16), row(pool_scale[0]), batch=batch, seq=seq, tp=tile)
    o_fox = _attention(qkv, kaug, qkv_m, d_m, batch=batch, seq=seq, th=tile, tk=tile)

    w_r = jnp.pad(jnp.concatenate([w_router_g[0], w_router_e[0]], axis=1),
                  ((0, 0), (0, LANES - N_GROUPS - N_EXPERTS)))
    w_r_hi = w_r.astype(BF16)
    w_r = jnp.concatenate([w_r_hi, (w_r - w_r_hi.astype(F32)).astype(BF16)], axis=1)
    b_r = jnp.pad(jnp.concatenate([b_router_g[0], b_router_e[0]]), (0, LANES - N_GROUPS - N_EXPERTS)).reshape(1, LANES)
    h1, h1p, ri, rg, cnt = _mix(x2, o_fox, o_pool, w_out[0].astype(BF16), gi, bi, row(ln_mix_g[0]),
                                row(ln_mix_b[0]), w_r, b_r, tm=tile)

    te = 256
    blk_e, n_act, rows_tok, rows_dst = _dispatch_tables(ri, cnt, rows=rows, te=te)
    y = _moe(blk_e, n_act, rows_tok, rows_dst, h1p, w13[0].astype(BF16), w2[0].astype(BF16), te=te)

    out = _final(h1, y, rg, row(ln_ffn_g[0]), row(ln_ffn_b[0]), tm=tile)
    return out.reshape(batch, seq, d_model)
```

```python
import functools

import jax
import jax.numpy as jnp
from jax import lax
from jax.experimental import pallas as pl
from jax.experimental.pallas import tpu as pltpu

F32 = jnp.float32
BF16 = jnp.bfloat16

N_META = 16
FOX_HEADS = 8
HEAD_DIM = 128
D_FOX = FOX_HEADS * HEAD_DIM
POOL_WINDOWS = (2, 4, 8, 16)
POOL_GROUP_DIM = 256
D_POOL = len(POOL_WINDOWS) * POOL_GROUP_DIM
N_GROUPS = 4
EXPERTS_PER_GROUP = 8
N_EXPERTS = N_GROUPS * EXPERTS_PER_GROUP
D_EXPERT = 1024
LN_EPS = 1e-5
DEEPNORM_ALPHA = 2.0 ** 0.25

LANES = 128
SLAB_ROWS = 8
META_PAD = 128
MASKED_BIAS = 1e30
LOG2E = 1.4426950408889634
BIAS_TERMS = 3
VMEM_LIMIT = 56 * 1024 * 1024


def _params(n_axes, vmem=VMEM_LIMIT):
    return pltpu.CompilerParams(dimension_semantics=("arbitrary",) * n_axes, vmem_limit_bytes=vmem)


def _layer_norm(x, g, b):
    mu = jnp.mean(x, axis=-1, keepdims=True)
    xc = x - mu
    var = jnp.mean(xc * xc, axis=-1, keepdims=True)
    return xc * lax.rsqrt(var + LN_EPS) * g + b


def _pack_halves(x):
    n = x.shape[1] // 2
    bits = lambda v: lax.bitcast_convert_type(v.astype(BF16).astype(F32), jnp.uint32)
    return (bits(x[:, :n]) >> 16) | (bits(x[:, n:]) & jnp.uint32(0xFFFF0000))


def _unpack_halves(w):
    lo = lax.bitcast_convert_type(w << 16, F32)
    hi = lax.bitcast_convert_type(w & jnp.uint32(0xFFFF0000), F32)
    return jnp.concatenate([lo, hi], axis=1)


def _store_slabs(ref, words):
    rows = words.shape[0]
    for k in range(SLAB_ROWS):
        ref[pl.ds(k, rows, stride=SLAB_ROWS), :] = words[:, k * LANES:(k + 1) * LANES]


def _load_slabs(ref):
    rows = ref.shape[0] // SLAB_ROWS
    return jnp.concatenate([ref[pl.ds(k, rows, stride=SLAB_ROWS), :] for k in range(SLAB_ROWS)], axis=1)


def _row_tiles_to_value(ref):
    rt, lt = ref.shape[0], ref.shape[1]
    return jnp.concatenate([ref[:, j].reshape(rt * SLAB_ROWS, LANES) for j in range(lt)], axis=1)


def _value_to_row_tiles(ref, value):
    rt, lt = ref.shape[0], ref.shape[1]
    for j in range(lt):
        ref[:, j] = value[:, j * LANES:(j + 1) * LANES].reshape(rt, SLAB_ROWS, LANES)


def _proj_kernel(x_ref, g_ref, b_ref, w_ref, bf_ref, sel_ref, qkv_ref, u_ref, d_ref, ka_ref, carry_ref,
                 *, tiles_per_seq, chunk):
    i = pl.program_id(0)
    tm = x_ref.shape[0]
    xn = _layer_norm(x_ref[...], g_ref[...], b_ref[...]).astype(BF16)
    scale = HEAD_DIM ** -0.5 * LOG2E
    for c in range(0, 3 * D_FOX, chunk):
        acc = jnp.dot(xn, w_ref[:, c:c + chunk], preferred_element_type=F32)
        if c < D_FOX:
            acc = acc * scale
        qkv_ref[:, c:c + chunk] = acc.astype(BF16)
    for c in range(0, D_POOL, chunk):
        u_ref[:, c:c + chunk] = jnp.dot(xn, w_ref[:, 3 * D_FOX + c:3 * D_FOX + c + chunk],
                                        preferred_element_type=F32)
    fl = jnp.dot(xn, w_ref[:, 3 * D_FOX + D_POOL:], preferred_element_type=F32) + bf_ref[...]
    lf = jnp.minimum(fl, 0.0) - jnp.log1p(jnp.exp(-jnp.abs(fl)))

    @pl.when(i % tiles_per_seq == 0)
    def _():
        carry_ref[...] = jnp.zeros_like(carry_ref)

    row = lax.broadcasted_iota(jnp.int32, lf.shape, 0)
    acc = lf
    k = 1
    while k < tm:
        acc = acc + jnp.where(row >= k, pltpu.roll(acc, k, 0), 0.0)
        k *= 2
    d = acc + carry_ref[...]
    d_ref[...] = d
    carry_ref[...] = d[tm - 1:tm, :]

    nd = d * (-LOG2E)
    hi = nd.astype(BF16)
    r1 = nd - hi.astype(F32)
    mid = r1.astype(BF16)
    lo = (r1 - mid.astype(F32)).astype(BF16)
    split = jnp.concatenate([hi, mid, lo], axis=1)
    ka_ref[...] = jnp.dot(split, sel_ref[...], preferred_element_type=F32).astype(BF16)


def _bias_placement():
    t, h = jnp.meshgrid(jnp.arange(BIAS_TERMS), jnp.arange(FOX_HEADS), indexing="ij")
    sel = jnp.zeros((BIAS_TERMS * LANES, D_FOX), F32)
    return sel.at[(t * LANES + h).ravel(), (h * HEAD_DIM + t).ravel()].set(1.0).astype(BF16)


def _proj(x2, g, b, w, bf, sel, *, tm, tiles_per_seq):
    rows, d_model = x2.shape
    n_proj = w.shape[1]
    kern = functools.partial(_proj_kernel, tiles_per_seq=tiles_per_seq, chunk=512)
    return pl.pallas_call(
        kern,
        grid=(rows // tm,),
        in_specs=[
            pl.BlockSpec((tm, d_model), lambda i: (i, 0)),
            pl.BlockSpec((1, d_model), lambda i: (0, 0)),
            pl.BlockSpec((1, d_model), lambda i: (0, 0)),
            pl.BlockSpec((d_model, n_proj), lambda i: (0, 0)),
            pl.BlockSpec((1, LANES), lambda i: (0, 0)),
            pl.BlockSpec(sel.shape, lambda i: (0, 0)),
        ],
        out_specs=[
            pl.BlockSpec((tm, 3 * D_FOX), lambda i: (i, 0)),
            pl.BlockSpec((tm, D_POOL), lambda i: (i, 0)),
            pl.BlockSpec((tm, LANES), lambda i: (i, 0)),
            pl.BlockSpec((tm, D_FOX), lambda i: (i, 0)),
        ],
        out_shape=[
            jax.ShapeDtypeStruct((rows, 3 * D_FOX), BF16),
            jax.ShapeDtypeStruct((rows, D_POOL), F32),
            jax.ShapeDtypeStruct((rows, LANES), F32),
            jax.ShapeDtypeStruct((rows, D_FOX), BF16),
        ],
        scratch_shapes=[pltpu.VMEM((1, LANES), F32)],
        compiler_params=_params(1),
        name="proj",
    )(x2, g, b, w, bf, sel)


def _pool_kernel(u_ref, halo_ref, um_ref, pw_ref, ps_ref, o_ref, ext_ref):
    i = pl.program_id(1)
    tp = u_ref.shape[0]
    halo = jnp.where(i == 0, um_ref[...], halo_ref[...])
    ext_ref[0:N_META, :] = halo
    ext_ref[N_META:, :] = u_ref[...]
    for g, w in enumerate(POOL_WINDOWS):
        cols = slice(g * POOL_GROUP_DIM, (g + 1) * POOL_GROUP_DIM)
        tok = ext_ref[N_META:N_META + tp, cols]
        acc = tok
        for j in range(1, w):
            acc = acc + ext_ref[N_META - j:N_META - j + tp, cols]
        pooled = acc * (1.0 / w) - tok
        mixed = jnp.dot(pooled.astype(BF16), pw_ref[g], preferred_element_type=F32)
        o_ref[:, cols] = (mixed * ps_ref[:, cols]).astype(BF16)


def _pool(u, um, pw, ps, *, batch, seq, tp):
    tiles = seq // tp
    halo_blocks = tp // N_META
    return pl.pallas_call(
        _pool_kernel,
        grid=(batch, tiles),
        in_specs=[
            pl.BlockSpec((tp, D_POOL), lambda b, i: (b * tiles + i, 0)),
            pl.BlockSpec((N_META, D_POOL), lambda b, i: (jnp.maximum((b * tiles + i) * halo_blocks - 1, 0), 0)),
            pl.BlockSpec((N_META, D_POOL), lambda b, i: (0, 0)),
            pl.BlockSpec(pw.shape, lambda b, i: (0, 0, 0)),
            pl.BlockSpec((1, D_POOL), lambda b, i: (0, 0)),
        ],
        out_specs=pl.BlockSpec((tp, D_POOL), lambda b, i: (b * tiles + i, 0)),
        out_shape=jax.ShapeDtypeStruct((batch * seq, D_POOL), BF16),
        scratch_shapes=[pltpu.VMEM((N_META + tp, D_POOL), F32)],
        compiler_params=_params(2),
        name="pool",
    )(u, u, um, pw, ps)


def _attn_kernel(q_ref, k_ref, ka_ref, v_ref, km_ref, vm_ref, dm_ref, o_ref, sa_ref, sb_ref, *, th, tk):
    qi = pl.program_id(2)
    nt = (((1,), (1,)), ((), ()))
    lane = lax.broadcasted_iota(jnp.int32, (th, HEAD_DIM), 1)
    ones = jnp.where(lane < BIAS_TERMS, 1.0, 0.0).astype(BF16)
    km = km_ref[...]
    vm = vm_ref[...]

    def start(half):
        q = q_ref[half * th:(half + 1) * th, :]
        s = lax.dot_general(q, km, nt, preferred_element_type=F32) - dm_ref[0]
        m = jnp.max(s, axis=-1, keepdims=True)
        p = jnp.exp2(s - m)
        l = jnp.sum(p, axis=-1, keepdims=True)
        acc = jnp.dot(p.astype(BF16), vm, preferred_element_type=F32)
        return jnp.concatenate([q, ones], axis=1), (m, l, acc)

    def scores(j, half):
        ks = pl.multiple_of(j * tk, tk)
        ka = jnp.concatenate([k_ref[pl.ds(ks, tk), :], ka_ref[pl.ds(ks, tk), :]], axis=1)
        return lax.dot_general(qa[half], ka, nt, preferred_element_type=F32)

    def update(read_s, j, carry, masked=False):
        m, l, acc = carry
        ks = pl.multiple_of(j * tk, tk)
        if masked:
            row = lax.broadcasted_iota(jnp.int32, (th, tk), 0)
            col = lax.broadcasted_iota(jnp.int32, (th, tk), 1)
            read = lambda: jnp.where(col <= row, read_s(), -jnp.inf)
        else:
            read = read_s
        m_new = jnp.maximum(m, jnp.max(read(), axis=-1, keepdims=True))
        a = jnp.exp2(m - m_new)
        p = jnp.exp2(read() - m_new)
        l = a * l + jnp.sum(p, axis=-1, keepdims=True)
        acc = a * acc + jnp.dot(p.astype(BF16), v_ref[pl.ds(ks, tk), :], preferred_element_type=F32)
        return m_new, l, acc

    qa, carry = zip(start(0), start(1))

    def fill(buf, j):
        for h in range(2):
            buf[h] = scores(j, h)

    def drain(buf, j, carry):
        return tuple(update(lambda h=h: buf[h], j, carry[h]) for h in range(2))

    def body(jj, carry):
        j = 2 * jj
        fill(sb_ref, j + 1)
        carry = drain(sa_ref, j, carry)
        fill(sa_ref, j + 2)
        return drain(sb_ref, j + 1, carry)

    first = 2 * qi
    fill(sa_ref, 0)
    c0, c1 = lax.fori_loop(0, qi, body, tuple(carry))
    s11 = scores(first + 1, 1)
    c0 = update(lambda: sa_ref[0], first, c0, masked=True)
    c1 = update(lambda: sa_ref[1], first, c1)
    c1 = update(lambda: s11, first + 1, c1, masked=True)
    for h, (m, l, acc) in enumerate((c0, c1)):
        o_ref[h * th:(h + 1) * th, :] = (acc / l).astype(BF16)


def _attention(qkv, kaug, qkv_m, d_m, *, batch, seq, th, tk):
    assert th == tk
    tq = 2 * th
    nq = seq // tq
    kern = functools.partial(_attn_kernel, th=th, tk=tk)
    return pl.pallas_call(
        kern,
        grid=(batch, FOX_HEADS, nq),
        in_specs=[
            pl.BlockSpec((tq, HEAD_DIM), lambda b, h, i: (b * nq + i, h)),
            pl.BlockSpec((seq, HEAD_DIM), lambda b, h, i: (b, FOX_HEADS + h)),
            pl.BlockSpec((seq, HEAD_DIM), lambda b, h, i: (b, h)),
            pl.BlockSpec((seq, HEAD_DIM), lambda b, h, i: (b, 2 * FOX_HEADS + h)),
            pl.BlockSpec((META_PAD, HEAD_DIM), lambda b, h, i: (0, FOX_HEADS + h)),
            pl.BlockSpec((META_PAD, HEAD_DIM), lambda b, h, i: (0, 2 * FOX_HEADS + h)),
            pl.BlockSpec((1, 1, META_PAD), lambda b, h, i: (h, 0, 0)),
        ],
        out_specs=pl.BlockSpec((tq, HEAD_DIM), lambda b, h, i: (b * nq + i, h)),
        out_shape=jax.ShapeDtypeStruct((batch * seq, D_FOX), BF16),
        scratch_shapes=[pltpu.VMEM((2, th, tk), F32), pltpu.VMEM((2, th, tk), F32)],
        compiler_params=_params(3),
        name="attn",
    )(qkv, qkv, kaug, qkv, qkv_m, qkv_m, d_m)


def _mix_kernel(x_ref, of_ref, op_ref, wo_ref, gi_ref, bi_ref, gm_ref, bm_ref, wr_ref, br_ref, tri_ref,
                h_ref, hp_ref, ri_ref, rg_ref, cnt_ref, carry_ref, *, chunk):
    i = pl.program_id(0)
    tm = x_ref.shape[0]
    d_model = x_ref.shape[1]
    h0 = _layer_norm(x_ref[...], gi_ref[...], bi_ref[...])
    of = of_ref[...]
    op = op_ref[...]
    for c in range(0, d_model, chunk):
        mix = jnp.dot(of, wo_ref[0:D_FOX, c:c + chunk], preferred_element_type=F32)
        mix = mix + jnp.dot(op, wo_ref[D_FOX:, c:c + chunk], preferred_element_type=F32)
        h_ref[:, c:c + chunk] = DEEPNORM_ALPHA * h0[:, c:c + chunk] + mix
    h1 = _layer_norm(h_ref[...], gm_ref[...], bm_ref[...])
    h_ref[...] = h1
    _store_slabs(hp_ref, _pack_halves(h1))

    h_hi = h1.astype(BF16)
    h_mid = (h1 - h_hi.astype(F32)).astype(BF16)
    hh = jnp.dot(h_hi, wr_ref[...], preferred_element_type=F32)
    mh = jnp.dot(h_mid, wr_ref[:, :LANES], preferred_element_type=F32)
    logits = hh[:, :LANES] + hh[:, LANES:] + mh + br_ref[...]
    lane = lax.broadcasted_iota(jnp.int32, logits.shape, 1)
    neg = -jnp.inf
    gl = jnp.where(lane < N_GROUPS, logits, neg)
    gmax = jnp.max(gl, axis=-1, keepdims=True)
    g_idx = jnp.min(jnp.where(gl == gmax, lane, LANES), axis=-1, keepdims=True)
    p_g = 1.0 / jnp.sum(jnp.exp(gl - gmax), axis=-1, keepdims=True)
    lo = N_GROUPS + g_idx * EXPERTS_PER_GROUP
    el = jnp.where((lane >= lo) & (lane < lo + EXPERTS_PER_GROUP), logits, neg)
    v1 = jnp.max(el, axis=-1, keepdims=True)
    i1 = jnp.min(jnp.where(el == v1, lane, LANES), axis=-1, keepdims=True)
    el2 = jnp.where(lane == i1, neg, el)
    v2 = jnp.max(el2, axis=-1, keepdims=True)
    i2 = jnp.min(jnp.where(el2 == v2, lane, LANES), axis=-1, keepdims=True)
    t = jnp.exp(v2 - v1)
    gate1 = p_g / (1.0 + t)
    gate2 = gate1 * t
    e1 = i1 - N_GROUPS
    e2 = i2 - N_GROUPS

    @pl.when(i == 0)
    def _():
        carry_ref[...] = jnp.zeros_like(carry_ref)

    lower = tri_ref[...]
    oh1 = (lane == e1).astype(F32)
    oh2 = (lane == e2).astype(F32)
    pre1 = jnp.dot(lower, oh1.astype(BF16), preferred_element_type=F32)
    pre2 = jnp.dot(lower, oh2.astype(BF16), preferred_element_type=F32)
    cnt1 = jnp.sum(oh1, axis=0, keepdims=True)
    cnt2 = jnp.sum(oh2, axis=0, keepdims=True)
    base = carry_ref[...]
    rank1 = jnp.sum((pre1 + base) * oh1, axis=-1, keepdims=True)
    rank2 = jnp.sum((pre2 + base + cnt1) * oh2, axis=-1, keepdims=True)
    total = base + cnt1 + cnt2
    carry_ref[...] = total
    cnt_ref[...] = total

    ri_ref[...] = jnp.where(lane == 0, e1, jnp.where(lane == 1, e2, jnp.where(
        lane == 2, rank1.astype(jnp.int32), jnp.where(lane == 3, rank2.astype(jnp.int32), 0))))
    rg_ref[...] = jnp.where(lane == 0, gate1, jnp.where(lane == 1, gate2, 0.0))


def _mix(x2, o_fox, o_pool, wo, gi, bi, gm, bm, wr, br, *, tm):
    rows, d_model = x2.shape
    idx = jnp.arange(tm, dtype=jnp.int32)
    tri = (idx[None, :] < idx[:, None]).astype(BF16)
    kern = functools.partial(_mix_kernel, chunk=512)
    row_spec = lambda w: pl.BlockSpec((tm, w), lambda i: (i, 0))
    vec_spec = lambda w: pl.BlockSpec((1, w), lambda i: (0, 0))
    return pl.pallas_call(
        kern,
        grid=(rows // tm,),
        in_specs=[
            row_spec(d_model), row_spec(D_FOX), row_spec(D_POOL),
            pl.BlockSpec(wo.shape, lambda i: (0, 0)),
            vec_spec(d_model), vec_spec(d_model), vec_spec(d_model), vec_spec(d_model),
            pl.BlockSpec(wr.shape, lambda i: (0, 0)),
            vec_spec(LANES),
            pl.BlockSpec((tm, tm), lambda i: (0, 0)),
        ],
        out_specs=[row_spec(d_model), pl.BlockSpec((tm * SLAB_ROWS, LANES), lambda i: (i, 0)), row_spec(LANES),
                   row_spec(LANES), vec_spec(LANES)],
        out_shape=[
            jax.ShapeDtypeStruct((rows, d_model), F32),
            jax.ShapeDtypeStruct((rows * SLAB_ROWS, LANES), jnp.uint32),
            jax.ShapeDtypeStruct((rows, LANES), jnp.int32),
            jax.ShapeDtypeStruct((rows, LANES), F32),
            jax.ShapeDtypeStruct((1, LANES), F32),
        ],
        scratch_shapes=[pltpu.VMEM((1, LANES), F32)],
        compiler_params=_params(1),
        name="mix",
    )(x2, o_fox, o_pool, wo, gi, bi, gm, bm, wr, br, tri)


def _moe_kernel(be_ref, na_ref, tok_ref, tokn_ref, dst_ref, h_hbm, w13_ref, w2_ref, y_hbm, xbuf, obuf, gsem, ssem,
                *, chunk):
    i = pl.program_id(0)
    n_act = na_ref[0]
    te = xbuf.shape[1] * SLAB_ROWS
    slot = i % 2

    def row_view(buf, s, r):
        return buf.at[s, r // SLAB_ROWS, :, r % SLAB_ROWS, :]

    def start_gather(idx_ref, s):
        for r in range(te):
            pltpu.make_async_copy(h_hbm.at[idx_ref[0, 0, r]], row_view(xbuf, s, r), gsem.at[s]).start(priority=r % 2)

    def wait_gather(s):
        pltpu.make_async_copy(xbuf.at[1 - s], xbuf.at[s], gsem.at[s]).wait()

    def start_scatter(idx, s):
        for r in range(te):
            pltpu.make_async_copy(row_view(obuf, s, r), y_hbm.at[idx(r)], ssem.at[s]).start(priority=r % 2)

    def wait_scatter(s):
        pltpu.make_async_copy(obuf.at[1 - s], obuf.at[s], ssem.at[s]).wait()

    @pl.when(i < n_act)
    def _():
        @pl.when(i == 0)
        def _():
            start_gather(tok_ref, slot)
            spare = y_hbm.shape[0] - 2 * te
            obuf[...] = jnp.zeros_like(obuf)
            for s in range(2):
                start_scatter(lambda r, s=s: spare + s * te + r, s)
            for s in range(2):
                wait_scatter(s)

        wait_gather(slot)

        @pl.when(i + 1 < n_act)
        def _():
            start_gather(tokn_ref, 1 - slot)

        @pl.when(i >= 2)
        def _():
            wait_scatter(slot)

        x = _unpack_halves(_row_tiles_to_value(xbuf.at[slot])).astype(BF16)
        out = None
        for c in range(0, D_EXPERT, chunk):
            gate = jnp.dot(x, w13_ref[0, :, c:c + chunk], preferred_element_type=F32)
            up = jnp.dot(x, w13_ref[0, :, D_EXPERT + c:D_EXPERT + c + chunk], preferred_element_type=F32)
            hid = (gate / (1.0 + jnp.exp(-gate)) * up).astype(BF16)
            part = jnp.dot(hid, w2_ref[0, c:c + chunk, :], preferred_element_type=F32)
            out = part if out is None else out + part
        _value_to_row_tiles(obuf.at[slot], _pack_halves(out))
        start_scatter(lambda r: dst_ref[0, 0, r], slot)

        @pl.when(i == n_act - 1)
        def _():
            @pl.when(i >= 1)
            def _():
                wait_scatter(1 - slot)
            wait_scatter(slot)


def _moe(blk_e, n_act, rows_tok, rows_dst, h1p, w13, w2, *, te):
    rows = h1p.shape[0] // SLAB_ROWS
    words = SLAB_ROWS * LANES
    d_model = 2 * words
    nb = rows_tok.shape[0]
    kern = functools.partial(_moe_kernel, chunk=256)
    smem_spec = lambda off: pl.BlockSpec((1, 1, te), lambda i, be, na: (jnp.minimum(i + off, nb - 1), 0, 0),
                                         memory_space=pltpu.SMEM)
    tiles = (2, te // SLAB_ROWS, words // LANES, SLAB_ROWS, LANES)
    y = pl.pallas_call(
        kern,
        grid_spec=pltpu.PrefetchScalarGridSpec(
            num_scalar_prefetch=2,
            grid=(nb,),
            in_specs=[
                smem_spec(0), smem_spec(1), smem_spec(0),
                pl.BlockSpec(memory_space=pl.ANY),
                pl.BlockSpec((1, d_model, 2 * D_EXPERT), lambda i, be, na: (be[i], 0, 0)),
                pl.BlockSpec((1, D_EXPERT, d_model), lambda i, be, na: (be[i], 0, 0)),
            ],
            out_specs=pl.BlockSpec(memory_space=pl.ANY),
            scratch_shapes=[
                pltpu.VMEM(tiles, jnp.uint32),
                pltpu.VMEM(tiles, jnp.uint32),
                pltpu.SemaphoreType.DMA((2,)),
                pltpu.SemaphoreType.DMA((2,)),
            ],
        ),
        out_shape=jax.ShapeDtypeStruct((2 * rows + 2 * te, SLAB_ROWS, LANES), jnp.uint32),
        compiler_params=_params(1),
        name="moe",
    )(blk_e, n_act, rows_tok, rows_tok, rows_dst, h1p.reshape(rows, SLAB_ROWS, LANES), w13, w2)
    return y.reshape(-1, LANES)


def _final_kernel(h_ref, y0_ref, y1_ref, rg_ref, g_ref, b_ref, o_ref):
    rg = rg_ref[...]
    ffn = _unpack_halves(_load_slabs(y0_ref)) * rg[:, 0:1] + _unpack_halves(_load_slabs(y1_ref)) * rg[:, 1:2]
    o_ref[...] = _layer_norm(DEEPNORM_ALPHA * h_ref[...] + ffn, g_ref[...], b_ref[...])


def _final(h1, y, rg, g, b, *, tm):
    rows, d_model = h1.shape
    nt = rows // tm
    return pl.pallas_call(
        _final_kernel,
        grid=(nt,),
        in_specs=[
            pl.BlockSpec((tm, d_model), lambda i: (i, 0)),
            pl.BlockSpec((tm * SLAB_ROWS, LANES), lambda i: (i, 0)),
            pl.BlockSpec((tm * SLAB_ROWS, LANES), lambda i: (nt + i, 0)),
            pl.BlockSpec((tm, LANES), lambda i: (i, 0)),
            pl.BlockSpec((1, d_model), lambda i: (0, 0)),
            pl.BlockSpec((1, d_model), lambda i: (0, 0)),
        ],
        out_specs=pl.BlockSpec((tm, d_model), lambda i: (i, 0)),
        out_shape=jax.ShapeDtypeStruct((rows, d_model), F32),
        compiler_params=_params(1),
        name="final",
    )(h1, y, y, rg, g, b)


def _dispatch_tables(ri, cnt, *, rows, te):
    experts = ri[:, 0:2]
    rank = ri[:, 2:4]
    counts = cnt[0, :N_EXPERTS].astype(jnp.int32)
    padded = ((counts + te - 1) // te) * te
    pend = jnp.cumsum(padded)
    pstart = pend - padded
    dest = (pstart[experts] + rank).reshape(-1)
    nb = (2 * rows + N_EXPERTS * (te - 1) + te - 1) // te
    tok = jnp.repeat(jnp.arange(rows, dtype=jnp.int32), 2)
    out_row = tok + jnp.tile(jnp.array([0, rows], jnp.int32), rows)
    pos = jnp.arange(nb * te, dtype=jnp.int32)
    rows_dst = (2 * rows + pos % (2 * te)).at[dest].set(out_row)
    rows_tok = jnp.where(rows_dst < rows, rows_dst, jnp.where(rows_dst < 2 * rows, rows_dst - rows, 0))
    blk_e = jnp.minimum(jnp.sum(pend[None, :] <= (jnp.arange(nb, dtype=jnp.int32) * te)[:, None], axis=1),
                        N_EXPERTS - 1).astype(jnp.int32)
    n_act = (pend[-1] // te).astype(jnp.int32).reshape(1)
    return blk_e, n_act, rows_tok.reshape(nb, 1, te), rows_dst.reshape(nb, 1, te)


def kernel(x, meta, ln_in_g, ln_in_b, w_in, b_f, pool_w, pool_scale, w_out, ln_mix_g, ln_mix_b,
           w_router_g, b_router_g, w_router_e, b_router_e, w13, w2, ln_ffn_g, ln_ffn_b):
    batch, seq, d_model = x.shape
    rows = batch * seq
    tile = min(512, seq)
    assert seq % (2 * tile) == 0 and tile % N_META == 0
    assert w_in.shape[0] == 1, "depth-1 trunk"

    x2 = x.reshape(rows, d_model)
    row = lambda v: v.reshape(1, -1).astype(F32)

    wi = w_in[0]
    w_f = jnp.pad(wi[:, 3 * D_FOX:3 * D_FOX + FOX_HEADS], ((0, 0), (0, LANES - FOX_HEADS)))
    w_proj = jnp.concatenate([wi[:, :3 * D_FOX], wi[:, 3 * D_FOX + FOX_HEADS:], w_f], axis=1).astype(BF16)
    bf = jnp.pad(b_f[0], (0, LANES - FOX_HEADS)).reshape(1, LANES)
    gi, bi = row(ln_in_g), row(ln_in_b)

    meta_pad = jnp.pad(meta.astype(F32), ((0, META_PAD - N_META), (0, 0)))
    sel = _bias_placement()
    qkv_m, u_m, c_m, _ = _proj(meta_pad, gi, bi, w_proj, bf, sel, tm=META_PAD, tiles_per_seq=1)
    d_meta = (c_m[:N_META, :FOX_HEADS] - c_m[N_META - 1:N_META, :FOX_HEADS]) * LOG2E
    d_m = jnp.pad(d_meta.T, ((0, 0), (0, META_PAD - N_META)), constant_values=MASKED_BIAS)
    d_m = d_m.reshape(FOX_HEADS, 1, META_PAD)

    qkv, u, _, kaug = _proj(x2, gi, bi, w_proj, bf, sel, tm=tile, tiles_per_seq=seq // tile)

    o_pool = _pool(u, u_m[:N_META], pool_w[0].astype(BF16), row(pool_scale[0]), batch=batch, seq=seq, tp=tile)
    o_fox = _attention(qkv, kaug, qkv_m, d_m, batch=batch, seq=seq, th=tile, tk=tile)

    w_r = jnp.pad(jnp.concatenate([w_router_g[0], w_router_e[0]], axis=1),
                  ((0, 0), (0, LANES - N_GROUPS - N_EXPERTS)))
    w_r_hi = w_r.astype(BF16)
    w_r = jnp.concatenate([w_r_hi, (w_r - w_r_hi.astype(F32)).astype(BF16)], axis=1)
    b_r = jnp.pad(jnp.concatenate([b_router_g[0], b_router_e[0]]), (0, LANES - N_GROUPS - N_EXPERTS)).reshape(1, LANES)
    h1, h1p, ri, rg, cnt = _mix(x2, o_fox, o_pool, w_out[0].astype(BF16), gi, bi, row(ln_mix_g[0]),
                                row(ln_mix_b[0]), w_r, b_r, tm=tile)

    te = 256
    blk_e, n_act, rows_tok, rows_dst = _dispatch_tables(ri, cnt, rows=rows, te=te)
    y = _moe(blk_e, n_act, rows_tok, rows_dst, h1p, w13[0].astype(BF16), w2[0].astype(BF16), te=te)

    out = _final(h1, y, rg, row(ln_ffn_g[0]), row(ln_ffn_b[0]), tm=tile)
    return out.reshape(batch, seq, d_model)
```

```python
import functools

import jax
import jax.numpy as jnp
from jax import lax
from jax.experimental import pallas as pl
from jax.experimental.pallas import tpu as pltpu

F32 = jnp.float32
BF16 = jnp.bfloat16

N_META = 16
FOX_HEADS = 8
HEAD_DIM = 128
D_FOX = FOX_HEADS * HEAD_DIM
POOL_WINDOWS = (2, 4, 8, 16)
POOL_GROUP_DIM = 256
D_POOL = len(POOL_WINDOWS) * POOL_GROUP_DIM
N_GROUPS = 4
EXPERTS_PER_GROUP = 8
N_EXPERTS = N_GROUPS * EXPERTS_PER_GROUP
D_EXPERT = 1024
LN_EPS = 1e-5
DEEPNORM_ALPHA = 2.0 ** 0.25

LANES = 128
SLAB_ROWS = 8
META_PAD = 128
MASKED_BIAS = 1e30
LOG2E = 1.4426950408889634
BIAS_TERMS = 3
VMEM_LIMIT = 56 * 1024 * 1024
MOE_VMEM_LIMIT = 62 * 1024 * 1024


def _params(n_axes, vmem=VMEM_LIMIT):
    return pltpu.CompilerParams(dimension_semantics=("arbitrary",) * n_axes, vmem_limit_bytes=vmem)


def _layer_norm(x, g, b):
    mu = jnp.mean(x, axis=-1, keepdims=True)
    xc = x - mu
    var = jnp.mean(xc * xc, axis=-1, keepdims=True)
    return xc * lax.rsqrt(var + LN_EPS) * g + b


def _pack_halves(x):
    n = x.shape[1] // 2
    bits = lambda v: lax.bitcast_convert_type(v.astype(BF16).astype(F32), jnp.uint32)
    return (bits(x[:, :n]) >> 16) | (bits(x[:, n:]) & jnp.uint32(0xFFFF0000))


def _unpack_halves(w):
    lo = lax.bitcast_convert_type(w << 16, F32)
    hi = lax.bitcast_convert_type(w & jnp.uint32(0xFFFF0000), F32)
    return jnp.concatenate([lo, hi], axis=1)


def _store_slabs(ref, words):
    rows = words.shape[0]
    for k in range(SLAB_ROWS):
        ref[pl.ds(k, rows, stride=SLAB_ROWS), :] = words[:, k * LANES:(k + 1) * LANES]


def _load_slabs(ref):
    rows = ref.shape[0] // SLAB_ROWS
    return jnp.concatenate([ref[pl.ds(k, rows, stride=SLAB_ROWS), :] for k in range(SLAB_ROWS)], axis=1)


def _row_tiles_to_value(ref):
    rt, lt = ref.shape[0], ref.shape[1]
    return jnp.concatenate([ref[:, j].reshape(rt * SLAB_ROWS, LANES) for j in range(lt)], axis=1)


def _value_to_row_tiles(ref, value):
    rt, lt = ref.shape[0], ref.shape[1]
    for j in range(lt):
        ref[:, j] = value[:, j * LANES:(j + 1) * LANES].reshape(rt, SLAB_ROWS, LANES)


def _proj_kernel(x_ref, g_ref, b_ref, w_ref, bf_ref, sel_ref, qkv_ref, u_ref, d_ref, ka_ref, carry_ref,
                 *, tiles_per_seq, chunk):
    i = pl.program_id(0)
    tm = x_ref.shape[0]
    xn = _layer_norm(x_ref[...], g_ref[...], b_ref[...]).astype(BF16)
    scale = HEAD_DIM ** -0.5 * LOG2E
    for c in range(0, 3 * D_FOX, chunk):
        acc = jnp.dot(xn, w_ref[:, c:c + chunk], preferred_element_type=F32)
        if c < D_FOX:
            acc = acc * scale
        qkv_ref[:, c:c + chunk] = acc.astype(BF16)
    for c in range(0, D_POOL, chunk):
        u_ref[:, c:c + chunk] = jnp.dot(xn, w_ref[:, 3 * D_FOX + c:3 * D_FOX + c + chunk],
                                        preferred_element_type=F32)
    fl = jnp.dot(xn, w_ref[:, 3 * D_FOX + D_POOL:], preferred_element_type=F32) + bf_ref[...]
    lf = jnp.minimum(fl, 0.0) - jnp.log1p(jnp.exp(-jnp.abs(fl)))

    @pl.when(i % tiles_per_seq == 0)
    def _():
        carry_ref[...] = jnp.zeros_like(carry_ref)

    row = lax.broadcasted_iota(jnp.int32, lf.shape, 0)
    acc = lf
    k = 1
    while k < tm:
        acc = acc + jnp.where(row >= k, pltpu.roll(acc, k, 0), 0.0)
        k *= 2
    d = acc + carry_ref[...]
    d_ref[...] = d
    carry_ref[...] = d[tm - 1:tm, :]

    nd = d * (-LOG2E)
    hi = nd.astype(BF16)
    r1 = nd - hi.astype(F32)
    mid = r1.astype(BF16)
    lo = (r1 - mid.astype(F32)).astype(BF16)
    split = jnp.concatenate([hi, mid, lo], axis=1)
    ka_ref[...] = jnp.dot(split, sel_ref[...], preferred_element_type=F32).astype(BF16)


def _bias_placement():
    t, h = jnp.meshgrid(jnp.arange(BIAS_TERMS), jnp.arange(FOX_HEADS), indexing="ij")
    sel = jnp.zeros((BIAS_TERMS * LANES, D_FOX), F32)
    return sel.at[(t * LANES + h).ravel(), (h * HEAD_DIM + t).ravel()].set(1.0).astype(BF16)


def _proj(x2, g, b, w, bf, sel, *, tm, tiles_per_seq):
    rows, d_model = x2.shape
    n_proj = w.shape[1]
    kern = functools.partial(_proj_kernel, tiles_per_seq=tiles_per_seq, chunk=512)
    return pl.pallas_call(
        kern,
        grid=(rows // tm,),
        in_specs=[
            pl.BlockSpec((tm, d_model), lambda i: (i, 0)),
            pl.BlockSpec((1, d_model), lambda i: (0, 0)),
            pl.BlockSpec((1, d_model), lambda i: (0, 0)),
            pl.BlockSpec((d_model, n_proj), lambda i: (0, 0)),
            pl.BlockSpec((1, LANES), lambda i: (0, 0)),
            pl.BlockSpec(sel.shape, lambda i: (0, 0)),
        ],
        out_specs=[
            pl.BlockSpec((tm, 3 * D_FOX), lambda i: (i, 0)),
            pl.BlockSpec((tm, D_POOL), lambda i: (i, 0)),
            pl.BlockSpec((tm, LANES), lambda i: (i, 0)),
            pl.BlockSpec((tm, D_FOX), lambda i: (i, 0)),
        ],
        out_shape=[
            jax.ShapeDtypeStruct((rows, 3 * D_FOX), BF16),
            jax.ShapeDtypeStruct((rows, D_POOL), F32),
            jax.ShapeDtypeStruct((rows, LANES), F32),
            jax.ShapeDtypeStruct((rows, D_FOX), BF16),
        ],
        scratch_shapes=[pltpu.VMEM((1, LANES), F32)],
        compiler_params=_params(1),
        name="proj",
    )(x2, g, b, w, bf, sel)


def _pool_kernel(u_ref, halo_ref, um_ref, pw_ref, ps_ref, o_ref, ext_ref):
    i = pl.program_id(1)
    tp = u_ref.shape[0]
    halo = jnp.where(i == 0, um_ref[...], halo_ref[...])
    ext_ref[0:N_META, :] = halo
    ext_ref[N_META:, :] = u_ref[...]
    for g, w in enumerate(POOL_WINDOWS):
        cols = slice(g * POOL_GROUP_DIM, (g + 1) * POOL_GROUP_DIM)
        tok = ext_ref[N_META:N_META + tp, cols]
        acc = tok
        for j in range(1, w):
            acc = acc + ext_ref[N_META - j:N_META - j + tp, cols]
        pooled = acc * (1.0 / w) - tok
        mixed = jnp.dot(pooled.astype(BF16), pw_ref[g], preferred_element_type=F32)
        o_ref[:, cols] = (mixed * ps_ref[:, cols]).astype(BF16)


def _pool(u, um, pw, ps, *, batch, seq, tp):
    tiles = seq // tp
    halo_blocks = tp // N_META
    return pl.pallas_call(
        _pool_kernel,
        grid=(batch, tiles),
        in_specs=[
            pl.BlockSpec((tp, D_POOL), lambda b, i: (b * tiles + i, 0)),
            pl.BlockSpec((N_META, D_POOL), lambda b, i: (jnp.maximum((b * tiles + i) * halo_blocks - 1, 0), 0)),
            pl.BlockSpec((N_META, D_POOL), lambda b, i: (0, 0)),
            pl.BlockSpec(pw.shape, lambda b, i: (0, 0, 0)),
            pl.BlockSpec((1, D_POOL), lambda b, i: (0, 0)),
        ],
        out_specs=pl.BlockSpec((tp, D_POOL), lambda b, i: (b * tiles + i, 0)),
        out_shape=jax.ShapeDtypeStruct((batch * seq, D_POOL), BF16),
        scratch_shapes=[pltpu.VMEM((N_META + tp, D_POOL), F32)],
        compiler_params=_params(2),
        name="pool",
    )(u, u, um, pw, ps)


def _attn_kernel(q_ref, k_ref, ka_ref, v_ref, km_ref, vm_ref, dm_ref, o_ref, sa_ref, sb_ref, *, th, tk):
    qi = pl.program_id(2)
    nt = (((1,), (1,)), ((), ()))
    lane = lax.broadcasted_iota(jnp.int32, (th, HEAD_DIM), 1)
    ones = jnp.where(lane < BIAS_TERMS, 1.0, 0.0).astype(BF16)
    km = km_ref[...]
    vm = vm_ref[...]

    def start(half):
        q = q_ref[half * th:(half + 1) * th, :]
        s = lax.dot_general(q, km, nt, preferred_element_type=F32) - dm_ref[0]
        m = jnp.max(s, axis=-1, keepdims=True)
        p = jnp.exp2(s - m)
        l = jnp.sum(p, axis=-1, keepdims=True)
        acc = jnp.dot(p.astype(BF16), vm, preferred_element_type=F32)
        return jnp.concatenate([q, ones], axis=1), (m, l, acc)

    def scores(j, half):
        ks = pl.multiple_of(j * tk, tk)
        ka = jnp.concatenate([k_ref[pl.ds(ks, tk), :], ka_ref[pl.ds(ks, tk), :]], axis=1)
        return lax.dot_general(qa[half], ka, nt, preferred_element_type=F32)

    def update(read_s, j, carry, masked=False):
        m, l, acc = carry
        ks = pl.multiple_of(j * tk, tk)
        if masked:
            row = lax.broadcasted_iota(jnp.int32, (th, tk), 0)
            col = lax.broadcasted_iota(jnp.int32, (th, tk), 1)
            read = lambda: jnp.where(col <= row, read_s(), -jnp.inf)
        else:
            read = read_s
        m_new = jnp.maximum(m, jnp.max(read(), axis=-1, keepdims=True))
        a = jnp.exp2(m - m_new)
        p = jnp.exp2(read() - m_new)
        l = a * l + jnp.sum(p, axis=-1, keepdims=True)
        acc = a * acc + jnp.dot(p.astype(BF16), v_ref[pl.ds(ks, tk), :], preferred_element_type=F32)
        return m_new, l, acc

    qa, carry = zip(start(0), start(1))

    def fill(buf, j):
        for h in range(2):
            buf[h] = scores(j, h)

    def drain(buf, j, carry):
        return tuple(update(lambda h=h: buf[h], j, carry[h]) for h in range(2))

    def body(jj, carry):
        j = 2 * jj
        fill(sb_ref, j + 1)
        carry = drain(sa_ref, j, carry)
        fill(sa_ref, j + 2)
        return drain(sb_ref, j + 1, carry)

    first = 2 * qi
    fill(sa_ref, 0)
    c0, c1 = lax.fori_loop(0, qi, body, tuple(carry))
    s11 = scores(first + 1, 1)
    c0 = update(lambda: sa_ref[0], first, c0, masked=True)
    c1 = update(lambda: sa_ref[1], first, c1)
    c1 = update(lambda: s11, first + 1, c1, masked=True)
    for h, (m, l, acc) in enumerate((c0, c1)):
        o_ref[h * th:(h + 1) * th, :] = (acc / l).astype(BF16)


def _attention(qkv, kaug, qkv_m, d_m, *, batch, seq, th, tk):
    assert th == tk
    tq = 2 * th
    nq = seq // tq
    kern = functools.partial(_attn_kernel, th=th, tk=tk)
    return pl.pallas_call(
        kern,
        grid=(batch, FOX_HEADS, nq),
        in_specs=[
            pl.BlockSpec((tq, HEAD_DIM), lambda b, h, i: (b * nq + i, h)),
            pl.BlockSpec((seq, HEAD_DIM), lambda b, h, i: (b, FOX_HEADS + h)),
            pl.BlockSpec((seq, HEAD_DIM), lambda b, h, i: (b, h)),
            pl.BlockSpec((seq, HEAD_DIM), lambda b, h, i: (b, 2 * FOX_HEADS + h)),
            pl.BlockSpec((META_PAD, HEAD_DIM), lambda b, h, i: (0, FOX_HEADS + h)),
            pl.BlockSpec((META_PAD, HEAD_DIM), lambda b, h, i: (0, 2 * FOX_HEADS + h)),
            pl.BlockSpec((1, 1, META_PAD), lambda b, h, i: (h, 0, 0)),
        ],
        out_specs=pl.BlockSpec((tq, HEAD_DIM), lambda b, h, i: (b * nq + i, h)),
        out_shape=jax.ShapeDtypeStruct((batch * seq, D_FOX), BF16),
        scratch_shapes=[pltpu.VMEM((2, th, tk), F32), pltpu.VMEM((2, th, tk), F32)],
        compiler_params=_params(3),
        name="attn",
    )(qkv, qkv, kaug, qkv, qkv_m, qkv_m, d_m)


def _mix_kernel(x_ref, of_ref, op_ref, wo_ref, gi_ref, bi_ref, gm_ref, bm_ref, wr_ref, br_ref, tri_ref,
                h_ref, hp_ref, ri_ref, rg_ref, cnt_ref, carry_ref, *, chunk):
    i = pl.program_id(0)
    tm = x_ref.shape[0]
    d_model = x_ref.shape[1]
    h0 = _layer_norm(x_ref[...], gi_ref[...], bi_ref[...])
    of = of_ref[...]
    op = op_ref[...]
    for c in range(0, d_model, chunk):
        mix = jnp.dot(of, wo_ref[0:D_FOX, c:c + chunk], preferred_element_type=F32)
        mix = mix + jnp.dot(op, wo_ref[D_FOX:, c:c + chunk], preferred_element_type=F32)
        h_ref[:, c:c + chunk] = DEEPNORM_ALPHA * h0[:, c:c + chunk] + mix
    h1 = _layer_norm(h_ref[...], gm_ref[...], bm_ref[...])
    h_ref[...] = h1
    _store_slabs(hp_ref, _pack_halves(h1))

    h_hi = h1.astype(BF16)
    h_mid = (h1 - h_hi.astype(F32)).astype(BF16)
    hh = jnp.dot(h_hi, wr_ref[...], preferred_element_type=F32)
    mh = jnp.dot(h_mid, wr_ref[:, :LANES], preferred_element_type=F32)
    logits = hh[:, :LANES] + hh[:, LANES:] + mh + br_ref[...]
    lane = lax.broadcasted_iota(jnp.int32, logits.shape, 1)
    neg = -jnp.inf
    gl = jnp.where(lane < N_GROUPS, logits, neg)
    gmax = jnp.max(gl, axis=-1, keepdims=True)
    g_idx = jnp.min(jnp.where(gl == gmax, lane, LANES), axis=-1, keepdims=True)
    p_g = 1.0 / jnp.sum(jnp.exp(gl - gmax), axis=-1, keepdims=True)
    lo = N_GROUPS + g_idx * EXPERTS_PER_GROUP
    el = jnp.where((lane >= lo) & (lane < lo + EXPERTS_PER_GROUP), logits, neg)
    v1 = jnp.max(el, axis=-1, keepdims=True)
    i1 = jnp.min(jnp.where(el == v1, lane, LANES), axis=-1, keepdims=True)
    el2 = jnp.where(lane == i1, neg, el)
    v2 = jnp.max(el2, axis=-1, keepdims=True)
    i2 = jnp.min(jnp.where(el2 == v2, lane, LANES), axis=-1, keepdims=True)
    t = jnp.exp(v2 - v1)
    gate1 = p_g / (1.0 + t)
    gate2 = gate1 * t
    e1 = i1 - N_GROUPS
    e2 = i2 - N_GROUPS

    @pl.when(i == 0)
    def _():
        carry_ref[...] = jnp.zeros_like(carry_ref)

    lower = tri_ref[...]
    oh1 = (lane == e1).astype(F32)
    oh2 = (lane == e2).astype(F32)
    pre1 = jnp.dot(lower, oh1.astype(BF16), preferred_element_type=F32)
    pre2 = jnp.dot(lower, oh2.astype(BF16), preferred_element_type=F32)
    cnt1 = jnp.sum(oh1, axis=0, keepdims=True)
    cnt2 = jnp.sum(oh2, axis=0, keepdims=True)
    base = carry_ref[...]
    rank1 = jnp.sum((pre1 + base) * oh1, axis=-1, keepdims=True)
    rank2 = jnp.sum((pre2 + base + cnt1) * oh2, axis=-1, keepdims=True)
    total = base + cnt1 + cnt2
    carry_ref[...] = total
    cnt_ref[...] = total

    ri_ref[...] = jnp.where(lane == 0, e1, jnp.where(lane == 1, e2, jnp.where(
        lane == 2, rank1.astype(jnp.int32), jnp.where(lane == 3, rank2.astype(jnp.int32), 0))))
    rg_ref[...] = jnp.where(lane == 0, gate1, jnp.where(lane == 1, gate2, 0.0))


def _mix(x2, o_fox, o_pool, wo, gi, bi, gm, bm, wr, br, *, tm):
    rows, d_model = x2.shape
    idx = jnp.arange(tm, dtype=jnp.int32)
    tri = (idx[None, :] < idx[:, None]).astype(BF16)
    kern = functools.partial(_mix_kernel, chunk=512)
    row_spec = lambda w: pl.BlockSpec((tm, w), lambda i: (i, 0))
    vec_spec = lambda w: pl.BlockSpec((1, w), lambda i: (0, 0))
    return pl.pallas_call(
        kern,
        grid=(rows // tm,),
        in_specs=[
            row_spec(d_model), row_spec(D_FOX), row_spec(D_POOL),
            pl.BlockSpec(wo.shape, lambda i: (0, 0)),
            vec_spec(d_model), vec_spec(d_model), vec_spec(d_model), vec_spec(d_model),
            pl.BlockSpec(wr.shape, lambda i: (0, 0)),
            vec_spec(LANES),
            pl.BlockSpec((tm, tm), lambda i: (0, 0)),
        ],
        out_specs=[row_spec(d_model), pl.BlockSpec((tm * SLAB_ROWS, LANES), lambda i: (i, 0)), row_spec(LANES),
                   row_spec(LANES), vec_spec(LANES)],
        out_shape=[
            jax.ShapeDtypeStruct((rows, d_model), F32),
            jax.ShapeDtypeStruct((rows * SLAB_ROWS, LANES), jnp.uint32),
            jax.ShapeDtypeStruct((rows, LANES), jnp.int32),
            jax.ShapeDtypeStruct((rows, LANES), F32),
            jax.ShapeDtypeStruct((1, LANES), F32),
        ],
        scratch_shapes=[pltpu.VMEM((1, LANES), F32)],
        compiler_params=_params(1),
        name="mix",
    )(x2, o_fox, o_pool, wo, gi, bi, gm, bm, wr, br, tri)


def _moe_kernel(be_ref, na_ref, tok_ref, tokn_ref, dst_ref, h_hbm, w13_ref, w2_ref, y_hbm, xbuf, obuf, gsem, ssem,
                *, chunk):
    i = pl.program_id(0)
    n_act = na_ref[0]
    te = xbuf.shape[1] * SLAB_ROWS
    slot = i % 2

    def row_view(buf, s, r):
        return buf.at[s, r // SLAB_ROWS, :, r % SLAB_ROWS, :]

    def start_gather(idx_ref, s):
        for r in range(te):
            pltpu.make_async_copy(h_hbm.at[idx_ref[0, 0, r]], row_view(xbuf, s, r), gsem.at[s]).start(priority=r % 2)

    def wait_gather(s):
        pltpu.make_async_copy(xbuf.at[1 - s], xbuf.at[s], gsem.at[s]).wait()

    def start_scatter(idx, s):
        for r in range(te):
            pltpu.make_async_copy(row_view(obuf, s, r), y_hbm.at[idx(r)], ssem.at[s]).start(priority=r % 2)

    def wait_scatter(s):
        pltpu.make_async_copy(obuf.at[1 - s], obuf.at[s], ssem.at[s]).wait()

    @pl.when(i < n_act)
    def _():
        @pl.when(i == 0)
        def _():
            start_gather(tok_ref, slot)
            spare = y_hbm.shape[0] - 2 * te
            obuf[...] = jnp.zeros_like(obuf)
            for s in range(2):
                start_scatter(lambda r, s=s: spare + s * te + r, s)
            for s in range(2):
                wait_scatter(s)

        wait_gather(slot)

        @pl.when(i + 1 < n_act)
        def _():
            start_gather(tokn_ref, 1 - slot)

        @pl.when(i >= 2)
        def _():
            wait_scatter(slot)

        x = _unpack_halves(_row_tiles_to_value(xbuf.at[slot])).astype(BF16)
        out = None
        for c in range(0, D_EXPERT, chunk):
            gate = jnp.dot(x, w13_ref[0, :, c:c + chunk].astype(BF16), preferred_element_type=F32)
            up = jnp.dot(x, w13_ref[0, :, D_EXPERT + c:D_EXPERT + c + chunk].astype(BF16),
                         preferred_element_type=F32)
            hid = (gate / (1.0 + jnp.exp(-gate)) * up).astype(BF16)
            part = jnp.dot(hid, w2_ref[0, c:c + chunk, :].astype(BF16), preferred_element_type=F32)
            out = part if out is None else out + part
        _value_to_row_tiles(obuf.at[slot], _pack_halves(out))
        start_scatter(lambda r: dst_ref[0, 0, r], slot)

        @pl.when(i == n_act - 1)
        def _():
            @pl.when(i >= 1)
            def _():
                wait_scatter(1 - slot)
            wait_scatter(slot)


def _moe(blk_e, n_act, rows_tok, rows_dst, h1p, w13, w2, *, te):
    rows = h1p.shape[0] // SLAB_ROWS
    words = SLAB_ROWS * LANES
    d_model = 2 * words
    nb = rows_tok.shape[0]
    kern = functools.partial(_moe_kernel, chunk=256)
    smem_spec = lambda off: pl.BlockSpec((1, 1, te), lambda i, be, na: (jnp.minimum(i + off, nb - 1), 0, 0),
                                         memory_space=pltpu.SMEM)
    tiles = (2, te // SLAB_ROWS, words // LANES, SLAB_ROWS, LANES)
    y = pl.pallas_call(
        kern,
        grid_spec=pltpu.PrefetchScalarGridSpec(
            num_scalar_prefetch=2,
            grid=(nb,),
            in_specs=[
                smem_spec(0), smem_spec(1), smem_spec(0),
                pl.BlockSpec(memory_space=pl.ANY),
                pl.BlockSpec((1, d_model, 2 * D_EXPERT), lambda i, be, na: (be[i], 0, 0)),
                pl.BlockSpec((1, D_EXPERT, d_model), lambda i, be, na: (be[i], 0, 0)),
            ],
            out_specs=pl.BlockSpec(memory_space=pl.ANY),
            scratch_shapes=[
                pltpu.VMEM(tiles, jnp.uint32),
                pltpu.VMEM(tiles, jnp.uint32),
                pltpu.SemaphoreType.DMA((2,)),
                pltpu.SemaphoreType.DMA((2,)),
            ],
        ),
        out_shape=jax.ShapeDtypeStruct((2 * rows + 2 * te, SLAB_ROWS, LANES), jnp.uint32),
        compiler_params=_params(1, vmem=MOE_VMEM_LIMIT),
        name="moe",
    )(blk_e, n_act, rows_tok, rows_tok, rows_dst, h1p.reshape(rows, SLAB_ROWS, LANES), w13, w2)
    return y.reshape(-1, LANES)


def _final_kernel(h_ref, y0_ref, y1_ref, rg_ref, g_ref, b_ref, o_ref):
    rg = rg_ref[...]
    ffn = _unpack_halves(_load_slabs(y0_ref)) * rg[:, 0:1] + _unpack_halves(_load_slabs(y1_ref)) * rg[:, 1:2]
    o_ref[...] = _layer_norm(DEEPNORM_ALPHA * h_ref[...] + ffn, g_ref[...], b_ref[...])


def _final(h1, y, rg, g, b, *, tm):
    rows, d_model = h1.shape
    nt = rows // tm
    return pl.pallas_call(
        _final_kernel,
        grid=(nt,),
        in_specs=[
            pl.BlockSpec((tm, d_model), lambda i: (i, 0)),
            pl.BlockSpec((tm * SLAB_ROWS, LANES), lambda i: (i, 0)),
            pl.BlockSpec((tm * SLAB_ROWS, LANES), lambda i: (nt + i, 0)),
            pl.BlockSpec((tm, LANES), lambda i: (i, 0)),
            pl.BlockSpec((1, d_model), lambda i: (0, 0)),
            pl.BlockSpec((1, d_model), lambda i: (0, 0)),
        ],
        out_specs=pl.BlockSpec((tm, d_model), lambda i: (i, 0)),
        out_shape=jax.ShapeDtypeStruct((rows, d_model), F32),
        compiler_params=_params(1),
        name="final",
    )(h1, y, y, rg, g, b)


def _dispatch_tables(ri, cnt, *, rows, te):
    experts = ri[:, 0:2]
    rank = ri[:, 2:4]
    counts = cnt[0, :N_EXPERTS].astype(jnp.int32)
    padded = ((counts + te - 1) // te) * te
    pend = jnp.cumsum(padded)
    pstart = pend - padded
    dest = (pstart[experts] + rank).reshape(-1)
    nb = (2 * rows + N_EXPERTS * (te - 1) + te - 1) // te
    tok = jnp.repeat(jnp.arange(rows, dtype=jnp.int32), 2)
    out_row = tok + jnp.tile(jnp.array([0, rows], jnp.int32), rows)
    pos = jnp.arange(nb * te, dtype=jnp.int32)
    rows_dst = (2 * rows + pos % (2 * te)).at[dest].set(out_row)
    rows_tok = jnp.where(rows_dst < rows, rows_dst, jnp.where(rows_dst < 2 * rows, rows_dst - rows, 0))
    blk_e = jnp.minimum(jnp.sum(pend[None, :] <= (jnp.arange(nb, dtype=jnp.int32) * te)[:, None], axis=1),
                        N_EXPERTS - 1).astype(jnp.int32)
    n_act = (pend[-1] // te).astype(jnp.int32).reshape(1)
    return blk_e, n_act, rows_tok.reshape(nb, 1, te), rows_dst.reshape(nb, 1, te)


def kernel(x, meta, ln_in_g, ln_in_b, w_in, b_f, pool_w, pool_scale, w_out, ln_mix_g, ln_mix_b,
           w_router_g, b_router_g, w_router_e, b_router_e, w13, w2, ln_ffn_g, ln_ffn_b):
    batch, seq, d_model = x.shape
    rows = batch * seq
    tile = min(512, seq)
    assert seq % (2 * tile) == 0 and tile % N_META == 0
    assert w_in.shape[0] == 1, "depth-1 trunk"

    x2 = x.reshape(rows, d_model)
    row = lambda v: v.reshape(1, -1).astype(F32)

    wi = w_in[0]
    w_f = jnp.pad(wi[:, 3 * D_FOX:3 * D_FOX + FOX_HEADS], ((0, 0), (0, LANES - FOX_HEADS)))
    w_proj = jnp.concatenate([wi[:, :3 * D_FOX], wi[:, 3 * D_FOX + FOX_HEADS:], w_f], axis=1).astype(BF16)
    bf = jnp.pad(b_f[0], (0, LANES - FOX_HEADS)).reshape(1, LANES)
    gi, bi = row(ln_in_g), row(ln_in_b)

    meta_pad = jnp.pad(meta.astype(F32), ((0, META_PAD - N_META), (0, 0)))
    sel = _bias_placement()
    qkv_m, u_m, c_m, _ = _proj(meta_pad, gi, bi, w_proj, bf, sel, tm=META_PAD, tiles_per_seq=1)
    d_meta = (c_m[:N_META, :FOX_HEADS] - c_m[N_META - 1:N_META, :FOX_HEADS]) * LOG2E
    d_m = jnp.pad(d_meta.T, ((0, 0), (0, META_PAD - N_META)), constant_values=MASKED_BIAS)
    d_m = d_m.reshape(FOX_HEADS, 1, META_PAD)

    qkv, u, _, kaug = _proj(x2, gi, bi, w_proj, bf, sel, tm=tile, tiles_per_seq=seq // tile)

    o_pool = _pool(u, u_m[:N_META], pool_w[0].astype(BF16), row(pool_scale[0]), batch=batch, seq=seq, tp=tile)
    o_fox = _attention(qkv, kaug, qkv_m, d_m, batch=batch, seq=seq, th=tile, tk=tile)

    w_r = jnp.pad(jnp.concatenate([w_router_g[0], w_router_e[0]], axis=1),
                  ((0, 0), (0, LANES - N_GROUPS - N_EXPERTS)))
    w_r_hi = w_r.astype(BF16)
    w_r = jnp.concatenate([w_r_hi, (w_r - w_r_hi.astype(F32)).astype(BF16)], axis=1)
    b_r = jnp.pad(jnp.concatenate([b_router_g[0], b_router_e[0]]), (0, LANES - N_GROUPS - N_EXPERTS)).reshape(1, LANES)
    h1, h1p, ri, rg, cnt = _mix(x2, o_fox, o_pool, w_out[0].astype(BF16), gi, bi, row(ln_mix_g[0]),
                                row(ln_mix_b[0]), w_r, b_r, tm=tile)

    te = 256
    blk_e, n_act, rows_tok, rows_dst = _dispatch_tables(ri, cnt, rows=rows, te=te)
    y = _moe(blk_e, n_act, rows_tok, rows_dst, h1p, w13[0], w2[0], te=te)

    out = _final(h1, y, rg, row(ln_ffn_g[0]), row(ln_ffn_b[0]), tm=tile)
    return out.reshape(batch, seq, d_model)
```

```python
import functools

import jax
import jax.numpy as jnp
from jax import lax
from jax.experimental import pallas as pl
from jax.experimental.pallas import tpu as pltpu

F32 = jnp.float32
BF16 = jnp.bfloat16

N_META = 16
FOX_HEADS = 8
HEAD_DIM = 128
D_FOX = FOX_HEADS * HEAD_DIM
POOL_WINDOWS = (2, 4, 8, 16)
POOL_GROUP_DIM = 256
D_POOL = len(POOL_WINDOWS) * POOL_GROUP_DIM
N_GROUPS = 4
EXPERTS_PER_GROUP = 8
N_EXPERTS = N_GROUPS * EXPERTS_PER_GROUP
D_EXPERT = 1024
LN_EPS = 1e-5
DEEPNORM_ALPHA = 2.0 ** 0.25

LANES = 128
SLAB_ROWS = 8
META_PAD = 128
MASKED_BIAS = 1e30
LOG2E = 1.4426950408889634
BIAS_TERMS = 3
VT_ROWS = 144
VMEM_LIMIT = 56 * 1024 * 1024
MOE_VMEM_LIMIT = 62 * 1024 * 1024


def _params(n_axes, vmem=VMEM_LIMIT):
    return pltpu.CompilerParams(dimension_semantics=("arbitrary",) * n_axes, vmem_limit_bytes=vmem)


def _layer_norm(x, g, b):
    mu = jnp.mean(x, axis=-1, keepdims=True)
    xc = x - mu
    var = jnp.mean(xc * xc, axis=-1, keepdims=True)
    return xc * lax.rsqrt(var + LN_EPS) * g + b


def _pack_halves(x):
    n = x.shape[1] // 2
    bits = lambda v: lax.bitcast_convert_type(v.astype(BF16).astype(F32), jnp.uint32)
    return (bits(x[:, :n]) >> 16) | (bits(x[:, n:]) & jnp.uint32(0xFFFF0000))


def _unpack_halves(w):
    lo = lax.bitcast_convert_type(w << 16, F32)
    hi = lax.bitcast_convert_type(w & jnp.uint32(0xFFFF0000), F32)
    return jnp.concatenate([lo, hi], axis=1)


def _store_slabs(ref, words):
    rows = words.shape[0]
    for k in range(SLAB_ROWS):
        ref[pl.ds(k, rows, stride=SLAB_ROWS), :] = words[:, k * LANES:(k + 1) * LANES]


def _load_slabs(ref):
    rows = ref.shape[0] // SLAB_ROWS
    return jnp.concatenate([ref[pl.ds(k, rows, stride=SLAB_ROWS), :] for k in range(SLAB_ROWS)], axis=1)


def _row_tiles_to_value(ref):
    rt, lt = ref.shape[0], ref.shape[1]
    return jnp.concatenate([ref[:, j].reshape(rt * SLAB_ROWS, LANES) for j in range(lt)], axis=1)


def _value_to_row_tiles(ref, value):
    rt, lt = ref.shape[0], ref.shape[1]
    for j in range(lt):
        ref[:, j] = value[:, j * LANES:(j + 1) * LANES].reshape(rt, SLAB_ROWS, LANES)


def _proj_kernel(x_ref, g_ref, b_ref, w_ref, bf_ref, sel_ref, qkv_ref, u_ref, d_ref, ka_ref, carry_ref,
                 *, tiles_per_seq, chunk):
    i = pl.program_id(0)
    tm = x_ref.shape[0]
    xn = _layer_norm(x_ref[...], g_ref[...], b_ref[...]).astype(BF16)
    scale = HEAD_DIM ** -0.5 * LOG2E
    for c in range(0, 3 * D_FOX, chunk):
        acc = jnp.dot(xn, w_ref[:, c:c + chunk], preferred_element_type=F32)
        if c < D_FOX:
            acc = acc * scale
        qkv_ref[:, c:c + chunk] = acc.astype(BF16)
    for c in range(0, D_POOL, chunk):
        u_ref[:, c:c + chunk] = jnp.dot(xn, w_ref[:, 3 * D_FOX + c:3 * D_FOX + c + chunk],
                                        preferred_element_type=F32)
    fl = jnp.dot(xn, w_ref[:, 3 * D_FOX + D_POOL:], preferred_element_type=F32) + bf_ref[...]
    lf = jnp.minimum(fl, 0.0) - jnp.log1p(jnp.exp(-jnp.abs(fl)))

    @pl.when(i % tiles_per_seq == 0)
    def _():
        carry_ref[...] = jnp.zeros_like(carry_ref)

    row = lax.broadcasted_iota(jnp.int32, lf.shape, 0)
    acc = lf
    k = 1
    while k < tm:
        acc = acc + jnp.where(row >= k, pltpu.roll(acc, k, 0), 0.0)
        k *= 2
    d = acc + carry_ref[...]
    d_ref[...] = d
    carry_ref[...] = d[tm - 1:tm, :]

    nd = d * (-LOG2E)
    hi = nd.astype(BF16)
    r1 = nd - hi.astype(F32)
    mid = r1.astype(BF16)
    lo = (r1 - mid.astype(F32)).astype(BF16)
    split = jnp.concatenate([hi, mid, lo], axis=1)
    ka_ref[...] = jnp.dot(split, sel_ref[...], preferred_element_type=F32).astype(BF16)


def _bias_placement():
    t, h = jnp.meshgrid(jnp.arange(BIAS_TERMS), jnp.arange(FOX_HEADS), indexing="ij")
    sel = jnp.zeros((BIAS_TERMS * LANES, D_FOX), F32)
    return sel.at[(t * LANES + h).ravel(), (h * HEAD_DIM + t).ravel()].set(1.0).astype(BF16)


def _proj(x2, g, b, w, bf, sel, *, tm, tiles_per_seq):
    rows, d_model = x2.shape
    n_proj = w.shape[1]
    kern = functools.partial(_proj_kernel, tiles_per_seq=tiles_per_seq, chunk=512)
    return pl.pallas_call(
        kern,
        grid=(rows // tm,),
        in_specs=[
            pl.BlockSpec((tm, d_model), lambda i: (i, 0)),
            pl.BlockSpec((1, d_model), lambda i: (0, 0)),
            pl.BlockSpec((1, d_model), lambda i: (0, 0)),
            pl.BlockSpec((d_model, n_proj), lambda i: (0, 0)),
            pl.BlockSpec((1, LANES), lambda i: (0, 0)),
            pl.BlockSpec(sel.shape, lambda i: (0, 0)),
        ],
        out_specs=[
            pl.BlockSpec((tm, 3 * D_FOX), lambda i: (i, 0)),
            pl.BlockSpec((tm, D_POOL), lambda i: (i, 0)),
            pl.BlockSpec((tm, LANES), lambda i: (i, 0)),
            pl.BlockSpec((tm, D_FOX), lambda i: (i, 0)),
        ],
        out_shape=[
            jax.ShapeDtypeStruct((rows, 3 * D_FOX), BF16),
            jax.ShapeDtypeStruct((rows, D_POOL), F32),
            jax.ShapeDtypeStruct((rows, LANES), F32),
            jax.ShapeDtypeStruct((rows, D_FOX), BF16),
        ],
        scratch_shapes=[pltpu.VMEM((1, LANES), F32)],
        compiler_params=_params(1),
        name="proj",
    )(x2, g, b, w, bf, sel)


def _pool_kernel(u_ref, halo_ref, um_ref, pw_ref, ps_ref, o_ref, ext_ref):
    i = pl.program_id(1)
    tp = u_ref.shape[0]
    halo = jnp.where(i == 0, um_ref[...], halo_ref[...])
    ext_ref[0:N_META, :] = halo
    ext_ref[N_META:, :] = u_ref[...]
    for g, w in enumerate(POOL_WINDOWS):
        cols = slice(g * POOL_GROUP_DIM, (g + 1) * POOL_GROUP_DIM)
        tok = ext_ref[N_META:N_META + tp, cols]
        acc = tok
        for j in range(1, w):
            acc = acc + ext_ref[N_META - j:N_META - j + tp, cols]
        pooled = acc * (1.0 / w) - tok
        mixed = jnp.dot(pooled.astype(BF16), pw_ref[g], preferred_element_type=F32)
        o_ref[:, cols] = (mixed * ps_ref[:, cols]).astype(BF16)


def _pool(u, um, pw, ps, *, batch, seq, tp):
    tiles = seq // tp
    halo_blocks = tp // N_META
    return pl.pallas_call(
        _pool_kernel,
        grid=(batch, tiles),
        in_specs=[
            pl.BlockSpec((tp, D_POOL), lambda b, i: (b * tiles + i, 0)),
            pl.BlockSpec((N_META, D_POOL), lambda b, i: (jnp.maximum((b * tiles + i) * halo_blocks - 1, 0), 0)),
            pl.BlockSpec((N_META, D_POOL), lambda b, i: (0, 0)),
            pl.BlockSpec(pw.shape, lambda b, i: (0, 0, 0)),
            pl.BlockSpec((1, D_POOL), lambda b, i: (0, 0)),
        ],
        out_specs=pl.BlockSpec((tp, D_POOL), lambda b, i: (b * tiles + i, 0)),
        out_shape=jax.ShapeDtypeStruct((batch * seq, D_POOL), BF16),
        scratch_shapes=[pltpu.VMEM((N_META + tp, D_POOL), F32)],
        compiler_params=_params(2),
        name="pool",
    )(u, u, um, pw, ps)


def _attn_kernel(q_ref, k_ref, ka_ref, vt_ref, km_ref, kam_ref, vtm_ref, o_ref, sa_ref, sb_ref, *, th, tk):
    qi = pl.program_id(2)
    nt = (((1,), (1,)), ((), ()))
    lane = lax.broadcasted_iota(jnp.int32, (th, HEAD_DIM), 1)
    ones = jnp.where(lane < BIAS_TERMS, 1.0, 0.0).astype(BF16)
    qa = [jnp.concatenate([q_ref[h * th:(h + 1) * th, :], ones], axis=1) for h in range(2)]
    vrows = vt_ref.shape[0]

    def scores(keys, half):
        return lax.dot_general(keys, qa[half], nt, preferred_element_type=F32)

    def x_keys(j):
        ks = pl.multiple_of(j * tk, tk)
        return jnp.concatenate([k_ref[pl.ds(ks, tk), :], ka_ref[pl.ds(ks, tk), :]], axis=1)

    def update(read_s, vt, carry, masked=False):
        m, acc = carry
        if masked:
            key = lax.broadcasted_iota(jnp.int32, (tk, th), 0)
            qry = lax.broadcasted_iota(jnp.int32, (tk, th), 1)
            read = lambda: jnp.where(key <= qry, read_s(), -jnp.inf)
        else:
            read = read_s
        m_new = jnp.maximum(m, jnp.max(read(), axis=0, keepdims=True))
        a = jnp.exp2(m - m_new)
        p = jnp.exp2(read() - m_new).astype(BF16)
        return m_new, a * acc + jnp.dot(vt, p, preferred_element_type=F32)

    def x_vt(j):
        return vt_ref[:, pl.ds(pl.multiple_of(j * tk, tk), tk)]

    keys_m = jnp.concatenate([km_ref[...], kam_ref[...]], axis=1)
    carry = []
    for h in range(2):
        s_m = scores(keys_m, h)
        init = (jnp.full((1, th), -jnp.inf, F32), jnp.zeros((vrows, th), F32))
        carry.append(update(lambda: s_m, vtm_ref[...], init))

    def fill(buf, j):
        keys = x_keys(j)
        for h in range(2):
            buf[h] = scores(keys, h)

    def drain(buf, j, carry):
        vt = x_vt(j)
        return tuple(update(lambda h=h: buf[h], vt, carry[h]) for h in range(2))

    def body(jj, carry):
        j = 2 * jj
        fill(sb_ref, j + 1)
        carry = drain(sa_ref, j, carry)
        fill(sa_ref, j + 2)
        return drain(sb_ref, j + 1, carry)

    first = 2 * qi
    fill(sa_ref, 0)
    c0, c1 = lax.fori_loop(0, qi, body, tuple(carry))
    s11 = scores(x_keys(first + 1), 1)
    c0 = update(lambda: sa_ref[0], x_vt(first), c0, masked=True)
    c1 = update(lambda: sa_ref[1], x_vt(first), c1)
    c1 = update(lambda: s11, x_vt(first + 1), c1, masked=True)
    for h, (m, acc) in enumerate((c0, c1)):
        out_t = acc[:HEAD_DIM] / acc[HEAD_DIM:HEAD_DIM + 1]
        o_ref[h * th:(h + 1) * th, :] = out_t.T.astype(BF16)


def _attention(qkv, kaug, vt, qkv_m, kaug_m, vt_m, *, batch, seq, th, tk):
    assert th == tk
    tq = 2 * th
    nq = seq // tq
    vrows = vt.shape[0] // (batch * FOX_HEADS)
    kern = functools.partial(_attn_kernel, th=th, tk=tk)
    return pl.pallas_call(
        kern,
        grid=(batch, FOX_HEADS, nq),
        in_specs=[
            pl.BlockSpec((tq, HEAD_DIM), lambda b, h, i: (b * nq + i, h)),
            pl.BlockSpec((seq, HEAD_DIM), lambda b, h, i: (b, FOX_HEADS + h)),
            pl.BlockSpec((seq, HEAD_DIM), lambda b, h, i: (b, h)),
            pl.BlockSpec((vrows, seq), lambda b, h, i: (b * FOX_HEADS + h, 0)),
            pl.BlockSpec((META_PAD, HEAD_DIM), lambda b, h, i: (0, FOX_HEADS + h)),
            pl.BlockSpec((META_PAD, HEAD_DIM), lambda b, h, i: (0, h)),
            pl.BlockSpec((vrows, META_PAD), lambda b, h, i: (h, 0)),
        ],
        out_specs=pl.BlockSpec((tq, HEAD_DIM), lambda b, h, i: (b * nq + i, h)),
        out_shape=jax.ShapeDtypeStruct((batch * seq, D_FOX), BF16),
        scratch_shapes=[pltpu.VMEM((2, tk, th), F32), pltpu.VMEM((2, tk, th), F32)],
        compiler_params=_params(3),
        name="attn",
    )(qkv, qkv, kaug, vt, qkv_m, kaug_m, vt_m)


def _values_transposed(v, heads_leading):
    rows = v.shape[-2]
    vt = jnp.swapaxes(v.reshape(heads_leading + (rows, FOX_HEADS, HEAD_DIM)), -3, -1)
    vt = jnp.swapaxes(vt, -3, -2)
    pad = jnp.zeros(heads_leading + (FOX_HEADS, VT_ROWS - HEAD_DIM, rows), v.dtype).at[..., 0, :].set(1.0)
    return jnp.concatenate([vt, pad], axis=-2).reshape(-1, rows)


def _bias_terms(neg_bias):
    terms, rest = [], neg_bias
    for _ in range(BIAS_TERMS):
        t = rest.astype(BF16)
        terms.append(t)
        rest = rest - t.astype(F32)
    return jnp.stack(terms)


def _mix_kernel(x_ref, of_ref, op_ref, wo_ref, gi_ref, bi_ref, gm_ref, bm_ref, wr_ref, br_ref, tri_ref,
                h_ref, hp_ref, ri_ref, rg_ref, cnt_ref, carry_ref, *, chunk):
    i = pl.program_id(0)
    tm = x_ref.shape[0]
    d_model = x_ref.shape[1]
    h0 = _layer_norm(x_ref[...], gi_ref[...], bi_ref[...])
    of = of_ref[...]
    op = op_ref[...]
    for c in range(0, d_model, chunk):
        mix = jnp.dot(of, wo_ref[0:D_FOX, c:c + chunk], preferred_element_type=F32)
        mix = mix + jnp.dot(op, wo_ref[D_FOX:, c:c + chunk], preferred_element_type=F32)
        h_ref[:, c:c + chunk] = DEEPNORM_ALPHA * h0[:, c:c + chunk] + mix
    h1 = _layer_norm(h_ref[...], gm_ref[...], bm_ref[...])
    h_ref[...] = h1
    _store_slabs(hp_ref, _pack_halves(h1))

    h_hi = h1.astype(BF16)
    h_mid = (h1 - h_hi.astype(F32)).astype(BF16)
    hh = jnp.dot(h_hi, wr_ref[...], preferred_element_type=F32)
    mh = jnp.dot(h_mid, wr_ref[:, :LANES], preferred_element_type=F32)
    logits = hh[:, :LANES] + hh[:, LANES:] + mh + br_ref[...]
    lane = lax.broadcasted_iota(jnp.int32, logits.shape, 1)
    neg = -jnp.inf
    gl = jnp.where(lane < N_GROUPS, logits, neg)
    gmax = jnp.max(gl, axis=-1, keepdims=True)
    g_idx = jnp.min(jnp.where(gl == gmax, lane, LANES), axis=-1, keepdims=True)
    p_g = 1.0 / jnp.sum(jnp.exp(gl - gmax), axis=-1, keepdims=True)
    lo = N_GROUPS + g_idx * EXPERTS_PER_GROUP
    el = jnp.where((lane >= lo) & (lane < lo + EXPERTS_PER_GROUP), logits, neg)
    v1 = jnp.max(el, axis=-1, keepdims=True)
    i1 = jnp.min(jnp.where(el == v1, lane, LANES), axis=-1, keepdims=True)
    el2 = jnp.where(lane == i1, neg, el)
    v2 = jnp.max(el2, axis=-1, keepdims=True)
    i2 = jnp.min(jnp.where(el2 == v2, lane, LANES), axis=-1, keepdims=True)
    t = jnp.exp(v2 - v1)
    gate1 = p_g / (1.0 + t)
    gate2 = gate1 * t
    e1 = i1 - N_GROUPS
    e2 = i2 - N_GROUPS

    @pl.when(i == 0)
    def _():
        carry_ref[...] = jnp.zeros_like(carry_ref)

    lower = tri_ref[...]
    oh1 = (lane == e1).astype(F32)
    oh2 = (lane == e2).astype(F32)
    pre1 = jnp.dot(lower, oh1.astype(BF16), preferred_element_type=F32)
    pre2 = jnp.dot(lower, oh2.astype(BF16), preferred_element_type=F32)
    cnt1 = jnp.sum(oh1, axis=0, keepdims=True)
    cnt2 = jnp.sum(oh2, axis=0, keepdims=True)
    base = carry_ref[...]
    rank1 = jnp.sum((pre1 + base) * oh1, axis=-1, keepdims=True)
    rank2 = jnp.sum((pre2 + base + cnt1) * oh2, axis=-1, keepdims=True)
    total = base + cnt1 + cnt2
    carry_ref[...] = total
    cnt_ref[...] = total

    ri_ref[...] = jnp.where(lane == 0, e1, jnp.where(lane == 1, e2, jnp.where(
        lane == 2, rank1.astype(jnp.int32), jnp.where(lane == 3, rank2.astype(jnp.int32), 0))))
    rg_ref[...] = jnp.where(lane == 0, gate1, jnp.where(lane == 1, gate2, 0.0))


def _mix(x2, o_fox, o_pool, wo, gi, bi, gm, bm, wr, br, *, tm):
    rows, d_model = x2.shape
    idx = jnp.arange(tm, dtype=jnp.int32)
    tri = (idx[None, :] < idx[:, None]).astype(BF16)
    kern = functools.partial(_mix_kernel, chunk=512)
    row_spec = lambda w: pl.BlockSpec((tm, w), lambda i: (i, 0))
    vec_spec = lambda w: pl.BlockSpec((1, w), lambda i: (0, 0))
    return pl.pallas_call(
        kern,
        grid=(rows // tm,),
        in_specs=[
            row_spec(d_model), row_spec(D_FOX), row_spec(D_POOL),
            pl.BlockSpec(wo.shape, lambda i: (0, 0)),
            vec_spec(d_model), vec_spec(d_model), vec_spec(d_model), vec_spec(d_model),
            pl.BlockSpec(wr.shape, lambda i: (0, 0)),
            vec_spec(LANES),
            pl.BlockSpec((tm, tm), lambda i: (0, 0)),
        ],
        out_specs=[row_spec(d_model), pl.BlockSpec((tm * SLAB_ROWS, LANES), lambda i: (i, 0)), row_spec(LANES),
                   row_spec(LANES), vec_spec(LANES)],
        out_shape=[
            jax.ShapeDtypeStruct((rows, d_model), F32),
            jax.ShapeDtypeStruct((rows * SLAB_ROWS, LANES), jnp.uint32),
            jax.ShapeDtypeStruct((rows, LANES), jnp.int32),
            jax.ShapeDtypeStruct((rows, LANES), F32),
            jax.ShapeDtypeStruct((1, LANES), F32),
        ],
        scratch_shapes=[pltpu.VMEM((1, LANES), F32)],
        compiler_params=_params(1),
        name="mix",
    )(x2, o_fox, o_pool, wo, gi, bi, gm, bm, wr, br, tri)


def _moe_kernel(be_ref, na_ref, tok_ref, tokn_ref, dst_ref, h_hbm, w13_ref, w2_ref, y_hbm, xbuf, obuf, gsem, ssem,
                *, chunk):
    i = pl.program_id(0)
    n_act = na_ref[0]
    te = xbuf.shape[1] * SLAB_ROWS
    slot = i % 2

    def row_view(buf, s, r):
        return buf.at[s, r // SLAB_ROWS, :, r % SLAB_ROWS, :]

    def start_gather(idx_ref, s):
        for r in range(te):
            pltpu.make_async_copy(h_hbm.at[idx_ref[0, 0, r]], row_view(xbuf, s, r), gsem.at[s]).start(priority=r % 2)

    def wait_gather(s):
        pltpu.make_async_copy(xbuf.at[1 - s], xbuf.at[s], gsem.at[s]).wait()

    def start_scatter(idx, s):
        for r in range(te):
            pltpu.make_async_copy(row_view(obuf, s, r), y_hbm.at[idx(r)], ssem.at[s]).start(priority=r % 2)

    def wait_scatter(s):
        pltpu.make_async_copy(obuf.at[1 - s], obuf.at[s], ssem.at[s]).wait()

    @pl.when(i < n_act)
    def _():
        @pl.when(i == 0)
        def _():
            start_gather(tok_ref, slot)
            spare = y_hbm.shape[0] - 2 * te
            obuf[...] = jnp.zeros_like(obuf)
            for s in range(2):
                start_scatter(lambda r, s=s: spare + s * te + r, s)
            for s in range(2):
                wait_scatter(s)

        wait_gather(slot)

        @pl.when(i + 1 < n_act)
        def _():
            start_gather(tokn_ref, 1 - slot)

        @pl.when(i >= 2)
        def _():
            wait_scatter(slot)

        x = _unpack_halves(_row_tiles_to_value(xbuf.at[slot])).astype(BF16)
        out = None
        for c in range(0, D_EXPERT, chunk):
            gate = jnp.dot(x, w13_ref[0, :, c:c + chunk].astype(BF16), preferred_element_type=F32)
            up = jnp.dot(x, w13_ref[0, :, D_EXPERT + c:D_EXPERT + c + chunk].astype(BF16),
                         preferred_element_type=F32)
            hid = (gate / (1.0 + jnp.exp(-gate)) * up).astype(BF16)
            part = jnp.dot(hid, w2_ref[0, c:c + chunk, :].astype(BF16), preferred_element_type=F32)
            out = part if out is None else out + part
        _value_to_row_tiles(obuf.at[slot], _pack_halves(out))
        start_scatter(lambda r: dst_ref[0, 0, r], slot)

        @pl.when(i == n_act - 1)
        def _():
            @pl.when(i >= 1)
            def _():
                wait_scatter(1 - slot)
            wait_scatter(slot)


def _moe(blk_e, n_act, rows_tok, rows_dst, h1p, w13, w2, *, te):
    rows = h1p.shape[0] // SLAB_ROWS
    words = SLAB_ROWS * LANES
    d_model = 2 * words
    nb = rows_tok.shape[0]
    kern = functools.partial(_moe_kernel, chunk=256)
    smem_spec = lambda off: pl.BlockSpec((1, 1, te), lambda i, be, na: (jnp.minimum(i + off, nb - 1), 0, 0),
                                         memory_space=pltpu.SMEM)
    tiles = (2, te // SLAB_ROWS, words // LANES, SLAB_ROWS, LANES)
    y = pl.pallas_call(
        kern,
        grid_spec=pltpu.PrefetchScalarGridSpec(
            num_scalar_prefetch=2,
            grid=(nb,),
            in_specs=[
                smem_spec(0), smem_spec(1), smem_spec(0),
                pl.BlockSpec(memory_space=pl.ANY),
                pl.BlockSpec((1, d_model, 2 * D_EXPERT), lambda i, be, na: (be[i], 0, 0)),
                pl.BlockSpec((1, D_EXPERT, d_model), lambda i, be, na: (be[i], 0, 0)),
            ],
            out_specs=pl.BlockSpec(memory_space=pl.ANY),
            scratch_shapes=[
                pltpu.VMEM(tiles, jnp.uint32),
                pltpu.VMEM(tiles, jnp.uint32),
                pltpu.SemaphoreType.DMA((2,)),
                pltpu.SemaphoreType.DMA((2,)),
            ],
        ),
        out_shape=jax.ShapeDtypeStruct((2 * rows + 2 * te, SLAB_ROWS, LANES), jnp.uint32),
        compiler_params=_params(1, vmem=MOE_VMEM_LIMIT),
        name="moe",
    )(blk_e, n_act, rows_tok, rows_tok, rows_dst, h1p.reshape(rows, SLAB_ROWS, LANES), w13, w2)
    return y.reshape(-1, LANES)


def _final_kernel(h_ref, y0_ref, y1_ref, rg_ref, g_ref, b_ref, o_ref):
    rg = rg_ref[...]
    ffn = _unpack_halves(_load_slabs(y0_ref)) * rg[:, 0:1] + _unpack_halves(_load_slabs(y1_ref)) * rg[:, 1:2]
    o_ref[...] = _layer_norm(DEEPNORM_ALPHA * h_ref[...] + ffn, g_ref[...], b_ref[...])


def _final(h1, y, rg, g, b, *, tm):
    rows, d_model = h1.shape
    nt = rows // tm
    return pl.pallas_call(
        _final_kernel,
        grid=(nt,),
        in_specs=[
            pl.BlockSpec((tm, d_model), lambda i: (i, 0)),
            pl.BlockSpec((tm * SLAB_ROWS, LANES), lambda i: (i, 0)),
            pl.BlockSpec((tm * SLAB_ROWS, LANES), lambda i: (nt + i, 0)),
            pl.BlockSpec((tm, LANES), lambda i: (i, 0)),
            pl.BlockSpec((1, d_model), lambda i: (0, 0)),
            pl.BlockSpec((1, d_model), lambda i: (0, 0)),
        ],
        out_specs=pl.BlockSpec((tm, d_model), lambda i: (i, 0)),
        out_shape=jax.ShapeDtypeStruct((rows, d_model), F32),
        compiler_params=_params(1),
        name="final",
    )(h1, y, y, rg, g, b)


def _dispatch_tables(ri, cnt, *, rows, te):
    experts = ri[:, 0:2]
    rank = ri[:, 2:4]
    counts = cnt[0, :N_EXPERTS].astype(jnp.int32)
    padded = ((counts + te - 1) // te) * te
    pend = jnp.cumsum(padded)
    pstart = pend - padded
    dest = (pstart[experts] + rank).reshape(-1)
    nb = (2 * rows + N_EXPERTS * (te - 1) + te - 1) // te
    tok = jnp.repeat(jnp.arange(rows, dtype=jnp.int32), 2)
    out_row = tok + jnp.tile(jnp.array([0, rows], jnp.int32), rows)
    pos = jnp.arange(nb * te, dtype=jnp.int32)
    rows_dst = (2 * rows + pos % (2 * te)).at[dest].set(out_row)
    rows_tok = jnp.where(rows_dst < rows, rows_dst, jnp.where(rows_dst < 2 * rows, rows_dst - rows, 0))
    blk_e = jnp.minimum(jnp.sum(pend[None, :] <= (jnp.arange(nb, dtype=jnp.int32) * te)[:, None], axis=1),
                        N_EXPERTS - 1).astype(jnp.int32)
    n_act = (pend[-1] // te).astype(jnp.int32).reshape(1)
    return blk_e, n_act, rows_tok.reshape(nb, 1, te), rows_dst.reshape(nb, 1, te)


def kernel(x, meta, ln_in_g, ln_in_b, w_in, b_f, pool_w, pool_scale, w_out, ln_mix_g, ln_mix_b,
           w_router_g, b_router_g, w_router_e, b_router_e, w13, w2, ln_ffn_g, ln_ffn_b):
    batch, seq, d_model = x.shape
    rows = batch * seq
    tile = min(512, seq)
    assert seq % (2 * tile) == 0 and tile % N_META == 0
    assert w_in.shape[0] == 1, "depth-1 trunk"

    x2 = x.reshape(rows, d_model)
    row = lambda v: v.reshape(1, -1).astype(F32)

    wi = w_in[0]
    w_f = jnp.pad(wi[:, 3 * D_FOX:3 * D_FOX + FOX_HEADS], ((0, 0), (0, LANES - FOX_HEADS)))
    w_proj = jnp.concatenate([wi[:, :3 * D_FOX], wi[:, 3 * D_FOX + FOX_HEADS:], w_f], axis=1).astype(BF16)
    bf = jnp.pad(b_f[0], (0, LANES - FOX_HEADS)).reshape(1, LANES)
    gi, bi = row(ln_in_g), row(ln_in_b)

    meta_pad = jnp.pad(meta.astype(F32), ((0, META_PAD - N_META), (0, 0)))
    sel = _bias_placement()
    qkv_m, u_m, c_m, _ = _proj(meta_pad, gi, bi, w_proj, bf, sel, tm=META_PAD, tiles_per_seq=1)
    d_meta = (c_m[:N_META, :FOX_HEADS] - c_m[N_META - 1:N_META, :FOX_HEADS]) * LOG2E
    d_meta = jnp.pad(d_meta, ((0, META_PAD - N_META), (0, 0)), constant_values=MASKED_BIAS)
    terms = _bias_terms(-d_meta)
    kaug_m = jnp.zeros((META_PAD, FOX_HEADS, HEAD_DIM), BF16).at[:, :, :BIAS_TERMS].set(terms.transpose(1, 2, 0))
    kaug_m = kaug_m.reshape(META_PAD, D_FOX)

    qkv, u, _, kaug = _proj(x2, gi, bi, w_proj, bf, sel, tm=tile, tiles_per_seq=seq // tile)

    o_pool = _pool(u, u_m[:N_META], pool_w[0].astype(BF16), row(pool_scale[0]), batch=batch, seq=seq, tp=tile)
    vt = _values_transposed(qkv[:, 2 * D_FOX:].reshape(batch, seq, D_FOX), (batch,))
    vt_m = _values_transposed(qkv_m[:, 2 * D_FOX:], ())
    o_fox = _attention(qkv, kaug, vt, qkv_m, kaug_m, vt_m, batch=batch, seq=seq, th=tile, tk=tile)

    w_r = jnp.pad(jnp.concatenate([w_router_g[0], w_router_e[0]], axis=1),
                  ((0, 0), (0, LANES - N_GROUPS - N_EXPERTS)))
    w_r_hi = w_r.astype(BF16)
    w_r = jnp.concatenate([w_r_hi, (w_r - w_r_hi.astype(F32)).astype(BF16)], axis=1)
    b_r = jnp.pad(jnp.concatenate([b_router_g[0], b_router_e[0]]), (0, LANES - N_GROUPS - N_EXPERTS)).reshape(1, LANES)
    h1, h1p, ri, rg, cnt = _mix(x2, o_fox, o_pool, w_out[0].astype(BF16), gi, bi, row(ln_mix_g[0]),
                                row(ln_mix_b[0]), w_r, b_r, tm=tile)

    te = 256
    blk_e, n_act, rows_tok, rows_dst = _dispatch_tables(ri, cnt, rows=rows, te=te)
    y = _moe(blk_e, n_act, rows_tok, rows_dst, h1p, w13[0], w2[0], te=te)

    out = _final(h1, y, rg, row(ln_ffn_g[0]), row(ln_ffn_b[0]), tm=tile)
    return out.reshape(batch, seq, d_model)
```

```python
import functools

import jax
import jax.numpy as jnp
from jax import lax
from jax.experimental import pallas as pl
from jax.experimental.pallas import tpu as pltpu

F32 = jnp.float32
BF16 = jnp.bfloat16

N_META = 16
FOX_HEADS = 8
HEAD_DIM = 128
D_FOX = FOX_HEADS * HEAD_DIM
POOL_WINDOWS = (2, 4, 8, 16)
POOL_GROUP_DIM = 256
D_POOL = len(POOL_WINDOWS) * POOL_GROUP_DIM
N_GROUPS = 4
EXPERTS_PER_GROUP = 8
N_EXPERTS = N_GROUPS * EXPERTS_PER_GROUP
D_EXPERT = 1024
LN_EPS = 1e-5
DEEPNORM_ALPHA = 2.0 ** 0.25

LANES = 128
SLAB_ROWS = 8
META_PAD = 128
MASKED_BIAS = 1e30
LOG2E = 1.4426950408889634
BIAS_TERMS = 3
VT_ROWS = 144
VMEM_LIMIT = 56 * 1024 * 1024
MOE_VMEM_LIMIT = 62 * 1024 * 1024


def _params(n_axes, vmem=VMEM_LIMIT):
    return pltpu.CompilerParams(dimension_semantics=("arbitrary",) * n_axes, vmem_limit_bytes=vmem)


def _layer_norm(x, g, b):
    mu = jnp.mean(x, axis=-1, keepdims=True)
    xc = x - mu
    var = jnp.mean(xc * xc, axis=-1, keepdims=True)
    return xc * lax.rsqrt(var + LN_EPS) * g + b


def _pack_halves(x):
    n = x.shape[1] // 2
    bits = lambda v: lax.bitcast_convert_type(v.astype(BF16).astype(F32), jnp.uint32)
    return (bits(x[:, :n]) >> 16) | (bits(x[:, n:]) & jnp.uint32(0xFFFF0000))


def _unpack_halves(w):
    lo = lax.bitcast_convert_type(w << 16, F32)
    hi = lax.bitcast_convert_type(w & jnp.uint32(0xFFFF0000), F32)
    return jnp.concatenate([lo, hi], axis=1)


def _store_slabs(ref, words):
    rows = words.shape[0]
    for k in range(SLAB_ROWS):
        ref[pl.ds(k, rows, stride=SLAB_ROWS), :] = words[:, k * LANES:(k + 1) * LANES]


def _load_slabs(ref):
    rows = ref.shape[0] // SLAB_ROWS
    return jnp.concatenate([ref[pl.ds(k, rows, stride=SLAB_ROWS), :] for k in range(SLAB_ROWS)], axis=1)


def _row_tiles_to_value(ref):
    rt, lt = ref.shape[0], ref.shape[1]
    return jnp.concatenate([ref[:, j].reshape(rt * SLAB_ROWS, LANES) for j in range(lt)], axis=1)


def _value_to_row_tiles(ref, value):
    rt, lt = ref.shape[0], ref.shape[1]
    for j in range(lt):
        ref[:, j] = value[:, j * LANES:(j + 1) * LANES].reshape(rt, SLAB_ROWS, LANES)


def _proj_kernel(x_ref, g_ref, b_ref, w_ref, bf_ref, sel_ref, qk_ref, vt_ref, u_ref, d_ref, ka_ref, carry_ref,
                 *, tiles_per_seq, chunk):
    i = pl.program_id(0)
    tm = x_ref.shape[0]
    xn = _layer_norm(x_ref[...], g_ref[...], b_ref[...]).astype(BF16)
    scale = HEAD_DIM ** -0.5 * LOG2E
    for c in range(0, 2 * D_FOX, chunk):
        acc = jnp.dot(xn, w_ref[:, c:c + chunk], preferred_element_type=F32)
        if c < D_FOX:
            acc = acc * scale
        qk_ref[:, c:c + chunk] = acc.astype(BF16)
    sub = lax.broadcasted_iota(jnp.int32, (VT_ROWS - HEAD_DIM, tm), 0)
    tail = jnp.where(sub == 0, 1.0, 0.0).astype(BF16)
    for c in range(0, D_FOX, chunk):
        acc = jnp.dot(xn, w_ref[:, 2 * D_FOX + c:2 * D_FOX + c + chunk], preferred_element_type=F32)
        for hd in range(chunk // HEAD_DIM):
            r0 = (c // HEAD_DIM + hd) * VT_ROWS
            vt_ref[r0:r0 + HEAD_DIM, :] = acc[:, hd * HEAD_DIM:(hd + 1) * HEAD_DIM].T.astype(BF16)
            vt_ref[r0 + HEAD_DIM:r0 + VT_ROWS, :] = tail
    for c in range(0, D_POOL, chunk):
        u_ref[:, c:c + chunk] = jnp.dot(xn, w_ref[:, 3 * D_FOX + c:3 * D_FOX + c + chunk],
                                        preferred_element_type=F32)
    fl = jnp.dot(xn, w_ref[:, 3 * D_FOX + D_POOL:], preferred_element_type=F32) + bf_ref[...]
    lf = jnp.minimum(fl, 0.0) - jnp.log1p(jnp.exp(-jnp.abs(fl)))

    @pl.when(i % tiles_per_seq == 0)
    def _():
        carry_ref[...] = jnp.zeros_like(carry_ref)

    row = lax.broadcasted_iota(jnp.int32, lf.shape, 0)
    acc = lf
    k = 1
    while k < tm:
        acc = acc + jnp.where(row >= k, pltpu.roll(acc, k, 0), 0.0)
        k *= 2
    d = acc + carry_ref[...]
    d_ref[...] = d
    carry_ref[...] = d[tm - 1:tm, :]

    nd = d * (-LOG2E)
    hi = nd.astype(BF16)
    r1 = nd - hi.astype(F32)
    mid = r1.astype(BF16)
    lo = (r1 - mid.astype(F32)).astype(BF16)
    split = jnp.concatenate([hi, mid, lo], axis=1)
    ka_ref[...] = jnp.dot(split, sel_ref[...], preferred_element_type=F32).astype(BF16)


def _bias_placement():
    t, h = jnp.meshgrid(jnp.arange(BIAS_TERMS), jnp.arange(FOX_HEADS), indexing="ij")
    sel = jnp.zeros((BIAS_TERMS * LANES, D_FOX), F32)
    return sel.at[(t * LANES + h).ravel(), (h * HEAD_DIM + t).ravel()].set(1.0).astype(BF16)


def _proj(x2, g, b, w, bf, sel, *, tm, tiles_per_seq):
    rows, d_model = x2.shape
    n_proj = w.shape[1]
    kern = functools.partial(_proj_kernel, tiles_per_seq=tiles_per_seq, chunk=512)
    return pl.pallas_call(
        kern,
        grid=(rows // tm,),
        in_specs=[
            pl.BlockSpec((tm, d_model), lambda i: (i, 0)),
            pl.BlockSpec((1, d_model), lambda i: (0, 0)),
            pl.BlockSpec((1, d_model), lambda i: (0, 0)),
            pl.BlockSpec((d_model, n_proj), lambda i: (0, 0)),
            pl.BlockSpec((1, LANES), lambda i: (0, 0)),
            pl.BlockSpec(sel.shape, lambda i: (0, 0)),
        ],
        out_specs=[
            pl.BlockSpec((tm, 2 * D_FOX), lambda i: (i, 0)),
            pl.BlockSpec((FOX_HEADS * VT_ROWS, tm), lambda i: (i // tiles_per_seq, i % tiles_per_seq)),
            pl.BlockSpec((tm, D_POOL), lambda i: (i, 0)),
            pl.BlockSpec((tm, LANES), lambda i: (i, 0)),
            pl.BlockSpec((tm, D_FOX), lambda i: (i, 0)),
        ],
        out_shape=[
            jax.ShapeDtypeStruct((rows, 2 * D_FOX), BF16),
            jax.ShapeDtypeStruct((rows // (tm * tiles_per_seq) * FOX_HEADS * VT_ROWS, tm * tiles_per_seq), BF16),
            jax.ShapeDtypeStruct((rows, D_POOL), F32),
            jax.ShapeDtypeStruct((rows, LANES), F32),
            jax.ShapeDtypeStruct((rows, D_FOX), BF16),
        ],
        scratch_shapes=[pltpu.VMEM((1, LANES), F32)],
        compiler_params=_params(1),
        name="proj",
    )(x2, g, b, w, bf, sel)


def _pool_kernel(u_ref, halo_ref, um_ref, pw_ref, ps_ref, o_ref, ext_ref):
    i = pl.program_id(1)
    tp = u_ref.shape[0]
    halo = jnp.where(i == 0, um_ref[...], halo_ref[...])
    ext_ref[0:N_META, :] = halo
    ext_ref[N_META:, :] = u_ref[...]
    for g, w in enumerate(POOL_WINDOWS):
        cols = slice(g * POOL_GROUP_DIM, (g + 1) * POOL_GROUP_DIM)
        tok = ext_ref[N_META:N_META + tp, cols]
        acc = tok
        for j in range(1, w):
            acc = acc + ext_ref[N_META - j:N_META - j + tp, cols]
        pooled = acc * (1.0 / w) - tok
        mixed = jnp.dot(pooled.astype(BF16), pw_ref[g], preferred_element_type=F32)
        o_ref[:, cols] = (mixed * ps_ref[:, cols]).astype(BF16)


def _pool(u, um, pw, ps, *, batch, seq, tp):
    tiles = seq // tp
    halo_blocks = tp // N_META
    return pl.pallas_call(
        _pool_kernel,
        grid=(batch, tiles),
        in_specs=[
            pl.BlockSpec((tp, D_POOL), lambda b, i: (b * tiles + i, 0)),
            pl.BlockSpec((N_META, D_POOL), lambda b, i: (jnp.maximum((b * tiles + i) * halo_blocks - 1, 0), 0)),
            pl.BlockSpec((N_META, D_POOL), lambda b, i: (0, 0)),
            pl.BlockSpec(pw.shape, lambda b, i: (0, 0, 0)),
            pl.BlockSpec((1, D_POOL), lambda b, i: (0, 0)),
        ],
        out_specs=pl.BlockSpec((tp, D_POOL), lambda b, i: (b * tiles + i, 0)),
        out_shape=jax.ShapeDtypeStruct((batch * seq, D_POOL), BF16),
        scratch_shapes=[pltpu.VMEM((N_META + tp, D_POOL), F32)],
        compiler_params=_params(2),
        name="pool",
    )(u, u, um, pw, ps)


def _attn_kernel(q_ref, k_ref, ka_ref, vt_ref, km_ref, kam_ref, vtm_ref, o_ref, sa_ref, sb_ref, *, th, tk):
    qi = pl.program_id(2)
    nt = (((1,), (1,)), ((), ()))
    lane = lax.broadcasted_iota(jnp.int32, (th, HEAD_DIM), 1)
    ones = jnp.where(lane < BIAS_TERMS, 1.0, 0.0).astype(BF16)
    qa = [jnp.concatenate([q_ref[h * th:(h + 1) * th, :], ones], axis=1) for h in range(2)]
    vrows = vt_ref.shape[0]

    def scores(keys, half):
        return lax.dot_general(keys, qa[half], nt, preferred_element_type=F32)

    def x_keys(j):
        ks = pl.multiple_of(j * tk, tk)
        return jnp.concatenate([k_ref[pl.ds(ks, tk), :], ka_ref[pl.ds(ks, tk), :]], axis=1)

    def update(read_s, vt, carry, masked=False):
        m, acc = carry
        if masked:
            key = lax.broadcasted_iota(jnp.int32, (tk, th), 0)
            qry = lax.broadcasted_iota(jnp.int32, (tk, th), 1)
            read = lambda: jnp.where(key <= qry, read_s(), -jnp.inf)
        else:
            read = read_s
        m_new = jnp.maximum(m, jnp.max(read(), axis=0, keepdims=True))
        a = jnp.exp2(m - m_new)
        p = jnp.exp2(read() - m_new).astype(BF16)
        return m_new, a * acc + jnp.dot(vt, p, preferred_element_type=F32)

    def x_vt(j):
        return vt_ref[:, pl.ds(pl.multiple_of(j * tk, tk), tk)]

    keys_m = jnp.concatenate([km_ref[...], kam_ref[...]], axis=1)
    carry = []
    for h in range(2):
        s_m = scores(keys_m, h)
        init = (jnp.full((1, th), -jnp.inf, F32), jnp.zeros((vrows, th), F32))
        carry.append(update(lambda: s_m, vtm_ref[...], init))

    def fill(buf, j):
        keys = x_keys(j)
        for h in range(2):
            buf[h] = scores(keys, h)

    def drain(buf, j, carry):
        vt = x_vt(j)
        return tuple(update(lambda h=h: buf[h], vt, carry[h]) for h in range(2))

    def body(jj, carry):
        j = 2 * jj
        fill(sb_ref, j + 1)
        carry = drain(sa_ref, j, carry)
        fill(sa_ref, j + 2)
        return drain(sb_ref, j + 1, carry)

    first = 2 * qi
    fill(sa_ref, 0)
    c0, c1 = lax.fori_loop(0, qi, body, tuple(carry))
    s11 = scores(x_keys(first + 1), 1)
    c0 = update(lambda: sa_ref[0], x_vt(first), c0, masked=True)
    c1 = update(lambda: sa_ref[1], x_vt(first), c1)
    c1 = update(lambda: s11, x_vt(first + 1), c1, masked=True)
    for h, (m, acc) in enumerate((c0, c1)):
        out_t = acc[:HEAD_DIM] / acc[HEAD_DIM:HEAD_DIM + 1]
        o_ref[h * th:(h + 1) * th, :] = out_t.T.astype(BF16)


def _attention(qk, kaug, vt, qk_m, kaug_m, vt_m, *, batch, seq, th, tk):
    assert th == tk
    tq = 2 * th
    nq = seq // tq
    vrows = vt.shape[0] // (batch * FOX_HEADS)
    kern = functools.partial(_attn_kernel, th=th, tk=tk)
    return pl.pallas_call(
        kern,
        grid=(batch, FOX_HEADS, nq),
        in_specs=[
            pl.BlockSpec((tq, HEAD_DIM), lambda b, h, i: (b * nq + i, h)),
            pl.BlockSpec((seq, HEAD_DIM), lambda b, h, i: (b, FOX_HEADS + h)),
            pl.BlockSpec((seq, HEAD_DIM), lambda b, h, i: (b, h)),
            pl.BlockSpec((vrows, seq), lambda b, h, i: (b * FOX_HEADS + h, 0)),
            pl.BlockSpec((META_PAD, HEAD_DIM), lambda b, h, i: (0, FOX_HEADS + h)),
            pl.BlockSpec((META_PAD, HEAD_DIM), lambda b, h, i: (0, h)),
            pl.BlockSpec((vrows, META_PAD), lambda b, h, i: (h, 0)),
        ],
        out_specs=pl.BlockSpec((tq, HEAD_DIM), lambda b, h, i: (b * nq + i, h)),
        out_shape=jax.ShapeDtypeStruct((batch * seq, D_FOX), BF16),
        scratch_shapes=[pltpu.VMEM((2, tk, th), F32), pltpu.VMEM((2, tk, th), F32)],
        compiler_params=_params(3),
        name="attn",
    )(qk, qk, kaug, vt, qk_m, kaug_m, vt_m)


def _bias_terms(neg_bias):
    terms, rest = [], neg_bias
    for _ in range(BIAS_TERMS):
        t = rest.astype(BF16)
        terms.append(t)
        rest = rest - t.astype(F32)
    return jnp.stack(terms)


def _mix_kernel(x_ref, of_ref, op_ref, wo_ref, gi_ref, bi_ref, gm_ref, bm_ref, wr_ref, br_ref, tri_ref,
                h_ref, hp_ref, ri_ref, rg_ref, cnt_ref, carry_ref, *, chunk):
    i = pl.program_id(0)
    tm = x_ref.shape[0]
    d_model = x_ref.shape[1]
    h0 = _layer_norm(x_ref[...], gi_ref[...], bi_ref[...])
    of = of_ref[...]
    op = op_ref[...]
    for c in range(0, d_model, chunk):
        mix = jnp.dot(of, wo_ref[0:D_FOX, c:c + chunk], preferred_element_type=F32)
        mix = mix + jnp.dot(op, wo_ref[D_FOX:, c:c + chunk], preferred_element_type=F32)
        h_ref[:, c:c + chunk] = DEEPNORM_ALPHA * h0[:, c:c + chunk] + mix
    h1 = _layer_norm(h_ref[...], gm_ref[...], bm_ref[...])
    h_ref[...] = h1
    _store_slabs(hp_ref, _pack_halves(h1))

    h_hi = h1.astype(BF16)
    h_mid = (h1 - h_hi.astype(F32)).astype(BF16)
    hh = jnp.dot(h_hi, wr_ref[...], preferred_element_type=F32)
    mh = jnp.dot(h_mid, wr_ref[:, :LANES], preferred_element_type=F32)
    logits = hh[:, :LANES] + hh[:, LANES:] + mh + br_ref[...]
    lane = lax.broadcasted_iota(jnp.int32, logits.shape, 1)
    neg = -jnp.inf
    gl = jnp.where(lane < N_GROUPS, logits, neg)
    gmax = jnp.max(gl, axis=-1, keepdims=True)
    g_idx = jnp.min(jnp.where(gl == gmax, lane, LANES), axis=-1, keepdims=True)
    p_g = 1.0 / jnp.sum(jnp.exp(gl - gmax), axis=-1, keepdims=True)
    lo = N_GROUPS + g_idx * EXPERTS_PER_GROUP
    el = jnp.where((lane >= lo) & (lane < lo + EXPERTS_PER_GROUP), logits, neg)
    v1 = jnp.max(el, axis=-1, keepdims=True)
    i1 = jnp.min(jnp.where(el == v1, lane, LANES), axis=-1, keepdims=True)
    el2 = jnp.where(lane == i1, neg, el)
    v2 = jnp.max(el2, axis=-1, keepdims=True)
    i2 = jnp.min(jnp.where(el2 == v2, lane, LANES), axis=-1, keepdims=True)
    t = jnp.exp(v2 - v1)
    gate1 = p_g / (1.0 + t)
    gate2 = gate1 * t
    e1 = i1 - N_GROUPS
    e2 = i2 - N_GROUPS

    @pl.when(i == 0)
    def _():
        carry_ref[...] = jnp.zeros_like(carry_ref)

    lower = tri_ref[...]
    oh1 = (lane == e1).astype(F32)
    oh2 = (lane == e2).astype(F32)
    pre1 = jnp.dot(lower, oh1.astype(BF16), preferred_element_type=F32)
    pre2 = jnp.dot(lower, oh2.astype(BF16), preferred_element_type=F32)
    cnt1 = jnp.sum(oh1, axis=0, keepdims=True)
    cnt2 = jnp.sum(oh2, axis=0, keepdims=True)
    base = carry_ref[...]
    rank1 = jnp.sum((pre1 + base) * oh1, axis=-1, keepdims=True)
    rank2 = jnp.sum((pre2 + base + cnt1) * oh2, axis=-1, keepdims=True)
    total = base + cnt1 + cnt2
    carry_ref[...] = total
    cnt_ref[...] = total

    ri_ref[...] = jnp.where(lane == 0, e1, jnp.where(lane == 1, e2, jnp.where(
        lane == 2, rank1.astype(jnp.int32), jnp.where(lane == 3, rank2.astype(jnp.int32), 0))))
    rg_ref[...] = jnp.where(lane == 0, gate1, jnp.where(lane == 1, gate2, 0.0))


def _mix(x2, o_fox, o_pool, wo, gi, bi, gm, bm, wr, br, *, tm):
    rows, d_model = x2.shape
    idx = jnp.arange(tm, dtype=jnp.int32)
    tri = (idx[None, :] < idx[:, None]).astype(BF16)
    kern = functools.partial(_mix_kernel, chunk=512)
    row_spec = lambda w: pl.BlockSpec((tm, w), lambda i: (i, 0))
    vec_spec = lambda w: pl.BlockSpec((1, w), lambda i: (0, 0))
    return pl.pallas_call(
        kern,
        grid=(rows // tm,),
        in_specs=[
            row_spec(d_model), row_spec(D_FOX), row_spec(D_POOL),
            pl.BlockSpec(wo.shape, lambda i: (0, 0)),
            vec_spec(d_model), vec_spec(d_model), vec_spec(d_model), vec_spec(d_model),
            pl.BlockSpec(wr.shape, lambda i: (0, 0)),
            vec_spec(LANES),
            pl.BlockSpec((tm, tm), lambda i: (0, 0)),
        ],
        out_specs=[row_spec(d_model), pl.BlockSpec((tm * SLAB_ROWS, LANES), lambda i: (i, 0)), row_spec(LANES),
                   row_spec(LANES), vec_spec(LANES)],
        out_shape=[
            jax.ShapeDtypeStruct((rows, d_model), F32),
            jax.ShapeDtypeStruct((rows * SLAB_ROWS, LANES), jnp.uint32),
            jax.ShapeDtypeStruct((rows, LANES), jnp.int32),
            jax.ShapeDtypeStruct((rows, LANES), F32),
            jax.ShapeDtypeStruct((1, LANES), F32),
        ],
        scratch_shapes=[pltpu.VMEM((1, LANES), F32)],
        compiler_params=_params(1),
        name="mix",
    )(x2, o_fox, o_pool, wo, gi, bi, gm, bm, wr, br, tri)


def _moe_kernel(be_ref, na_ref, tok_ref, tokn_ref, dst_ref, h_hbm, w13_ref, w2_ref, y_hbm, xbuf, obuf, gsem, ssem,
                *, chunk):
    i = pl.program_id(0)
    n_act = na_ref[0]
    te = xbuf.shape[1] * SLAB_ROWS
    slot = i % 2

    def row_view(buf, s, r):
        return buf.at[s, r // SLAB_ROWS, :, r % SLAB_ROWS, :]

    def start_gather(idx_ref, s):
        for r in range(te):
            pltpu.make_async_copy(h_hbm.at[idx_ref[0, 0, r]], row_view(xbuf, s, r), gsem.at[s]).start(priority=r % 2)

    def wait_gather(s):
        pltpu.make_async_copy(xbuf.at[1 - s], xbuf.at[s], gsem.at[s]).wait()

    def start_scatter(idx, s):
        for r in range(te):
            pltpu.make_async_copy(row_view(obuf, s, r), y_hbm.at[idx(r)], ssem.at[s]).start(priority=r % 2)

    def wait_scatter(s):
        pltpu.make_async_copy(obuf.at[1 - s], obuf.at[s], ssem.at[s]).wait()

    @pl.when(i < n_act)
    def _():
        @pl.when(i == 0)
        def _():
            start_gather(tok_ref, slot)
            spare = y_hbm.shape[0] - 2 * te
            obuf[...] = jnp.zeros_like(obuf)
            for s in range(2):
                start_scatter(lambda r, s=s: spare + s * te + r, s)
            for s in range(2):
                wait_scatter(s)

        wait_gather(slot)

        @pl.when(i + 1 < n_act)
        def _():
            start_gather(tokn_ref, 1 - slot)

        @pl.when(i >= 2)
        def _():
            wait_scatter(slot)

        x = _unpack_halves(_row_tiles_to_value(xbuf.at[slot])).astype(BF16)
        out = None
        for c in range(0, D_EXPERT, chunk):
            gate = jnp.dot(x, w13_ref[0, :, c:c + chunk].astype(BF16), preferred_element_type=F32)
            up = jnp.dot(x, w13_ref[0, :, D_EXPERT + c:D_EXPERT + c + chunk].astype(BF16),
                         preferred_element_type=F32)
            hid = (gate / (1.0 + jnp.exp(-gate)) * up).astype(BF16)
            part = jnp.dot(hid, w2_ref[0, c:c + chunk, :].astype(BF16), preferred_element_type=F32)
            out = part if out is None else out + part
        _value_to_row_tiles(obuf.at[slot], _pack_halves(out))
        start_scatter(lambda r: dst_ref[0, 0, r], slot)

        @pl.when(i == n_act - 1)
        def _():
            @pl.when(i >= 1)
            def _():
                wait_scatter(1 - slot)
            wait_scatter(slot)


def _moe(blk_e, n_act, rows_tok, rows_dst, h1p, w13, w2, *, te):
    rows = h1p.shape[0] // SLAB_ROWS
    words = SLAB_ROWS * LANES
    d_model = 2 * words
    nb = rows_tok.shape[0]
    kern = functools.partial(_moe_kernel, chunk=256)
    smem_spec = lambda off: pl.BlockSpec((1, 1, te), lambda i, be, na: (jnp.minimum(i + off, nb - 1), 0, 0),
                                         memory_space=pltpu.SMEM)
    tiles = (2, te // SLAB_ROWS, words // LANES, SLAB_ROWS, LANES)
    y = pl.pallas_call(
        kern,
        grid_spec=pltpu.PrefetchScalarGridSpec(
            num_scalar_prefetch=2,
            grid=(nb,),
            in_specs=[
                smem_spec(0), smem_spec(1), smem_spec(0),
                pl.BlockSpec(memory_space=pl.ANY),
                pl.BlockSpec((1, d_model, 2 * D_EXPERT), lambda i, be, na: (be[i], 0, 0)),
                pl.BlockSpec((1, D_EXPERT, d_model), lambda i, be, na: (be[i], 0, 0)),
            ],
            out_specs=pl.BlockSpec(memory_space=pl.ANY),
            scratch_shapes=[
                pltpu.VMEM(tiles, jnp.uint32),
                pltpu.VMEM(tiles, jnp.uint32),
                pltpu.SemaphoreType.DMA((2,)),
                pltpu.SemaphoreType.DMA((2,)),
            ],
        ),
        out_shape=jax.ShapeDtypeStruct((2 * rows + 2 * te, SLAB_ROWS, LANES), jnp.uint32),
        compiler_params=_params(1, vmem=MOE_VMEM_LIMIT),
        name="moe",
    )(blk_e, n_act, rows_tok, rows_tok, rows_dst, h1p.reshape(rows, SLAB_ROWS, LANES), w13, w2)
    return y.reshape(-1, LANES)


def _final_kernel(h_ref, y0_ref, y1_ref, rg_ref, g_ref, b_ref, o_ref):
    rg = rg_ref[...]
    ffn = _unpack_halves(_load_slabs(y0_ref)) * rg[:, 0:1] + _unpack_halves(_load_slabs(y1_ref)) * rg[:, 1:2]
    o_ref[...] = _layer_norm(DEEPNORM_ALPHA * h_ref[...] + ffn, g_ref[...], b_ref[...])


def _final(h1, y, rg, g, b, *, tm):
    rows, d_model = h1.shape
    nt = rows // tm
    return pl.pallas_call(
        _final_kernel,
        grid=(nt,),
        in_specs=[
            pl.BlockSpec((tm, d_model), lambda i: (i, 0)),
            pl.BlockSpec((tm * SLAB_ROWS, LANES), lambda i: (i, 0)),
            pl.BlockSpec((tm * SLAB_ROWS, LANES), lambda i: (nt + i, 0)),
            pl.BlockSpec((tm, LANES), lambda i: (i, 0)),
            pl.BlockSpec((1, d_model), lambda i: (0, 0)),
            pl.BlockSpec((1, d_model), lambda i: (0, 0)),
        ],
        out_specs=pl.BlockSpec((tm, d_model), lambda i: (i, 0)),
        out_shape=jax.ShapeDtypeStruct((rows, d_model), F32),
        compiler_params=_params(1),
        name="final",
    )(h1, y, y, rg, g, b)


def _dispatch_tables(ri, cnt, *, rows, te):
    experts = ri[:, 0:2]
    rank = ri[:, 2:4]
    counts = cnt[0, :N_EXPERTS].astype(jnp.int32)
    padded = ((counts + te - 1) // te) * te
    pend = jnp.cumsum(padded)
    pstart = pend - padded
    eids = jnp.arange(N_EXPERTS, dtype=jnp.int32)
    seg = jnp.sum(jnp.where(experts[..., None] == eids, pstart, 0), axis=-1)
    dest = (seg + rank).reshape(-1)
    nb = (2 * rows + N_EXPERTS * (te - 1) + te - 1) // te
    tok = jnp.repeat(jnp.arange(rows, dtype=jnp.int32), 2)
    out_row = tok + jnp.tile(jnp.array([0, rows], jnp.int32), rows)
    pos = jnp.arange(nb * te, dtype=jnp.int32)
    rows_dst = (2 * rows + pos % (2 * te)).at[dest].set(out_row)
    rows_tok = jnp.where(rows_dst < rows, rows_dst, jnp.where(rows_dst < 2 * rows, rows_dst - rows, 0))
    blk_e = jnp.minimum(jnp.sum(pend[None, :] <= (jnp.arange(nb, dtype=jnp.int32) * te)[:, None], axis=1),
                        N_EXPERTS - 1).astype(jnp.int32)
    n_act = (pend[-1] // te).astype(jnp.int32).reshape(1)
    return blk_e, n_act, rows_tok.reshape(nb, 1, te), rows_dst.reshape(nb, 1, te)


def kernel(x, meta, ln_in_g, ln_in_b, w_in, b_f, pool_w, pool_scale, w_out, ln_mix_g, ln_mix_b,
           w_router_g, b_router_g, w_router_e, b_router_e, w13, w2, ln_ffn_g, ln_ffn_b):
    batch, seq, d_model = x.shape
    rows = batch * seq
    tile = min(512, seq)
    assert seq % (2 * tile) == 0 and tile % N_META == 0
    assert w_in.shape[0] == 1, "depth-1 trunk"

    x2 = x.reshape(rows, d_model)
    row = lambda v: v.reshape(1, -1).astype(F32)

    wi = w_in[0]
    w_f = jnp.pad(wi[:, 3 * D_FOX:3 * D_FOX + FOX_HEADS], ((0, 0), (0, LANES - FOX_HEADS)))
    w_proj = jnp.concatenate([wi[:, :3 * D_FOX], wi[:, 3 * D_FOX + FOX_HEADS:], w_f], axis=1).astype(BF16)
    bf = jnp.pad(b_f[0], (0, LANES - FOX_HEADS)).reshape(1, LANES)
    gi, bi = row(ln_in_g), row(ln_in_b)

    meta_pad = jnp.pad(meta.astype(F32), ((0, META_PAD - N_META), (0, 0)))
    sel = _bias_placement()
    qk_m, vt_m, u_m, c_m, _ = _proj(meta_pad, gi, bi, w_proj, bf, sel, tm=META_PAD, tiles_per_seq=1)
    d_meta = (c_m[:N_META, :FOX_HEADS] - c_m[N_META - 1:N_META, :FOX_HEADS]) * LOG2E
    d_meta = jnp.pad(d_meta, ((0, META_PAD - N_META), (0, 0)), constant_values=MASKED_BIAS)
    terms = _bias_terms(-d_meta)
    kaug_m = jnp.zeros((META_PAD, FOX_HEADS, HEAD_DIM), BF16).at[:, :, :BIAS_TERMS].set(terms.transpose(1, 2, 0))
    kaug_m = kaug_m.reshape(META_PAD, D_FOX)

    qk, vt, u, _, kaug = _proj(x2, gi, bi, w_proj, bf, sel, tm=tile, tiles_per_seq=seq // tile)

    o_pool = _pool(u, u_m[:N_META], pool_w[0].astype(BF16), row(pool_scale[0]), batch=batch, seq=seq, tp=tile)
    o_fox = _attention(qk, kaug, vt, qk_m, kaug_m, vt_m, batch=batch, seq=seq, th=tile, tk=tile)

    w_r = jnp.pad(jnp.concatenate([w_router_g[0], w_router_e[0]], axis=1),
                  ((0, 0), (0, LANES - N_GROUPS - N_EXPERTS)))
    w_r_hi = w_r.astype(BF16)
    w_r = jnp.concatenate([w_r_hi, (w_r - w_r_hi.astype(F32)).astype(BF16)], axis=1)
    b_r = jnp.pad(jnp.concatenate([b_router_g[0], b_router_e[0]]), (0, LANES - N_GROUPS - N_EXPERTS)).reshape(1, LANES)
    h1, h1p, ri, rg, cnt = _mix(x2, o_fox, o_pool, w_out[0].astype(BF16), gi, bi, row(ln_mix_g[0]),
                                row(ln_mix_b[0]), w_r, b_r, tm=tile)

    te = 256
    blk_e, n_act, rows_tok, rows_dst = _dispatch_tables(ri, cnt, rows=rows, te=te)
    y = _moe(blk_e, n_act, rows_tok, rows_dst, h1p, w13[0], w2[0], te=te)

    out = _final(h1, y, rg, row(ln_ffn_g[0]), row(ln_ffn_b[0]), tm=tile)
    return out.reshape(batch, seq, d_model)
```

```python
import functools

import jax
import jax.numpy as jnp
from jax import lax
from jax.experimental import pallas as pl
from jax.experimental.pallas import tpu as pltpu

F32 = jnp.float32
BF16 = jnp.bfloat16

N_META = 16
FOX_HEADS = 8
HEAD_DIM = 128
D_FOX = FOX_HEADS * HEAD_DIM
POOL_WINDOWS = (2, 4, 8, 16)
POOL_GROUP_DIM = 256
D_POOL = len(POOL_WINDOWS) * POOL_GROUP_DIM
N_GROUPS = 4
EXPERTS_PER_GROUP = 8
N_EXPERTS = N_GROUPS * EXPERTS_PER_GROUP
D_EXPERT = 1024
LN_EPS = 1e-5
DEEPNORM_ALPHA = 2.0 ** 0.25

LANES = 128
SLAB_ROWS = 8
META_PAD = 128
MASKED_BIAS = 1e30
LOG2E = 1.4426950408889634
BIAS_TERMS = 3
VT_ROWS = 144
VMEM_LIMIT = 56 * 1024 * 1024
MOE_VMEM_LIMIT = 62 * 1024 * 1024


def _params(n_axes, vmem=VMEM_LIMIT):
    return pltpu.CompilerParams(dimension_semantics=("arbitrary",) * n_axes, vmem_limit_bytes=vmem)


def _layer_norm(x, g, b):
    mu = jnp.mean(x, axis=-1, keepdims=True)
    xc = x - mu
    var = jnp.mean(xc * xc, axis=-1, keepdims=True)
    return xc * lax.rsqrt(var + LN_EPS) * g + b


def _pack_halves(x):
    n = x.shape[1] // 2
    bits = lambda v: lax.bitcast_convert_type(v.astype(BF16).astype(F32), jnp.uint32)
    return (bits(x[:, :n]) >> 16) | (bits(x[:, n:]) & jnp.uint32(0xFFFF0000))


def _unpack_halves(w):
    lo = lax.bitcast_convert_type(w << 16, F32)
    hi = lax.bitcast_convert_type(w & jnp.uint32(0xFFFF0000), F32)
    return jnp.concatenate([lo, hi], axis=1)


def _store_slabs(ref, words):
    rows = words.shape[0]
    for k in range(SLAB_ROWS):
        ref[pl.ds(k, rows, stride=SLAB_ROWS), :] = words[:, k * LANES:(k + 1) * LANES]


def _load_slabs(ref):
    rows = ref.shape[0] // SLAB_ROWS
    return jnp.concatenate([ref[pl.ds(k, rows, stride=SLAB_ROWS), :] for k in range(SLAB_ROWS)], axis=1)


def _row_tiles_to_value(ref):
    rt, lt = ref.shape[0], ref.shape[1]
    return jnp.concatenate([ref[:, j].reshape(rt * SLAB_ROWS, LANES) for j in range(lt)], axis=1)


def _value_to_row_tiles(ref, value):
    rt, lt = ref.shape[0], ref.shape[1]
    for j in range(lt):
        ref[:, j] = value[:, j * LANES:(j + 1) * LANES].reshape(rt, SLAB_ROWS, LANES)


def _proj_kernel(x_ref, g_ref, b_ref, w_ref, bf_ref, sel_ref, qk_ref, vt_ref, u_ref, d_ref, ka_ref, carry_ref,
                 *, tiles_per_seq, chunk):
    i = pl.program_id(0)
    tm = x_ref.shape[0]
    xn = _layer_norm(x_ref[...], g_ref[...], b_ref[...]).astype(BF16)
    scale = HEAD_DIM ** -0.5 * LOG2E
    for c in range(0, 2 * D_FOX, chunk):
        acc = jnp.dot(xn, w_ref[:, c:c + chunk], preferred_element_type=F32)
        if c < D_FOX:
            acc = acc * scale
        qk_ref[:, c:c + chunk] = acc.astype(BF16)
    sub = lax.broadcasted_iota(jnp.int32, (VT_ROWS - HEAD_DIM, tm), 0)
    tail = jnp.where(sub == 0, 1.0, 0.0).astype(BF16)
    for c in range(0, D_FOX, chunk):
        acc = jnp.dot(xn, w_ref[:, 2 * D_FOX + c:2 * D_FOX + c + chunk], preferred_element_type=F32)
        for hd in range(chunk // HEAD_DIM):
            r0 = (c // HEAD_DIM + hd) * VT_ROWS
            vt_ref[r0:r0 + HEAD_DIM, :] = acc[:, hd * HEAD_DIM:(hd + 1) * HEAD_DIM].T.astype(BF16)
            vt_ref[r0 + HEAD_DIM:r0 + VT_ROWS, :] = tail
    for c in range(0, D_POOL, chunk):
        u_ref[:, c:c + chunk] = jnp.dot(xn, w_ref[:, 3 * D_FOX + c:3 * D_FOX + c + chunk],
                                        preferred_element_type=F32)
    fl = jnp.dot(xn, w_ref[:, 3 * D_FOX + D_POOL:], preferred_element_type=F32) + bf_ref[...]
    lf = jnp.minimum(fl, 0.0) - jnp.log1p(jnp.exp(-jnp.abs(fl)))

    @pl.when(i % tiles_per_seq == 0)
    def _():
        carry_ref[...] = jnp.zeros_like(carry_ref)

    row = lax.broadcasted_iota(jnp.int32, lf.shape, 0)
    acc = lf
    k = 1
    while k < tm:
        acc = acc + jnp.where(row >= k, pltpu.roll(acc, k, 0), 0.0)
        k *= 2
    d = acc + carry_ref[...]
    d_ref[...] = d
    carry_ref[...] = d[tm - 1:tm, :]

    nd = d * (-LOG2E)
    hi = nd.astype(BF16)
    r1 = nd - hi.astype(F32)
    mid = r1.astype(BF16)
    lo = (r1 - mid.astype(F32)).astype(BF16)
    split = jnp.concatenate([hi, mid, lo], axis=1)
    ka_ref[...] = jnp.dot(split, sel_ref[...], preferred_element_type=F32).astype(BF16)


def _bias_placement():
    t, h = jnp.meshgrid(jnp.arange(BIAS_TERMS), jnp.arange(FOX_HEADS), indexing="ij")
    sel = jnp.zeros((BIAS_TERMS * LANES, D_FOX), F32)
    return sel.at[(t * LANES + h).ravel(), (h * HEAD_DIM + t).ravel()].set(1.0).astype(BF16)


def _proj(x2, g, b, w, bf, sel, *, tm, tiles_per_seq):
    rows, d_model = x2.shape
    n_proj = w.shape[1]
    kern = functools.partial(_proj_kernel, tiles_per_seq=tiles_per_seq, chunk=512)
    return pl.pallas_call(
        kern,
        grid=(rows // tm,),
        in_specs=[
            pl.BlockSpec((tm, d_model), lambda i: (i, 0)),
            pl.BlockSpec((1, d_model), lambda i: (0, 0)),
            pl.BlockSpec((1, d_model), lambda i: (0, 0)),
            pl.BlockSpec((d_model, n_proj), lambda i: (0, 0)),
            pl.BlockSpec((1, LANES), lambda i: (0, 0)),
            pl.BlockSpec(sel.shape, lambda i: (0, 0)),
        ],
        out_specs=[
            pl.BlockSpec((tm, 2 * D_FOX), lambda i: (i, 0)),
            pl.BlockSpec((FOX_HEADS * VT_ROWS, tm), lambda i: (i // tiles_per_seq, i % tiles_per_seq)),
            pl.BlockSpec((tm, D_POOL), lambda i: (i, 0)),
            pl.BlockSpec((tm, LANES), lambda i: (i, 0)),
            pl.BlockSpec((tm, D_FOX), lambda i: (i, 0)),
        ],
        out_shape=[
            jax.ShapeDtypeStruct((rows, 2 * D_FOX), BF16),
            jax.ShapeDtypeStruct((rows // (tm * tiles_per_seq) * FOX_HEADS * VT_ROWS, tm * tiles_per_seq), BF16),
            jax.ShapeDtypeStruct((rows, D_POOL), F32),
            jax.ShapeDtypeStruct((rows, LANES), F32),
            jax.ShapeDtypeStruct((rows, D_FOX), BF16),
        ],
        scratch_shapes=[pltpu.VMEM((1, LANES), F32)],
        compiler_params=_params(1),
        name="proj",
    )(x2, g, b, w, bf, sel)


def _pool_kernel(u_ref, halo_ref, um_ref, pw_ref, ps_ref, o_ref, ext_ref):
    i = pl.program_id(1)
    tp = u_ref.shape[0]
    halo = jnp.where(i == 0, um_ref[...], halo_ref[...])
    ext_ref[0:N_META, :] = halo
    ext_ref[N_META:, :] = u_ref[...]
    for g, w in enumerate(POOL_WINDOWS):
        cols = slice(g * POOL_GROUP_DIM, (g + 1) * POOL_GROUP_DIM)
        tok = ext_ref[N_META:N_META + tp, cols]
        acc = tok
        for j in range(1, w):
            acc = acc + ext_ref[N_META - j:N_META - j + tp, cols]
        pooled = acc * (1.0 / w) - tok
        mixed = jnp.dot(pooled.astype(BF16), pw_ref[g], preferred_element_type=F32)
        o_ref[:, cols] = (mixed * ps_ref[:, cols]).astype(BF16)


def _pool(u, um, pw, ps, *, batch, seq, tp):
    tiles = seq // tp
    halo_blocks = tp // N_META
    return pl.pallas_call(
        _pool_kernel,
        grid=(batch, tiles),
        in_specs=[
            pl.BlockSpec((tp, D_POOL), lambda b, i: (b * tiles + i, 0)),
            pl.BlockSpec((N_META, D_POOL), lambda b, i: (jnp.maximum((b * tiles + i) * halo_blocks - 1, 0), 0)),
            pl.BlockSpec((N_META, D_POOL), lambda b, i: (0, 0)),
            pl.BlockSpec(pw.shape, lambda b, i: (0, 0, 0)),
            pl.BlockSpec((1, D_POOL), lambda b, i: (0, 0)),
        ],
        out_specs=pl.BlockSpec((tp, D_POOL), lambda b, i: (b * tiles + i, 0)),
        out_shape=jax.ShapeDtypeStruct((batch * seq, D_POOL), BF16),
        scratch_shapes=[pltpu.VMEM((N_META + tp, D_POOL), F32)],
        compiler_params=_params(2),
        name="pool",
    )(u, u, um, pw, ps)


def _attn_kernel(q_ref, k_ref, ka_ref, vt_ref, km_ref, kam_ref, vtm_ref, o_ref, sa_ref, sb_ref, *, th, tk):
    qi = pl.program_id(2)
    nt = (((1,), (1,)), ((), ()))
    lane = lax.broadcasted_iota(jnp.int32, (th, HEAD_DIM), 1)
    ones = jnp.where(lane < BIAS_TERMS, 1.0, 0.0).astype(BF16)
    qa = [jnp.concatenate([q_ref[h * th:(h + 1) * th, :], ones], axis=1) for h in range(2)]
    vrows = vt_ref.shape[0]

    def scores(keys, half):
        return lax.dot_general(keys, qa[half], nt, preferred_element_type=F32)

    def x_keys(j):
        ks = pl.multiple_of(j * tk, tk)
        return jnp.concatenate([k_ref[pl.ds(ks, tk), :], ka_ref[pl.ds(ks, tk), :]], axis=1)

    def update(read_s, vt, carry, shift=None):
        m, acc = carry
        if shift is not None:
            key = lax.broadcasted_iota(jnp.int32, (tk, th), 0)
            qry = lax.broadcasted_iota(jnp.int32, (tk, th), 1)
            read = lambda: jnp.where(key <= qry + shift, read_s(), -jnp.inf)
        else:
            read = read_s
        m_new = jnp.maximum(m, jnp.max(read(), axis=0, keepdims=True))
        a = jnp.exp2(m - m_new)
        p = jnp.exp2(read() - m_new).astype(BF16)
        return m_new, a * acc + jnp.dot(vt, p, preferred_element_type=F32)

    def x_vt(j):
        return vt_ref[:, pl.ds(pl.multiple_of(j * tk, tk), tk)]

    keys_m = jnp.concatenate([km_ref[...], kam_ref[...]], axis=1)
    carry = []
    for h in range(2):
        s_m = scores(keys_m, h)
        init = (jnp.full((1, th), -jnp.inf, F32), jnp.zeros((vrows, th), F32))
        carry.append(update(lambda: s_m, vtm_ref[...], init))

    def fill(buf, j):
        keys = x_keys(j)
        for h in range(2):
            buf[h] = scores(keys, h)

    def drain(buf, j, carry):
        vt = x_vt(j)
        return tuple(update(lambda h=h: buf[h], vt, carry[h]) for h in range(2))

    def body(jj, carry):
        j = 2 * jj
        fill(sb_ref, j + 1)
        carry = drain(sa_ref, j, carry)
        fill(sa_ref, j + 2)
        return drain(sb_ref, j + 1, carry)

    inner = 2 * th // tk
    first = qi * inner
    fill(sa_ref, 0)
    carry = list(lax.fori_loop(0, first // 2, body, tuple(carry)))
    for t in range(inner):
        keys, vt = (None if t == 0 else x_keys(first + t)), x_vt(first + t)
        for h in range(2):
            shift = h * th - t * tk
            if shift + th - 1 < 0:
                continue
            read_s = (lambda h=h: sa_ref[h]) if t == 0 else (lambda s=scores(keys, h): s)
            carry[h] = update(read_s, vt, carry[h], None if shift >= tk - 1 else shift)
    for h, (m, acc) in enumerate(carry):
        out_t = acc[:HEAD_DIM] / acc[HEAD_DIM:HEAD_DIM + 1]
        o_ref[h * th:(h + 1) * th, :] = out_t.T.astype(BF16)


def _attention(qk, kaug, vt, qk_m, kaug_m, vt_m, *, batch, seq, th, tk):
    assert th % tk == 0
    tq = 2 * th
    nq = seq // tq
    vrows = vt.shape[0] // (batch * FOX_HEADS)
    kern = functools.partial(_attn_kernel, th=th, tk=tk)
    return pl.pallas_call(
        kern,
        grid=(batch, FOX_HEADS, nq),
        in_specs=[
            pl.BlockSpec((tq, HEAD_DIM), lambda b, h, i: (b * nq + i, h)),
            pl.BlockSpec((seq, HEAD_DIM), lambda b, h, i: (b, FOX_HEADS + h)),
            pl.BlockSpec((seq, HEAD_DIM), lambda b, h, i: (b, h)),
            pl.BlockSpec((vrows, seq), lambda b, h, i: (b * FOX_HEADS + h, 0)),
            pl.BlockSpec((META_PAD, HEAD_DIM), lambda b, h, i: (0, FOX_HEADS + h)),
            pl.BlockSpec((META_PAD, HEAD_DIM), lambda b, h, i: (0, h)),
            pl.BlockSpec((vrows, META_PAD), lambda b, h, i: (h, 0)),
        ],
        out_specs=pl.BlockSpec((tq, HEAD_DIM), lambda b, h, i: (b * nq + i, h)),
        out_shape=jax.ShapeDtypeStruct((batch * seq, D_FOX), BF16),
        scratch_shapes=[pltpu.VMEM((2, tk, th), F32), pltpu.VMEM((2, tk, th), F32)],
        compiler_params=_params(3),
        name="attn",
    )(qk, qk, kaug, vt, qk_m, kaug_m, vt_m)


def _bias_terms(neg_bias):
    terms, rest = [], neg_bias
    for _ in range(BIAS_TERMS):
        t = rest.astype(BF16)
        terms.append(t)
        rest = rest - t.astype(F32)
    return jnp.stack(terms)


def _mix_kernel(x_ref, of_ref, op_ref, wo_ref, gi_ref, bi_ref, gm_ref, bm_ref, wr_ref, br_ref, tri_ref,
                h_ref, hp_ref, ri_ref, rg_ref, cnt_ref, carry_ref, *, chunk):
    i = pl.program_id(0)
    tm = x_ref.shape[0]
    d_model = x_ref.shape[1]
    h0 = _layer_norm(x_ref[...], gi_ref[...], bi_ref[...])
    of = of_ref[...]
    op = op_ref[...]
    for c in range(0, d_model, chunk):
        mix = jnp.dot(of, wo_ref[0:D_FOX, c:c + chunk], preferred_element_type=F32)
        mix = mix + jnp.dot(op, wo_ref[D_FOX:, c:c + chunk], preferred_element_type=F32)
        h_ref[:, c:c + chunk] = DEEPNORM_ALPHA * h0[:, c:c + chunk] + mix
    h1 = _layer_norm(h_ref[...], gm_ref[...], bm_ref[...])
    h_ref[...] = h1
    _store_slabs(hp_ref, _pack_halves(h1))

    h_hi = h1.astype(BF16)
    h_mid = (h1 - h_hi.astype(F32)).astype(BF16)
    hh = jnp.dot(h_hi, wr_ref[...], preferred_element_type=F32)
    mh = jnp.dot(h_mid, wr_ref[:, :LANES], preferred_element_type=F32)
    logits = hh[:, :LANES] + hh[:, LANES:] + mh + br_ref[...]
    lane = lax.broadcasted_iota(jnp.int32, logits.shape, 1)
    neg = -jnp.inf
    gl = jnp.where(lane < N_GROUPS, logits, neg)
    gmax = jnp.max(gl, axis=-1, keepdims=True)
    g_idx = jnp.min(jnp.where(gl == gmax, lane, LANES), axis=-1, keepdims=True)
    p_g = 1.0 / jnp.sum(jnp.exp(gl - gmax), axis=-1, keepdims=True)
    lo = N_GROUPS + g_idx * EXPERTS_PER_GROUP
    el = jnp.where((lane >= lo) & (lane < lo + EXPERTS_PER_GROUP), logits, neg)
    v1 = jnp.max(el, axis=-1, keepdims=True)
    i1 = jnp.min(jnp.where(el == v1, lane, LANES), axis=-1, keepdims=True)
    el2 = jnp.where(lane == i1, neg, el)
    v2 = jnp.max(el2, axis=-1, keepdims=True)
    i2 = jnp.min(jnp.where(el2 == v2, lane, LANES), axis=-1, keepdims=True)
    t = jnp.exp(v2 - v1)
    gate1 = p_g / (1.0 + t)
    gate2 = gate1 * t
    e1 = i1 - N_GROUPS
    e2 = i2 - N_GROUPS

    @pl.when(i == 0)
    def _():
        carry_ref[...] = jnp.zeros_like(carry_ref)

    lower = tri_ref[...]
    oh1 = (lane == e1).astype(F32)
    oh2 = (lane == e2).astype(F32)
    pre1 = jnp.dot(lower, oh1.astype(BF16), preferred_element_type=F32)
    pre2 = jnp.dot(lower, oh2.astype(BF16), preferred_element_type=F32)
    cnt1 = jnp.sum(oh1, axis=0, keepdims=True)
    cnt2 = jnp.sum(oh2, axis=0, keepdims=True)
    base = carry_ref[...]
    rank1 = jnp.sum((pre1 + base) * oh1, axis=-1, keepdims=True)
    rank2 = jnp.sum((pre2 + base + cnt1) * oh2, axis=-1, keepdims=True)
    total = base + cnt1 + cnt2
    carry_ref[...] = total
    cnt_ref[...] = total

    ri_ref[...] = jnp.where(lane == 0, e1, jnp.where(lane == 1, e2, jnp.where(
        lane == 2, rank1.astype(jnp.int32), jnp.where(lane == 3, rank2.astype(jnp.int32), 0))))
    rg_ref[...] = jnp.where(lane == 0, gate1, jnp.where(lane == 1, gate2, 0.0))


def _mix(x2, o_fox, o_pool, wo, gi, bi, gm, bm, wr, br, *, tm):
    rows, d_model = x2.shape
    idx = jnp.arange(tm, dtype=jnp.int32)
    tri = (idx[None, :] < idx[:, None]).astype(BF16)
    kern = functools.partial(_mix_kernel, chunk=512)
    row_spec = lambda w: pl.BlockSpec((tm, w), lambda i: (i, 0))
    vec_spec = lambda w: pl.BlockSpec((1, w), lambda i: (0, 0))
    return pl.pallas_call(
        kern,
        grid=(rows // tm,),
        in_specs=[
            row_spec(d_model), row_spec(D_FOX), row_spec(D_POOL),
            pl.BlockSpec(wo.shape, lambda i: (0, 0)),
            vec_spec(d_model), vec_spec(d_model), vec_spec(d_model), vec_spec(d_model),
            pl.BlockSpec(wr.shape, lambda i: (0, 0)),
            vec_spec(LANES),
            pl.BlockSpec((tm, tm), lambda i: (0, 0)),
        ],
        out_specs=[row_spec(d_model), pl.BlockSpec((tm * SLAB_ROWS, LANES), lambda i: (i, 0)), row_spec(LANES),
                   row_spec(LANES), vec_spec(LANES)],
        out_shape=[
            jax.ShapeDtypeStruct((rows, d_model), F32),
            jax.ShapeDtypeStruct((rows * SLAB_ROWS, LANES), jnp.uint32),
            jax.ShapeDtypeStruct((rows, LANES), jnp.int32),
            jax.ShapeDtypeStruct((rows, LANES), F32),
            jax.ShapeDtypeStruct((1, LANES), F32),
        ],
        scratch_shapes=[pltpu.VMEM((1, LANES), F32)],
        compiler_params=_params(1),
        name="mix",
    )(x2, o_fox, o_pool, wo, gi, bi, gm, bm, wr, br, tri)


def _moe_kernel(be_ref, na_ref, ws_ref, nx_ref, tok_ref, tokn_ref, dst_ref, h_hbm, w13_hbm, w2_hbm, y_hbm,
                xbuf, obuf, wa, wb, gsem, ssem, wsem, *, chunk):
    i = pl.program_id(0)
    n_act = na_ref[0]
    te = xbuf.shape[1] * SLAB_ROWS
    slot = i % 2
    wslot = ws_ref[i]
    new_expert = (i == 0) | (be_ref[i] != be_ref[jnp.maximum(i - 1, 0)])

    def row_view(buf, s, r):
        return buf.at[s, r // SLAB_ROWS, :, r % SLAB_ROWS, :]

    def start_gather(idx_ref, s):
        for r in range(te):
            pltpu.make_async_copy(h_hbm.at[idx_ref[0, 0, r]], row_view(xbuf, s, r), gsem.at[s]).start(priority=r % 2)

    def wait_gather(s):
        pltpu.make_async_copy(xbuf.at[1 - s], xbuf.at[s], gsem.at[s]).wait()

    def start_scatter(idx, s):
        for r in range(te):
            pltpu.make_async_copy(row_view(obuf, s, r), y_hbm.at[idx(r)], ssem.at[s]).start(priority=r % 2)

    def wait_scatter(s):
        pltpu.make_async_copy(obuf.at[1 - s], obuf.at[s], ssem.at[s]).wait()

    def weight_copies(e, s):
        return (pltpu.make_async_copy(w13_hbm.at[e], wa.at[s], wsem.at[s]),
                pltpu.make_async_copy(w2_hbm.at[e], wb.at[s], wsem.at[s]))

    @pl.when(i < n_act)
    def _():
        @pl.when(i == 0)
        def _():
            for cp in weight_copies(be_ref[0], wslot):
                cp.start()
            start_gather(tok_ref, slot)
            spare = y_hbm.shape[0] - 2 * te
            obuf[...] = jnp.zeros_like(obuf)
            for s in range(2):
                start_scatter(lambda r, s=s: spare + s * te + r, s)
            for s in range(2):
                wait_scatter(s)

        @pl.when(new_expert)
        def _():
            for cp in weight_copies(be_ref[i], wslot):
                cp.wait()

            @pl.when(nx_ref[i] >= 0)
            def _():
                for cp in weight_copies(nx_ref[i], 1 - wslot):
                    cp.start()

        wait_gather(slot)

        @pl.when(i + 1 < n_act)
        def _():
            start_gather(tokn_ref, 1 - slot)

        @pl.when(i >= 2)
        def _():
            wait_scatter(slot)

        x = _unpack_halves(_row_tiles_to_value(xbuf.at[slot])).astype(BF16)
        out = None
        for c in range(0, D_EXPERT, chunk):
            gate = jnp.dot(x, wa[wslot, :, c:c + chunk].astype(BF16), preferred_element_type=F32)
            up = jnp.dot(x, wa[wslot, :, D_EXPERT + c:D_EXPERT + c + chunk].astype(BF16),
                         preferred_element_type=F32)
            hid = (gate / (1.0 + jnp.exp(-gate)) * up).astype(BF16)
            part = jnp.dot(hid, wb[wslot, c:c + chunk, :].astype(BF16), preferred_element_type=F32)
            out = part if out is None else out + part
        _value_to_row_tiles(obuf.at[slot], _pack_halves(out))
        start_scatter(lambda r: dst_ref[0, 0, r], slot)

        @pl.when(i == n_act - 1)
        def _():
            @pl.when(i >= 1)
            def _():
                wait_scatter(1 - slot)
            wait_scatter(slot)


def _moe(blk_e, n_act, w_slot, next_e, rows_tok, rows_dst, h1p, w13, w2, *, te):
    rows = h1p.shape[0] // SLAB_ROWS
    words = SLAB_ROWS * LANES
    d_model = 2 * words
    nb = rows_tok.shape[0]
    kern = functools.partial(_moe_kernel, chunk=256)
    smem_spec = lambda off: pl.BlockSpec((1, 1, te), lambda i, *_: (jnp.minimum(i + off, nb - 1), 0, 0),
                                         memory_space=pltpu.SMEM)
    tiles = (2, te // SLAB_ROWS, words // LANES, SLAB_ROWS, LANES)
    y = pl.pallas_call(
        kern,
        grid_spec=pltpu.PrefetchScalarGridSpec(
            num_scalar_prefetch=4,
            grid=(nb,),
            in_specs=[
                smem_spec(0), smem_spec(1), smem_spec(0),
                pl.BlockSpec(memory_space=pl.ANY),
                pl.BlockSpec(memory_space=pl.ANY),
                pl.BlockSpec(memory_space=pl.ANY),
            ],
            out_specs=pl.BlockSpec(memory_space=pl.ANY),
            scratch_shapes=[
                pltpu.VMEM(tiles, jnp.uint32),
                pltpu.VMEM(tiles, jnp.uint32),
                pltpu.VMEM((2, d_model, 2 * D_EXPERT), F32),
                pltpu.VMEM((2, D_EXPERT, d_model), F32),
                pltpu.SemaphoreType.DMA((2,)),
                pltpu.SemaphoreType.DMA((2,)),
                pltpu.SemaphoreType.DMA((2,)),
            ],
        ),
        out_shape=jax.ShapeDtypeStruct((2 * rows + 2 * te, SLAB_ROWS, LANES), jnp.uint32),
        compiler_params=_params(1, vmem=MOE_VMEM_LIMIT),
        name="moe",
    )(blk_e, n_act, w_slot, next_e, rows_tok, rows_tok, rows_dst, h1p.reshape(rows, SLAB_ROWS, LANES), w13, w2)
    return y.reshape(-1, LANES)


def _final_kernel(h_ref, y0_ref, y1_ref, rg_ref, g_ref, b_ref, o_ref):
    rg = rg_ref[...]
    ffn = _unpack_halves(_load_slabs(y0_ref)) * rg[:, 0:1] + _unpack_halves(_load_slabs(y1_ref)) * rg[:, 1:2]
    o_ref[...] = _layer_norm(DEEPNORM_ALPHA * h_ref[...] + ffn, g_ref[...], b_ref[...])


def _final(h1, y, rg, g, b, *, tm):
    rows, d_model = h1.shape
    nt = rows // tm
    return pl.pallas_call(
        _final_kernel,
        grid=(nt,),
        in_specs=[
            pl.BlockSpec((tm, d_model), lambda i: (i, 0)),
            pl.BlockSpec((tm * SLAB_ROWS, LANES), lambda i: (i, 0)),
            pl.BlockSpec((tm * SLAB_ROWS, LANES), lambda i: (nt + i, 0)),
            pl.BlockSpec((tm, LANES), lambda i: (i, 0)),
            pl.BlockSpec((1, d_model), lambda i: (0, 0)),
            pl.BlockSpec((1, d_model), lambda i: (0, 0)),
        ],
        out_specs=pl.BlockSpec((tm, d_model), lambda i: (i, 0)),
        out_shape=jax.ShapeDtypeStruct((rows, d_model), F32),
        compiler_params=_params(1),
        name="final",
    )(h1, y, y, rg, g, b)


def _dispatch_tables(ri, cnt, *, rows, te):
    experts = ri[:, 0:2]
    rank = ri[:, 2:4]
    counts = cnt[0, :N_EXPERTS].astype(jnp.int32)
    padded = ((counts + te - 1) // te) * te
    pend = jnp.cumsum(padded)
    pstart = pend - padded
    eids = jnp.arange(N_EXPERTS, dtype=jnp.int32)
    seg = jnp.sum(jnp.where(experts[..., None] == eids, pstart, 0), axis=-1)
    dest = (seg + rank).reshape(-1)
    nb = (2 * rows + N_EXPERTS * (te - 1) + te - 1) // te
    tok = jnp.repeat(jnp.arange(rows, dtype=jnp.int32), 2)
    out_row = tok + jnp.tile(jnp.array([0, rows], jnp.int32), rows)
    pos = jnp.arange(nb * te, dtype=jnp.int32)
    rows_dst = (2 * rows + pos % (2 * te)).at[dest].set(out_row)
    rows_tok = jnp.where(rows_dst < rows, rows_dst, jnp.where(rows_dst < 2 * rows, rows_dst - rows, 0))
    blk_e = jnp.minimum(jnp.sum(pend[None, :] <= (jnp.arange(nb, dtype=jnp.int32) * te)[:, None], axis=1),
                        N_EXPERTS - 1).astype(jnp.int32)
    n_act = (pend[-1] // te).astype(jnp.int32).reshape(1)
    has = counts > 0
    slot_of = (jnp.cumsum(has) - 1) % 2
    cand = jnp.where(has, eids, N_EXPERTS)
    later = lax.cummin(cand, reverse=True)
    next_of = jnp.concatenate([later[1:], jnp.array([N_EXPERTS], jnp.int32)])
    next_of = jnp.where(next_of < N_EXPERTS, next_of, -1)
    w_slot = slot_of[blk_e].astype(jnp.int32)
    next_e = next_of[blk_e].astype(jnp.int32)
    return blk_e, n_act, w_slot, next_e, rows_tok.reshape(nb, 1, te), rows_dst.reshape(nb, 1, te)


def kernel(x, meta, ln_in_g, ln_in_b, w_in, b_f, pool_w, pool_scale, w_out, ln_mix_g, ln_mix_b,
           w_router_g, b_router_g, w_router_e, b_router_e, w13, w2, ln_ffn_g, ln_ffn_b):
    batch, seq, d_model = x.shape
    rows = batch * seq
    tile = min(512, seq)
    assert seq % (2 * tile) == 0 and tile % N_META == 0
    assert w_in.shape[0] == 1, "depth-1 trunk"

    x2 = x.reshape(rows, d_model)
    row = lambda v: v.reshape(1, -1).astype(F32)

    wi = w_in[0]
    w_f = jnp.pad(wi[:, 3 * D_FOX:3 * D_FOX + FOX_HEADS], ((0, 0), (0, LANES - FOX_HEADS)))
    w_proj = jnp.concatenate([wi[:, :3 * D_FOX], wi[:, 3 * D_FOX + FOX_HEADS:], w_f], axis=1).astype(BF16)
    bf = jnp.pad(b_f[0], (0, LANES - FOX_HEADS)).reshape(1, LANES)
    gi, bi = row(ln_in_g), row(ln_in_b)

    meta_pad = jnp.pad(meta.astype(F32), ((0, META_PAD - N_META), (0, 0)))
    sel = _bias_placement()
    qk_m, vt_m, u_m, c_m, _ = _proj(meta_pad, gi, bi, w_proj, bf, sel, tm=META_PAD, tiles_per_seq=1)
    d_meta = (c_m[:N_META, :FOX_HEADS] - c_m[N_META - 1:N_META, :FOX_HEADS]) * LOG2E
    d_meta = jnp.pad(d_meta, ((0, META_PAD - N_META), (0, 0)), constant_values=MASKED_BIAS)
    terms = _bias_terms(-d_meta)
    kaug_m = jnp.zeros((META_PAD, FOX_HEADS, HEAD_DIM), BF16).at[:, :, :BIAS_TERMS].set(terms.transpose(1, 2, 0))
    kaug_m = kaug_m.reshape(META_PAD, D_FOX)

    qk, vt, u, _, kaug = _proj(x2, gi, bi, w_proj, bf, sel, tm=tile, tiles_per_seq=seq // tile)

    o_pool = _pool(u, u_m[:N_META], pool_w[0].astype(BF16), row(pool_scale[0]), batch=batch, seq=seq, tp=tile)
    o_fox = _attention(qk, kaug, vt, qk_m, kaug_m, vt_m, batch=batch, seq=seq, th=tile, tk=tile)

    w_r = jnp.pad(jnp.concatenate([w_router_g[0], w_router_e[0]], axis=1),
                  ((0, 0), (0, LANES - N_GROUPS - N_EXPERTS)))
    w_r_hi = w_r.astype(BF16)
    w_r = jnp.concatenate([w_r_hi, (w_r - w_r_hi.astype(F32)).astype(BF16)], axis=1)
    b_r = jnp.pad(jnp.concatenate([b_router_g[0], b_router_e[0]]), (0, LANES - N_GROUPS - N_EXPERTS)).reshape(1, LANES)
    h1, h1p, ri, rg, cnt = _mix(x2, o_fox, o_pool, w_out[0].astype(BF16), gi, bi, row(ln_mix_g[0]),
                                row(ln_mix_b[0]), w_r, b_r, tm=tile)

    te = 256
    blk_e, n_act, w_slot, next_e, rows_tok, rows_dst = _dispatch_tables(ri, cnt, rows=rows, te=te)
    y = _moe(blk_e, n_act, w_slot, next_e, rows_tok, rows_dst, h1p, w13[0], w2[0], te=te)

    out = _final(h1, y, rg, row(ln_ffn_g[0]), row(ln_ffn_b[0]), tm=tile)
    return out.reshape(batch, seq, d_model)
```

```python
import functools

import jax
import jax.numpy as jnp
from jax import lax
from jax.experimental import pallas as pl
from jax.experimental.pallas import tpu as pltpu

F32 = jnp.float32
BF16 = jnp.bfloat16

N_META = 16
FOX_HEADS = 8
HEAD_DIM = 128
D_FOX = FOX_HEADS * HEAD_DIM
POOL_WINDOWS = (2, 4, 8, 16)
POOL_GROUP_DIM = 256
D_POOL = len(POOL_WINDOWS) * POOL_GROUP_DIM
N_GROUPS = 4
EXPERTS_PER_GROUP = 8
N_EXPERTS = N_GROUPS * EXPERTS_PER_GROUP
D_EXPERT = 1024
LN_EPS = 1e-5
DEEPNORM_ALPHA = 2.0 ** 0.25

LANES = 128
SLAB_ROWS = 8
META_PAD = 128
MASKED_BIAS = 1e30
LOG2E = 1.4426950408889634
BIAS_TERMS = 3
VT_ROWS = 144
VMEM_LIMIT = 56 * 1024 * 1024
MOE_VMEM_LIMIT = 62 * 1024 * 1024


def _params(n_axes, vmem=VMEM_LIMIT):
    return pltpu.CompilerParams(dimension_semantics=("arbitrary",) * n_axes, vmem_limit_bytes=vmem)


def _layer_norm(x, g, b):
    mu = jnp.mean(x, axis=-1, keepdims=True)
    xc = x - mu
    var = jnp.mean(xc * xc, axis=-1, keepdims=True)
    return xc * lax.rsqrt(var + LN_EPS) * g + b


def _pack_halves(x):
    n = x.shape[1] // 2
    bits = lambda v: lax.bitcast_convert_type(v.astype(BF16).astype(F32), jnp.uint32)
    return (bits(x[:, :n]) >> 16) | (bits(x[:, n:]) & jnp.uint32(0xFFFF0000))


def _unpack_halves(w):
    lo = lax.bitcast_convert_type(w << 16, F32)
    hi = lax.bitcast_convert_type(w & jnp.uint32(0xFFFF0000), F32)
    return jnp.concatenate([lo, hi], axis=1)


def _store_slabs(ref, words):
    rows = words.shape[0]
    for k in range(SLAB_ROWS):
        ref[pl.ds(k, rows, stride=SLAB_ROWS), :] = words[:, k * LANES:(k + 1) * LANES]


def _load_slabs(ref):
    rows = ref.shape[0] // SLAB_ROWS
    return jnp.concatenate([ref[pl.ds(k, rows, stride=SLAB_ROWS), :] for k in range(SLAB_ROWS)], axis=1)


def _row_tiles_to_value(ref):
    rt, lt = ref.shape[0], ref.shape[1]
    return jnp.concatenate([ref[:, j].reshape(rt * SLAB_ROWS, LANES) for j in range(lt)], axis=1)


def _value_to_row_tiles(ref, value):
    rt, lt = ref.shape[0], ref.shape[1]
    for j in range(lt):
        ref[:, j] = value[:, j * LANES:(j + 1) * LANES].reshape(rt, SLAB_ROWS, LANES)


def _proj_kernel(x_ref, g_ref, b_ref, w_ref, bf_ref, sel_ref, um_ref, pw_ref, ps_ref,
                 qk_ref, vt_ref, u_ref, op_ref, d_ref, ka_ref, carry_ref, ext_ref, *, tiles_per_seq, chunk):
    i = pl.program_id(0)
    tm = x_ref.shape[0]
    xn = _layer_norm(x_ref[...], g_ref[...], b_ref[...]).astype(BF16)
    first_tile = i % tiles_per_seq == 0
    ext_ref[0:N_META, :] = jnp.where(first_tile, um_ref[...], ext_ref[0:N_META, :])
    for c in range(0, D_POOL, chunk):
        uc = jnp.dot(xn, w_ref[:, 3 * D_FOX + c:3 * D_FOX + c + chunk], preferred_element_type=F32)
        u_ref[:, c:c + chunk] = uc
        ext_ref[N_META:, c:c + chunk] = uc
    for g, w in enumerate(POOL_WINDOWS):
        cols = slice(g * POOL_GROUP_DIM, (g + 1) * POOL_GROUP_DIM)
        tok = ext_ref[N_META:N_META + tm, cols]
        acc = tok
        for j in range(1, w):
            acc = acc + ext_ref[N_META - j:N_META - j + tm, cols]
        pooled = acc * (1.0 / w) - tok
        mixed = jnp.dot(pooled.astype(BF16), pw_ref[g], preferred_element_type=F32)
        op_ref[:, cols] = (mixed * ps_ref[:, cols]).astype(BF16)
    ext_ref[0:N_META, :] = ext_ref[tm:tm + N_META, :]

    scale = HEAD_DIM ** -0.5 * LOG2E
    for c in range(0, 2 * D_FOX, chunk):
        acc = jnp.dot(xn, w_ref[:, c:c + chunk], preferred_element_type=F32)
        if c < D_FOX:
            acc = acc * scale
        qk_ref[:, c:c + chunk] = acc.astype(BF16)
    sub = lax.broadcasted_iota(jnp.int32, (VT_ROWS - HEAD_DIM, tm), 0)
    tail = jnp.where(sub == 0, 1.0, 0.0).astype(BF16)
    for c in range(0, D_FOX, chunk):
        acc = jnp.dot(xn, w_ref[:, 2 * D_FOX + c:2 * D_FOX + c + chunk], preferred_element_type=F32)
        for hd in range(chunk // HEAD_DIM):
            r0 = (c // HEAD_DIM + hd) * VT_ROWS
            vt_ref[r0:r0 + HEAD_DIM, :] = acc[:, hd * HEAD_DIM:(hd + 1) * HEAD_DIM].T.astype(BF16)
            vt_ref[r0 + HEAD_DIM:r0 + VT_ROWS, :] = tail
    fl = jnp.dot(xn, w_ref[:, 3 * D_FOX + D_POOL:], preferred_element_type=F32) + bf_ref[...]
    lf = jnp.minimum(fl, 0.0) - jnp.log1p(jnp.exp(-jnp.abs(fl)))

    row = lax.broadcasted_iota(jnp.int32, lf.shape, 0)
    acc = lf
    k = 1
    while k < tm:
        acc = acc + jnp.where(row >= k, pltpu.roll(acc, k, 0), 0.0)
        k *= 2
    d = acc + jnp.where(first_tile, 0.0, carry_ref[...])
    d_ref[...] = d
    carry_ref[...] = d[tm - 1:tm, :]

    nd = d * (-LOG2E)
    hi = nd.astype(BF16)
    r1 = nd - hi.astype(F32)
    mid = r1.astype(BF16)
    lo = (r1 - mid.astype(F32)).astype(BF16)
    split = jnp.concatenate([hi, mid, lo], axis=1)
    ka_ref[...] = jnp.dot(split, sel_ref[...], preferred_element_type=F32).astype(BF16)


def _bias_placement():
    t, h = jnp.meshgrid(jnp.arange(BIAS_TERMS), jnp.arange(FOX_HEADS), indexing="ij")
    sel = jnp.zeros((BIAS_TERMS * LANES, D_FOX), F32)
    return sel.at[(t * LANES + h).ravel(), (h * HEAD_DIM + t).ravel()].set(1.0).astype(BF16)


def _proj(x2, g, b, w, bf, sel, um, pw, ps, *, tm, tiles_per_seq):
    rows, d_model = x2.shape
    n_proj = w.shape[1]
    kern = functools.partial(_proj_kernel, tiles_per_seq=tiles_per_seq, chunk=512)
    return pl.pallas_call(
        kern,
        grid=(rows // tm,),
        in_specs=[
            pl.BlockSpec((tm, d_model), lambda i: (i, 0)),
            pl.BlockSpec((1, d_model), lambda i: (0, 0)),
            pl.BlockSpec((1, d_model), lambda i: (0, 0)),
            pl.BlockSpec((d_model, n_proj), lambda i: (0, 0)),
            pl.BlockSpec((1, LANES), lambda i: (0, 0)),
            pl.BlockSpec(sel.shape, lambda i: (0, 0)),
            pl.BlockSpec((N_META, D_POOL), lambda i: (0, 0)),
            pl.BlockSpec(pw.shape, lambda i: (0, 0, 0)),
            pl.BlockSpec((1, D_POOL), lambda i: (0, 0)),
        ],
        out_specs=[
            pl.BlockSpec((tm, 2 * D_FOX), lambda i: (i, 0)),
            pl.BlockSpec((FOX_HEADS * VT_ROWS, tm), lambda i: (i // tiles_per_seq, i % tiles_per_seq)),
            pl.BlockSpec((tm, D_POOL), lambda i: (i, 0)),
            pl.BlockSpec((tm, D_POOL), lambda i: (i, 0)),
            pl.BlockSpec((tm, LANES), lambda i: (i, 0)),
            pl.BlockSpec((tm, D_FOX), lambda i: (i, 0)),
        ],
        out_shape=[
            jax.ShapeDtypeStruct((rows, 2 * D_FOX), BF16),
            jax.ShapeDtypeStruct((rows // (tm * tiles_per_seq) * FOX_HEADS * VT_ROWS, tm * tiles_per_seq), BF16),
            jax.ShapeDtypeStruct((rows, D_POOL), F32),
            jax.ShapeDtypeStruct((rows, D_POOL), BF16),
            jax.ShapeDtypeStruct((rows, LANES), F32),
            jax.ShapeDtypeStruct((rows, D_FOX), BF16),
        ],
        scratch_shapes=[pltpu.VMEM((1, LANES), F32), pltpu.VMEM((N_META + tm, D_POOL), F32)],
        compiler_params=_params(1),
        name="proj",
    )(x2, g, b, w, bf, sel, um, pw, ps)


def _attn_kernel(q_ref, k_ref, ka_ref, vt_ref, km_ref, kam_ref, vtm_ref, o_ref, sa_ref, sb_ref, *, th, tk):
    qi = pl.program_id(2)
    nt = (((1,), (1,)), ((), ()))
    lane = lax.broadcasted_iota(jnp.int32, (th, HEAD_DIM), 1)
    ones = jnp.where(lane < BIAS_TERMS, 1.0, 0.0).astype(BF16)
    qa = [jnp.concatenate([q_ref[h * th:(h + 1) * th, :], ones], axis=1) for h in range(2)]
    vrows = vt_ref.shape[0]

    def scores(keys, half):
        return lax.dot_general(keys, qa[half], nt, preferred_element_type=F32)

    def x_keys(j):
        ks = pl.multiple_of(j * tk, tk)
        return jnp.concatenate([k_ref[pl.ds(ks, tk), :], ka_ref[pl.ds(ks, tk), :]], axis=1)

    def update(read_s, vt, carry, masked=False):
        m, acc = carry
        if masked:
            key = lax.broadcasted_iota(jnp.int32, (tk, th), 0)
            qry = lax.broadcasted_iota(jnp.int32, (tk, th), 1)
            read = lambda: jnp.where(key <= qry, read_s(), -jnp.inf)
        else:
            read = read_s
        m_new = jnp.maximum(m, jnp.max(read(), axis=0, keepdims=True))
        a = jnp.exp2(m - m_new)
        p = jnp.exp2(read() - m_new).astype(BF16)
        return m_new, a * acc + jnp.dot(vt, p, preferred_element_type=F32)

    def x_vt(j):
        return vt_ref[:, pl.ds(pl.multiple_of(j * tk, tk), tk)]

    keys_m = jnp.concatenate([km_ref[...], kam_ref[...]], axis=1)
    carry = []
    for h in range(2):
        s_m = scores(keys_m, h)
        init = (jnp.full((1, th), -jnp.inf, F32), jnp.zeros((vrows, th), F32))
        carry.append(update(lambda: s_m, vtm_ref[...], init))

    def fill(buf, j):
        keys = x_keys(j)
        for h in range(2):
            buf[h] = scores(keys, h)

    def drain(buf, j, carry):
        vt = x_vt(j)
        return tuple(update(lambda h=h: buf[h], vt, carry[h]) for h in range(2))

    def body(jj, carry):
        j = 2 * jj
        fill(sb_ref, j + 1)
        carry = drain(sa_ref, j, carry)
        fill(sa_ref, j + 2)
        return drain(sb_ref, j + 1, carry)

    first = 2 * qi
    fill(sa_ref, 0)
    c0, c1 = lax.fori_loop(0, qi, body, tuple(carry))
    s11 = scores(x_keys(first + 1), 1)
    c0 = update(lambda: sa_ref[0], x_vt(first), c0, masked=True)
    c1 = update(lambda: sa_ref[1], x_vt(first), c1)
    c1 = update(lambda: s11, x_vt(first + 1), c1, masked=True)
    for h, (m, acc) in enumerate((c0, c1)):
        out_t = acc[:HEAD_DIM] / acc[HEAD_DIM:HEAD_DIM + 1]
        o_ref[h * th:(h + 1) * th, :] = out_t.T.astype(BF16)


def _attention(qk, kaug, vt, qk_m, kaug_m, vt_m, *, batch, seq, th, tk):
    assert th == tk
    tq = 2 * th
    nq = seq // tq
    vrows = vt.shape[0] // (batch * FOX_HEADS)
    kern = functools.partial(_attn_kernel, th=th, tk=tk)
    return pl.pallas_call(
        kern,
        grid=(batch, FOX_HEADS, nq),
        in_specs=[
            pl.BlockSpec((tq, HEAD_DIM), lambda b, h, i: (b * nq + i, h)),
            pl.BlockSpec((seq, HEAD_DIM), lambda b, h, i: (b, FOX_HEADS + h)),
            pl.BlockSpec((seq, HEAD_DIM), lambda b, h, i: (b, h)),
            pl.BlockSpec((vrows, seq), lambda b, h, i: (b * FOX_HEADS + h, 0)),
            pl.BlockSpec((META_PAD, HEAD_DIM), lambda b, h, i: (0, FOX_HEADS + h)),
            pl.BlockSpec((META_PAD, HEAD_DIM), lambda b, h, i: (0, h)),
            pl.BlockSpec((vrows, META_PAD), lambda b, h, i: (h, 0)),
        ],
        out_specs=pl.BlockSpec((tq, HEAD_DIM), lambda b, h, i: (b * nq + i, h)),
        out_shape=jax.ShapeDtypeStruct((batch * seq, D_FOX), BF16),
        scratch_shapes=[pltpu.VMEM((2, tk, th), F32), pltpu.VMEM((2, tk, th), F32)],
        compiler_params=_params(3),
        name="attn",
    )(qk, qk, kaug, vt, qk_m, kaug_m, vt_m)


def _bias_terms(neg_bias):
    terms, rest = [], neg_bias
    for _ in range(BIAS_TERMS):
        t = rest.astype(BF16)
        terms.append(t)
        rest = rest - t.astype(F32)
    return jnp.stack(terms)


def _mix_kernel(x_ref, of_ref, op_ref, wo_ref, gi_ref, bi_ref, gm_ref, bm_ref, wr_ref, br_ref, tri_ref,
                h_ref, hp_ref, ri_ref, rg_ref, cnt_ref, carry_ref, *, chunk):
    i = pl.program_id(0)
    tm = x_ref.shape[0]
    d_model = x_ref.shape[1]
    h0 = _layer_norm(x_ref[...], gi_ref[...], bi_ref[...])
    of = of_ref[...]
    op = op_ref[...]
    for c in range(0, d_model, chunk):
        mix = jnp.dot(of, wo_ref[0:D_FOX, c:c + chunk], preferred_element_type=F32)
        mix = mix + jnp.dot(op, wo_ref[D_FOX:, c:c + chunk], preferred_element_type=F32)
        h_ref[:, c:c + chunk] = DEEPNORM_ALPHA * h0[:, c:c + chunk] + mix
    h1 = _layer_norm(h_ref[...], gm_ref[...], bm_ref[...])
    h_ref[...] = h1
    _store_slabs(hp_ref, _pack_halves(h1))

    h_hi = h1.astype(BF16)
    h_mid = (h1 - h_hi.astype(F32)).astype(BF16)
    hh = jnp.dot(h_hi, wr_ref[...], preferred_element_type=F32)
    mh = jnp.dot(h_mid, wr_ref[:, :LANES], preferred_element_type=F32)
    logits = hh[:, :LANES] + hh[:, LANES:] + mh + br_ref[...]
    lane = lax.broadcasted_iota(jnp.int32, logits.shape, 1)
    neg = -jnp.inf
    gl = jnp.where(lane < N_GROUPS, logits, neg)
    gmax = jnp.max(gl, axis=-1, keepdims=True)
    g_idx = jnp.min(jnp.where(gl == gmax, lane, LANES), axis=-1, keepdims=True)
    p_g = 1.0 / jnp.sum(jnp.exp(gl - gmax), axis=-1, keepdims=True)
    lo = N_GROUPS + g_idx * EXPERTS_PER_GROUP
    el = jnp.where((lane >= lo) & (lane < lo + EXPERTS_PER_GROUP), logits, neg)
    v1 = jnp.max(el, axis=-1, keepdims=True)
    i1 = jnp.min(jnp.where(el == v1, lane, LANES), axis=-1, keepdims=True)
    el2 = jnp.where(lane == i1, neg, el)
    v2 = jnp.max(el2, axis=-1, keepdims=True)
    i2 = jnp.min(jnp.where(el2 == v2, lane, LANES), axis=-1, keepdims=True)
    t = jnp.exp(v2 - v1)
    gate1 = p_g / (1.0 + t)
    gate2 = gate1 * t
    e1 = i1 - N_GROUPS
    e2 = i2 - N_GROUPS

    @pl.when(i == 0)
    def _():
        carry_ref[...] = jnp.zeros_like(carry_ref)

    lower = tri_ref[...]
    oh1 = (lane == e1).astype(F32)
    oh2 = (lane == e2).astype(F32)
    pre1 = jnp.dot(lower, oh1.astype(BF16), preferred_element_type=F32)
    pre2 = jnp.dot(lower, oh2.astype(BF16), preferred_element_type=F32)
    cnt1 = jnp.sum(oh1, axis=0, keepdims=True)
    cnt2 = jnp.sum(oh2, axis=0, keepdims=True)
    base = carry_ref[...]
    rank1 = jnp.sum((pre1 + base) * oh1, axis=-1, keepdims=True)
    rank2 = jnp.sum((pre2 + base + cnt1) * oh2, axis=-1, keepdims=True)
    total = base + cnt1 + cnt2
    carry_ref[...] = total
    cnt_ref[...] = total

    ri_ref[...] = jnp.where(lane == 0, e1, jnp.where(lane == 1, e2, jnp.where(
        lane == 2, rank1.astype(jnp.int32), jnp.where(lane == 3, rank2.astype(jnp.int32), 0))))
    rg_ref[...] = jnp.where(lane == 0, gate1, jnp.where(lane == 1, gate2, 0.0))


def _mix(x2, o_fox, o_pool, wo, gi, bi, gm, bm, wr, br, *, tm):
    rows, d_model = x2.shape
    idx = jnp.arange(tm, dtype=jnp.int32)
    tri = (idx[None, :] < idx[:, None]).astype(BF16)
    kern = functools.partial(_mix_kernel, chunk=512)
    row_spec = lambda w: pl.BlockSpec((tm, w), lambda i: (i, 0))
    vec_spec = lambda w: pl.BlockSpec((1, w), lambda i: (0, 0))
    return pl.pallas_call(
        kern,
        grid=(rows // tm,),
        in_specs=[
            row_spec(d_model), row_spec(D_FOX), row_spec(D_POOL),
            pl.BlockSpec(wo.shape, lambda i: (0, 0)),
            vec_spec(d_model), vec_spec(d_model), vec_spec(d_model), vec_spec(d_model),
            pl.BlockSpec(wr.shape, lambda i: (0, 0)),
            vec_spec(LANES),
            pl.BlockSpec((tm, tm), lambda i: (0, 0)),
        ],
        out_specs=[row_spec(d_model), pl.BlockSpec((tm * SLAB_ROWS, LANES), lambda i: (i, 0)), row_spec(LANES),
                   row_spec(LANES), vec_spec(LANES)],
        out_shape=[
            jax.ShapeDtypeStruct((rows, d_model), F32),
            jax.ShapeDtypeStruct((rows * SLAB_ROWS, LANES), jnp.uint32),
            jax.ShapeDtypeStruct((rows, LANES), jnp.int32),
            jax.ShapeDtypeStruct((rows, LANES), F32),
            jax.ShapeDtypeStruct((1, LANES), F32),
        ],
        scratch_shapes=[pltpu.VMEM((1, LANES), F32)],
        compiler_params=_params(1),
        name="mix",
    )(x2, o_fox, o_pool, wo, gi, bi, gm, bm, wr, br, tri)


def _moe_kernel(be_ref, na_ref, tok_ref, tokn_ref, dst_ref, h_hbm, w13_ref, w2_ref, y_hbm, xbuf, obuf, gsem, ssem,
                *, chunk):
    i = pl.program_id(0)
    n_act = na_ref[0]
    te = xbuf.shape[1] * SLAB_ROWS
    slot = i % 2

    def row_view(buf, s, r):
        return buf.at[s, r // SLAB_ROWS, :, r % SLAB_ROWS, :]

    def start_gather(idx_ref, s):
        for r in range(te):
            pltpu.make_async_copy(h_hbm.at[idx_ref[0, 0, r]], row_view(xbuf, s, r), gsem.at[s]).start(priority=r % 2)

    def wait_gather(s):
        pltpu.make_async_copy(xbuf.at[1 - s], xbuf.at[s], gsem.at[s]).wait()

    def start_scatter(idx, s):
        for r in range(te):
            pltpu.make_async_copy(row_view(obuf, s, r), y_hbm.at[idx(r)], ssem.at[s]).start(priority=r % 2)

    def wait_scatter(s):
        pltpu.make_async_copy(obuf.at[1 - s], obuf.at[s], ssem.at[s]).wait()

    @pl.when(i < n_act)
    def _():
        @pl.when(i == 0)
        def _():
            start_gather(tok_ref, slot)
            spare = y_hbm.shape[0] - 2 * te
            obuf[...] = jnp.zeros_like(obuf)
            for s in range(2):
                start_scatter(lambda r, s=s: spare + s * te + r, s)
            for s in range(2):
                wait_scatter(s)

        wait_gather(slot)

        @pl.when(i + 1 < n_act)
        def _():
            start_gather(tokn_ref, 1 - slot)

        @pl.when(i >= 2)
        def _():
            wait_scatter(slot)

        x = _unpack_halves(_row_tiles_to_value(xbuf.at[slot])).astype(BF16)
        out = None
        for c in range(0, D_EXPERT, chunk):
            gate = jnp.dot(x, w13_ref[0, :, c:c + chunk].astype(BF16), preferred_element_type=F32)
            up = jnp.dot(x, w13_ref[0, :, D_EXPERT + c:D_EXPERT + c + chunk].astype(BF16),
                         preferred_element_type=F32)
            hid = (gate / (1.0 + jnp.exp(-gate)) * up).astype(BF16)
            part = jnp.dot(hid, w2_ref[0, c:c + chunk, :].astype(BF16), preferred_element_type=F32)
            out = part if out is None else out + part
        _value_to_row_tiles(obuf.at[slot], _pack_halves(out))
        start_scatter(lambda r: dst_ref[0, 0, r], slot)

        @pl.when(i == n_act - 1)
        def _():
            @pl.when(i >= 1)
            def _():
                wait_scatter(1 - slot)
            wait_scatter(slot)


def _moe(blk_e, n_act, rows_tok, rows_dst, h1p, w13, w2, *, te):
    rows = h1p.shape[0] // SLAB_ROWS
    words = SLAB_ROWS * LANES
    d_model = 2 * words
    nb = rows_tok.shape[0]
    kern = functools.partial(_moe_kernel, chunk=256)
    smem_spec = lambda off: pl.BlockSpec((1, 1, te), lambda i, be, na: (jnp.minimum(i + off, nb - 1), 0, 0),
                                         memory_space=pltpu.SMEM)
    tiles = (2, te // SLAB_ROWS, words // LANES, SLAB_ROWS, LANES)
    y = pl.pallas_call(
        kern,
        grid_spec=pltpu.PrefetchScalarGridSpec(
            num_scalar_prefetch=2,
            grid=(nb,),
            in_specs=[
                smem_spec(0), smem_spec(1), smem_spec(0),
                pl.BlockSpec(memory_space=pl.ANY),
                pl.BlockSpec((1, d_model, 2 * D_EXPERT), lambda i, be, na: (be[i], 0, 0)),
                pl.BlockSpec((1, D_EXPERT, d_model), lambda i, be, na: (be[i], 0, 0)),
            ],
            out_specs=pl.BlockSpec(memory_space=pl.ANY),
            scratch_shapes=[
                pltpu.VMEM(tiles, jnp.uint32),
                pltpu.VMEM(tiles, jnp.uint32),
                pltpu.SemaphoreType.DMA((2,)),
                pltpu.SemaphoreType.DMA((2,)),
            ],
        ),
        out_shape=jax.ShapeDtypeStruct((2 * rows + 2 * te, SLAB_ROWS, LANES), jnp.uint32),
        compiler_params=_params(1, vmem=MOE_VMEM_LIMIT),
        name="moe",
    )(blk_e, n_act, rows_tok, rows_tok, rows_dst, h1p.reshape(rows, SLAB_ROWS, LANES), w13, w2)
    return y.reshape(-1, LANES)


def _final_kernel(h_ref, y0_ref, y1_ref, rg_ref, g_ref, b_ref, o_ref):
    rg = rg_ref[...]
    ffn = _unpack_halves(_load_slabs(y0_ref)) * rg[:, 0:1] + _unpack_halves(_load_slabs(y1_ref)) * rg[:, 1:2]
    o_ref[...] = _layer_norm(DEEPNORM_ALPHA * h_ref[...] + ffn, g_ref[...], b_ref[...])


def _final(h1, y, rg, g, b, *, tm):
    rows, d_model = h1.shape
    nt = rows // tm
    return pl.pallas_call(
        _final_kernel,
        grid=(nt,),
        in_specs=[
            pl.BlockSpec((tm, d_model), lambda i: (i, 0)),
            pl.BlockSpec((tm * SLAB_ROWS, LANES), lambda i: (i, 0)),
            pl.BlockSpec((tm * SLAB_ROWS, LANES), lambda i: (nt + i, 0)),
            pl.BlockSpec((tm, LANES), lambda i: (i, 0)),
            pl.BlockSpec((1, d_model), lambda i: (0, 0)),
            pl.BlockSpec((1, d_model), lambda i: (0, 0)),
        ],
        out_specs=pl.BlockSpec((tm, d_model), lambda i: (i, 0)),
        out_shape=jax.ShapeDtypeStruct((rows, d_model), F32),
        compiler_params=_params(1),
        name="final",
    )(h1, y, y, rg, g, b)


def _dispatch_tables(ri, cnt, *, rows, te):
    experts = ri[:, 0:2]
    rank = ri[:, 2:4]
    counts = cnt[0, :N_EXPERTS].astype(jnp.int32)
    padded = ((counts + te - 1) // te) * te
    pend = jnp.cumsum(padded)
    pstart = pend - padded
    eids = jnp.arange(N_EXPERTS, dtype=jnp.int32)
    seg = jnp.sum(jnp.where(experts[..., None] == eids, pstart, 0), axis=-1)
    dest = (seg + rank).reshape(-1)
    nb = (2 * rows + N_EXPERTS * (te - 1) + te - 1) // te
    tok = jnp.repeat(jnp.arange(rows, dtype=jnp.int32), 2)
    out_row = tok + jnp.tile(jnp.array([0, rows], jnp.int32), rows)
    pos = jnp.arange(nb * te, dtype=jnp.int32)
    rows_dst = (2 * rows + pos % (2 * te)).at[dest].set(out_row)
    rows_tok = jnp.where(rows_dst < rows, rows_dst, jnp.where(rows_dst < 2 * rows, rows_dst - rows, 0))
    blk_e = jnp.minimum(jnp.sum(pend[None, :] <= (jnp.arange(nb, dtype=jnp.int32) * te)[:, None], axis=1),
                        N_EXPERTS - 1).astype(jnp.int32)
    n_act = (pend[-1] // te).astype(jnp.int32).reshape(1)
    return blk_e, n_act, rows_tok.reshape(nb, 1, te), rows_dst.reshape(nb, 1, te)


def kernel(x, meta, ln_in_g, ln_in_b, w_in, b_f, pool_w, pool_scale, w_out, ln_mix_g, ln_mix_b,
           w_router_g, b_router_g, w_router_e, b_router_e, w13, w2, ln_ffn_g, ln_ffn_b):
    batch, seq, d_model = x.shape
    rows = batch * seq
    tile = min(512, seq)
    assert seq % (2 * tile) == 0 and tile % N_META == 0
    assert w_in.shape[0] == 1, "depth-1 trunk"

    x2 = x.reshape(rows, d_model)
    row = lambda v: v.reshape(1, -1).astype(F32)

    wi = w_in[0]
    w_f = jnp.pad(wi[:, 3 * D_FOX:3 * D_FOX + FOX_HEADS], ((0, 0), (0, LANES - FOX_HEADS)))
    w_proj = jnp.concatenate([wi[:, :3 * D_FOX], wi[:, 3 * D_FOX + FOX_HEADS:], w_f], axis=1).astype(BF16)
    bf = jnp.pad(b_f[0], (0, LANES - FOX_HEADS)).reshape(1, LANES)
    gi, bi = row(ln_in_g), row(ln_in_b)

    meta_pad = jnp.pad(meta.astype(F32), ((0, META_PAD - N_META), (0, 0)))
    sel = _bias_placement()
    pw, ps = pool_w[0].astype(BF16), row(pool_scale[0])
    no_halo = jnp.zeros((N_META, D_POOL), F32)
    qk_m, vt_m, u_m, _, c_m, _ = _proj(meta_pad, gi, bi, w_proj, bf, sel, no_halo, pw, ps, tm=META_PAD, tiles_per_seq=1)
    d_meta = (c_m[:N_META, :FOX_HEADS] - c_m[N_META - 1:N_META, :FOX_HEADS]) * LOG2E
    d_meta = jnp.pad(d_meta, ((0, META_PAD - N_META), (0, 0)), constant_values=MASKED_BIAS)
    terms = _bias_terms(-d_meta)
    kaug_m = jnp.zeros((META_PAD, FOX_HEADS, HEAD_DIM), BF16).at[:, :, :BIAS_TERMS].set(terms.transpose(1, 2, 0))
    kaug_m = kaug_m.reshape(META_PAD, D_FOX)

    qk, vt, _, o_pool, _, kaug = _proj(x2, gi, bi, w_proj, bf, sel, u_m[:N_META], pw, ps, tm=tile,
                                       tiles_per_seq=seq // tile)
    o_fox = _attention(qk, kaug, vt, qk_m, kaug_m, vt_m, batch=batch, seq=seq, th=tile, tk=tile)

    w_r = jnp.pad(jnp.concatenate([w_router_g[0], w_router_e[0]], axis=1),
                  ((0, 0), (0, LANES - N_GROUPS - N_EXPERTS)))
    w_r_hi = w_r.astype(BF16)
    w_r = jnp.concatenate([w_r_hi, (w_r - w_r_hi.astype(F32)).astype(BF16)], axis=1)
    b_r = jnp.pad(jnp.concatenate([b_router_g[0], b_router_e[0]]), (0, LANES - N_GROUPS - N_EXPERTS)).reshape(1, LANES)
    h1, h1p, ri, rg, cnt = _mix(x2, o_fox, o_pool, w_out[0].astype(BF16), gi, bi, row(ln_mix_g[0]),
                                row(ln_mix_b[0]), w_r, b_r, tm=tile)

    te = 256
    blk_e, n_act, rows_tok, rows_dst = _dispatch_tables(ri, cnt, rows=rows, te=te)
    y = _moe(blk_e, n_act, rows_tok, rows_dst, h1p, w13[0], w2[0], te=te)

    out = _final(h1, y, rg, row(ln_ffn_g[0]), row(ln_ffn_b[0]), tm=tile)
    return out.reshape(batch, seq, d_model)
```

```python
import functools

import jax
import jax.numpy as jnp
from jax import lax
from jax.experimental import pallas as pl
from jax.experimental.pallas import tpu as pltpu

F32 = jnp.float32
BF16 = jnp.bfloat16

N_META = 16
FOX_HEADS = 8
HEAD_DIM = 128
D_FOX = FOX_HEADS * HEAD_DIM
POOL_WINDOWS = (2, 4, 8, 16)
POOL_GROUP_DIM = 256
D_POOL = len(POOL_WINDOWS) * POOL_GROUP_DIM
N_GROUPS = 4
EXPERTS_PER_GROUP = 8
N_EXPERTS = N_GROUPS * EXPERTS_PER_GROUP
D_EXPERT = 1024
LN_EPS = 1e-5
DEEPNORM_ALPHA = 2.0 ** 0.25

LANES = 128
SLAB_ROWS = 8
META_PAD = 128
MASKED_BIAS = 1e30
LOG2E = 1.4426950408889634
BIAS_TERMS = 3
VT_ROWS = 144
VMEM_LIMIT = 56 * 1024 * 1024
MOE_VMEM_LIMIT = 62 * 1024 * 1024


def _params(n_axes, vmem=VMEM_LIMIT):
    return pltpu.CompilerParams(dimension_semantics=("arbitrary",) * n_axes, vmem_limit_bytes=vmem)


def _layer_norm(x, g, b):
    mu = jnp.mean(x, axis=-1, keepdims=True)
    xc = x - mu
    var = jnp.mean(xc * xc, axis=-1, keepdims=True)
    return xc * lax.rsqrt(var + LN_EPS) * g + b


def _pack_halves(x):
    n = x.shape[1] // 2
    bits = lambda v: lax.bitcast_convert_type(v.astype(BF16).astype(F32), jnp.uint32)
    return (bits(x[:, :n]) >> 16) | (bits(x[:, n:]) & jnp.uint32(0xFFFF0000))


def _unpack_halves(w):
    lo = lax.bitcast_convert_type(w << 16, F32)
    hi = lax.bitcast_convert_type(w & jnp.uint32(0xFFFF0000), F32)
    return jnp.concatenate([lo, hi], axis=1)


def _store_slabs(ref, words):
    rows = words.shape[0]
    for k in range(SLAB_ROWS):
        ref[pl.ds(k, rows, stride=SLAB_ROWS), :] = words[:, k * LANES:(k + 1) * LANES]


def _load_slabs(ref):
    rows = ref.shape[0] // SLAB_ROWS
    return jnp.concatenate([ref[pl.ds(k, rows, stride=SLAB_ROWS), :] for k in range(SLAB_ROWS)], axis=1)


def _row_tiles_to_value(ref):
    rt, lt = ref.shape[0], ref.shape[1]
    return jnp.concatenate([ref[:, j].reshape(rt * SLAB_ROWS, LANES) for j in range(lt)], axis=1)


def _value_to_row_tiles(ref, value):
    rt, lt = ref.shape[0], ref.shape[1]
    for j in range(lt):
        ref[:, j] = value[:, j * LANES:(j + 1) * LANES].reshape(rt, SLAB_ROWS, LANES)


def _proj_kernel(x_ref, g_ref, b_ref, w_ref, bf_ref, sel_ref, um_ref, pw_ref, ps_ref,
                 qk_ref, vt_ref, u_ref, op_ref, d_ref, ka_ref, carry_ref, ext_ref, *, tiles_per_seq, chunk):
    i = pl.program_id(0)
    tm = x_ref.shape[0]
    xn = _layer_norm(x_ref[...], g_ref[...], b_ref[...]).astype(BF16)
    first_tile = i % tiles_per_seq == 0
    ext_ref[0:N_META, :] = jnp.where(first_tile, um_ref[...], ext_ref[0:N_META, :])
    for c in range(0, D_POOL, chunk):
        uc = jnp.dot(xn, w_ref[:, 3 * D_FOX + c:3 * D_FOX + c + chunk], preferred_element_type=F32)
        u_ref[:, c:c + chunk] = uc
        ext_ref[N_META:, c:c + chunk] = uc
    for g, w in enumerate(POOL_WINDOWS):
        cols = slice(g * POOL_GROUP_DIM, (g + 1) * POOL_GROUP_DIM)
        tok = ext_ref[N_META:N_META + tm, cols]
        acc = tok
        for j in range(1, w):
            acc = acc + ext_ref[N_META - j:N_META - j + tm, cols]
        pooled = acc * (1.0 / w) - tok
        mixed = jnp.dot(pooled.astype(BF16), pw_ref[g], preferred_element_type=F32)
        op_ref[:, cols] = (mixed * ps_ref[:, cols]).astype(BF16)
    ext_ref[0:N_META, :] = ext_ref[tm:tm + N_META, :]

    scale = HEAD_DIM ** -0.5 * LOG2E
    for c in range(0, 2 * D_FOX, chunk):
        acc = jnp.dot(xn, w_ref[:, c:c + chunk], preferred_element_type=F32)
        if c < D_FOX:
            acc = acc * scale
        qk_ref[:, c:c + chunk] = acc.astype(BF16)
    sub = lax.broadcasted_iota(jnp.int32, (VT_ROWS - HEAD_DIM, tm), 0)
    tail = jnp.where(sub == 0, 1.0, 0.0).astype(BF16)
    for c in range(0, D_FOX, chunk):
        acc = jnp.dot(xn, w_ref[:, 2 * D_FOX + c:2 * D_FOX + c + chunk], preferred_element_type=F32)
        for hd in range(chunk // HEAD_DIM):
            r0 = (c // HEAD_DIM + hd) * VT_ROWS
            vt_ref[r0:r0 + HEAD_DIM, :] = acc[:, hd * HEAD_DIM:(hd + 1) * HEAD_DIM].T.astype(BF16)
            vt_ref[r0 + HEAD_DIM:r0 + VT_ROWS, :] = tail
    fl = jnp.dot(xn, w_ref[:, 3 * D_FOX + D_POOL:], preferred_element_type=F32) + bf_ref[...]
    lf = jnp.minimum(fl, 0.0) - jnp.log1p(jnp.exp(-jnp.abs(fl)))

    row = lax.broadcasted_iota(jnp.int32, lf.shape, 0)
    acc = lf
    k = 1
    while k < tm:
        acc = acc + jnp.where(row >= k, pltpu.roll(acc, k, 0), 0.0)
        k *= 2
    d = acc + jnp.where(first_tile, 0.0, carry_ref[...])
    d_ref[...] = d
    carry_ref[...] = d[tm - 1:tm, :]

    nd = d * (-LOG2E)
    hi = nd.astype(BF16)
    r1 = nd - hi.astype(F32)
    mid = r1.astype(BF16)
    lo = (r1 - mid.astype(F32)).astype(BF16)
    split = jnp.concatenate([hi, mid, lo], axis=1)
    ka_ref[...] = jnp.dot(split, sel_ref[...], preferred_element_type=F32).astype(BF16)


def _bias_placement():
    t, h = jnp.meshgrid(jnp.arange(BIAS_TERMS), jnp.arange(FOX_HEADS), indexing="ij")
    sel = jnp.zeros((BIAS_TERMS * LANES, D_FOX), F32)
    return sel.at[(t * LANES + h).ravel(), (h * HEAD_DIM + t).ravel()].set(1.0).astype(BF16)


def _proj(x2, g, b, w, bf, sel, um, pw, ps, *, tm, tiles_per_seq):
    rows, d_model = x2.shape
    n_proj = w.shape[1]
    kern = functools.partial(_proj_kernel, tiles_per_seq=tiles_per_seq, chunk=512)
    return pl.pallas_call(
        kern,
        grid=(rows // tm,),
        in_specs=[
            pl.BlockSpec((tm, d_model), lambda i: (i, 0)),
            pl.BlockSpec((1, d_model), lambda i: (0, 0)),
            pl.BlockSpec((1, d_model), lambda i: (0, 0)),
            pl.BlockSpec((d_model, n_proj), lambda i: (0, 0)),
            pl.BlockSpec((1, LANES), lambda i: (0, 0)),
            pl.BlockSpec(sel.shape, lambda i: (0, 0)),
            pl.BlockSpec((N_META, D_POOL), lambda i: (0, 0)),
            pl.BlockSpec(pw.shape, lambda i: (0, 0, 0)),
            pl.BlockSpec((1, D_POOL), lambda i: (0, 0)),
        ],
        out_specs=[
            pl.BlockSpec((tm, 2 * D_FOX), lambda i: (i, 0)),
            pl.BlockSpec((FOX_HEADS * VT_ROWS, tm), lambda i: (i // tiles_per_seq, i % tiles_per_seq)),
            pl.BlockSpec((tm, D_POOL), lambda i: (i, 0)),
            pl.BlockSpec((tm, D_POOL), lambda i: (i, 0)),
            pl.BlockSpec((tm, LANES), lambda i: (i, 0)),
            pl.BlockSpec((tm, D_FOX), lambda i: (i, 0)),
        ],
        out_shape=[
            jax.ShapeDtypeStruct((rows, 2 * D_FOX), BF16),
            jax.ShapeDtypeStruct((rows // (tm * tiles_per_seq) * FOX_HEADS * VT_ROWS, tm * tiles_per_seq), BF16),
            jax.ShapeDtypeStruct((rows, D_POOL), F32),
            jax.ShapeDtypeStruct((rows, D_POOL), BF16),
            jax.ShapeDtypeStruct((rows, LANES), F32),
            jax.ShapeDtypeStruct((rows, D_FOX), BF16),
        ],
        scratch_shapes=[pltpu.VMEM((1, LANES), F32), pltpu.VMEM((N_META + tm, D_POOL), F32)],
        compiler_params=_params(1),
        name="proj",
    )(x2, g, b, w, bf, sel, um, pw, ps)


def _attn_kernel(q_ref, k_ref, ka_ref, vt_ref, km_ref, kam_ref, vtm_ref, o_ref, sa_ref, sb_ref, *, th, tk):
    qi = pl.program_id(2)
    nt = (((1,), (1,)), ((), ()))
    lane = lax.broadcasted_iota(jnp.int32, (th, HEAD_DIM), 1)
    ones = jnp.where(lane < BIAS_TERMS, 1.0, 0.0).astype(BF16)
    qa = [jnp.concatenate([q_ref[h * th:(h + 1) * th, :], ones], axis=1) for h in range(2)]
    vrows = vt_ref.shape[0]

    def scores(keys, half):
        return lax.dot_general(keys, qa[half], nt, preferred_element_type=F32)

    def x_keys(j):
        ks = pl.multiple_of(j * tk, tk)
        return jnp.concatenate([k_ref[pl.ds(ks, tk), :], ka_ref[pl.ds(ks, tk), :]], axis=1)

    def update(read_s, vt, carry, masked=False):
        m, acc = carry
        if masked:
            key = lax.broadcasted_iota(jnp.int32, (tk, th), 0)
            qry = lax.broadcasted_iota(jnp.int32, (tk, th), 1)
            read = lambda: jnp.where(key <= qry, read_s(), -jnp.inf)
        else:
            read = read_s
        m_new = jnp.maximum(m, jnp.max(read(), axis=0, keepdims=True))
        a = jnp.exp2(m - m_new)
        p = jnp.exp2(read() - m_new).astype(BF16)
        return m_new, a * acc + jnp.dot(vt, p, preferred_element_type=F32)

    def x_vt(j):
        return vt_ref[:, pl.ds(pl.multiple_of(j * tk, tk), tk)]

    keys_m = jnp.concatenate([km_ref[...], kam_ref[...]], axis=1)
    carry = []
    for h in range(2):
        s_m = scores(keys_m, h)
        init = (jnp.full((1, th), -jnp.inf, F32), jnp.zeros((vrows, th), F32))
        carry.append(update(lambda: s_m, vtm_ref[...], init))

    def fill(buf, j):
        keys = x_keys(j)
        for h in range(2):
            buf[h] = scores(keys, h)

    def drain(buf, j, carry):
        vt = x_vt(j)
        return tuple(update(lambda h=h: buf[h], vt, carry[h]) for h in range(2))

    def body(jj, carry):
        j = 2 * jj
        fill(sb_ref, j + 1)
        carry = drain(sa_ref, j, carry)
        fill(sa_ref, j + 2)
        return drain(sb_ref, j + 1, carry)

    first = 2 * qi
    fill(sa_ref, 0)
    c0, c1 = lax.fori_loop(0, qi, body, tuple(carry))
    s11 = scores(x_keys(first + 1), 1)
    c0 = update(lambda: sa_ref[0], x_vt(first), c0, masked=True)
    c1 = update(lambda: sa_ref[1], x_vt(first), c1)
    c1 = update(lambda: s11, x_vt(first + 1), c1, masked=True)
    for h, (m, acc) in enumerate((c0, c1)):
        out_t = acc[:HEAD_DIM] / acc[HEAD_DIM:HEAD_DIM + 1]
        o_ref[h * th:(h + 1) * th, :] = out_t.T.astype(BF16)


def _attention(qk, kaug, vt, qk_m, kaug_m, vt_m, *, batch, seq, th, tk):
    assert th == tk
    tq = 2 * th
    nq = seq // tq
    vrows = vt.shape[0] // (batch * FOX_HEADS)
    kern = functools.partial(_attn_kernel, th=th, tk=tk)
    return pl.pallas_call(
        kern,
        grid=(batch, FOX_HEADS, nq),
        in_specs=[
            pl.BlockSpec((tq, HEAD_DIM), lambda b, h, i: (b * nq + i, h)),
            pl.BlockSpec((seq, HEAD_DIM), lambda b, h, i: (b, FOX_HEADS + h)),
            pl.BlockSpec((seq, HEAD_DIM), lambda b, h, i: (b, h)),
            pl.BlockSpec((vrows, seq), lambda b, h, i: (b * FOX_HEADS + h, 0)),
            pl.BlockSpec((META_PAD, HEAD_DIM), lambda b, h, i: (0, FOX_HEADS + h)),
            pl.BlockSpec((META_PAD, HEAD_DIM), lambda b, h, i: (0, h)),
            pl.BlockSpec((vrows, META_PAD), lambda b, h, i: (h, 0)),
        ],
        out_specs=pl.BlockSpec((tq, HEAD_DIM), lambda b, h, i: (b * nq + i, h)),
        out_shape=jax.ShapeDtypeStruct((batch * seq, D_FOX), BF16),
        scratch_shapes=[pltpu.VMEM((2, tk, th), F32), pltpu.VMEM((2, tk, th), F32)],
        compiler_params=_params(3),
        name="attn",
    )(qk, qk, kaug, vt, qk_m, kaug_m, vt_m)


def _bias_terms(neg_bias):
    terms, rest = [], neg_bias
    for _ in range(BIAS_TERMS):
        t = rest.astype(BF16)
        terms.append(t)
        rest = rest - t.astype(F32)
    return jnp.stack(terms)


def _mix_kernel(x_ref, of_ref, op_ref, wo_ref, gi_ref, bi_ref, gm_ref, bm_ref, wr_ref, br_ref, tri_ref,
                h_ref, hp_ref, ri_ref, rg_ref, cnt_ref, carry_ref, *, chunk):
    i = pl.program_id(0)
    tm = x_ref.shape[0]
    d_model = x_ref.shape[1]
    h0 = _layer_norm(x_ref[...], gi_ref[...], bi_ref[...])
    of = of_ref[...]
    op = op_ref[...]
    for c in range(0, d_model, chunk):
        mix = jnp.dot(of, wo_ref[0:D_FOX, c:c + chunk], preferred_element_type=F32)
        mix = mix + jnp.dot(op, wo_ref[D_FOX:, c:c + chunk], preferred_element_type=F32)
        h_ref[:, c:c + chunk] = DEEPNORM_ALPHA * h0[:, c:c + chunk] + mix
    h1 = _layer_norm(h_ref[...], gm_ref[...], bm_ref[...])
    h_ref[...] = h1
    _store_slabs(hp_ref, _pack_halves(h1))

    h_hi = h1.astype(BF16)
    h_mid = (h1 - h_hi.astype(F32)).astype(BF16)
    hh = jnp.dot(h_hi, wr_ref[...], preferred_element_type=F32)
    mh = jnp.dot(h_mid, wr_ref[:, :LANES], preferred_element_type=F32)
    logits = hh[:, :LANES] + hh[:, LANES:] + mh + br_ref[...]
    lane = lax.broadcasted_iota(jnp.int32, logits.shape, 1)
    neg = -jnp.inf
    gl = jnp.where(lane < N_GROUPS, logits, neg)
    gmax = jnp.max(gl, axis=-1, keepdims=True)
    g_idx = jnp.min(jnp.where(gl == gmax, lane, LANES), axis=-1, keepdims=True)
    p_g = 1.0 / jnp.sum(jnp.exp(gl - gmax), axis=-1, keepdims=True)
    lo = N_GROUPS + g_idx * EXPERTS_PER_GROUP
    el = jnp.where((lane >= lo) & (lane < lo + EXPERTS_PER_GROUP), logits, neg)
    v1 = jnp.max(el, axis=-1, keepdims=True)
    i1 = jnp.min(jnp.where(el == v1, lane, LANES), axis=-1, keepdims=True)
    el2 = jnp.where(lane == i1, neg, el)
    v2 = jnp.max(el2, axis=-1, keepdims=True)
    i2 = jnp.min(jnp.where(el2 == v2, lane, LANES), axis=-1, keepdims=True)
    t = jnp.exp(v2 - v1)
    gate1 = p_g / (1.0 + t)
    gate2 = gate1 * t
    e1 = i1 - N_GROUPS
    e2 = i2 - N_GROUPS

    @pl.when(i == 0)
    def _():
        carry_ref[...] = jnp.zeros_like(carry_ref)

    lower = tri_ref[...]
    oh1 = (lane == e1).astype(F32)
    oh2 = (lane == e2).astype(F32)
    pre1 = jnp.dot(lower, oh1.astype(BF16), preferred_element_type=F32)
    pre2 = jnp.dot(lower, oh2.astype(BF16), preferred_element_type=F32)
    cnt1 = jnp.sum(oh1, axis=0, keepdims=True)
    cnt2 = jnp.sum(oh2, axis=0, keepdims=True)
    base = carry_ref[...]
    rank1 = jnp.sum((pre1 + base) * oh1, axis=-1, keepdims=True)
    rank2 = jnp.sum((pre2 + base + cnt1) * oh2, axis=-1, keepdims=True)
    total = base + cnt1 + cnt2
    carry_ref[...] = total
    cnt_ref[...] = total

    ri_ref[...] = jnp.where(lane == 0, e1, jnp.where(lane == 1, e2, jnp.where(
        lane == 2, rank1.astype(jnp.int32), jnp.where(lane == 3, rank2.astype(jnp.int32), 0))))
    rg_ref[...] = jnp.where(lane == 0, gate1, jnp.where(lane == 1, gate2, 0.0))


def _mix(x2, o_fox, o_pool, wo, gi, bi, gm, bm, wr, br, *, tm):
    rows, d_model = x2.shape
    idx = jnp.arange(tm, dtype=jnp.int32)
    tri = (idx[None, :] < idx[:, None]).astype(BF16)
    kern = functools.partial(_mix_kernel, chunk=512)
    row_spec = lambda w: pl.BlockSpec((tm, w), lambda i: (i, 0))
    vec_spec = lambda w: pl.BlockSpec((1, w), lambda i: (0, 0))
    return pl.pallas_call(
        kern,
        grid=(rows // tm,),
        in_specs=[
            row_spec(d_model), row_spec(D_FOX), row_spec(D_POOL),
            pl.BlockSpec(wo.shape, lambda i: (0, 0)),
            vec_spec(d_model), vec_spec(d_model), vec_spec(d_model), vec_spec(d_model),
            pl.BlockSpec(wr.shape, lambda i: (0, 0)),
            vec_spec(LANES),
            pl.BlockSpec((tm, tm), lambda i: (0, 0)),
        ],
        out_specs=[row_spec(d_model), pl.BlockSpec((tm * SLAB_ROWS, LANES), lambda i: (i, 0)), row_spec(LANES),
                   row_spec(LANES), vec_spec(LANES)],
        out_shape=[
            jax.ShapeDtypeStruct((rows, d_model), F32),
            jax.ShapeDtypeStruct((rows * SLAB_ROWS, LANES), jnp.uint32),
            jax.ShapeDtypeStruct((rows, LANES), jnp.int32),
            jax.ShapeDtypeStruct((rows, LANES), F32),
            jax.ShapeDtypeStruct((1, LANES), F32),
        ],
        scratch_shapes=[pltpu.VMEM((1, LANES), F32)],
        compiler_params=_params(1),
        name="mix",
    )(x2, o_fox, o_pool, wo, gi, bi, gm, bm, wr, br, tri)


def _moe_kernel(be_ref, na_ref, tok_ref, tokn_ref, dst_ref, h_hbm, w13_ref, w2_ref, y_hbm, xbuf, obuf, gsem, ssem,
                *, chunk):
    i = pl.program_id(0)
    n_act = na_ref[0]
    te = xbuf.shape[1] * SLAB_ROWS
    slot = i % 2

    def row_view(buf, s, r):
        return buf.at[s, r // SLAB_ROWS, :, r % SLAB_ROWS, :]

    def start_gather(idx_ref, s):
        for r in range(te):
            pltpu.make_async_copy(h_hbm.at[idx_ref[0, 0, r]], row_view(xbuf, s, r), gsem.at[s]).start(priority=r % 2)

    def wait_gather(s):
        pltpu.make_async_copy(xbuf.at[1 - s], xbuf.at[s], gsem.at[s]).wait()

    def start_scatter(idx, s):
        for r in range(te):
            pltpu.make_async_copy(row_view(obuf, s, r), y_hbm.at[idx(r)], ssem.at[s]).start(priority=r % 2)

    def wait_scatter(s):
        pltpu.make_async_copy(obuf.at[1 - s], obuf.at[s], ssem.at[s]).wait()

    @pl.when(i < n_act)
    def _():
        @pl.when(i == 0)
        def _():
            start_gather(tok_ref, slot)
            spare = y_hbm.shape[0] - 2 * te
            obuf[...] = jnp.zeros_like(obuf)
            for s in range(2):
                start_scatter(lambda r, s=s: spare + s * te + r, s)
            for s in range(2):
                wait_scatter(s)

        wait_gather(slot)

        @pl.when(i + 1 < n_act)
        def _():
            start_gather(tokn_ref, 1 - slot)

        @pl.when(i >= 2)
        def _():
            wait_scatter(slot)

        x = _unpack_halves(_row_tiles_to_value(xbuf.at[slot])).astype(BF16)
        out = None
        for c in range(0, D_EXPERT, chunk):
            gate = jnp.dot(x, w13_ref[0, :, c:c + chunk].astype(BF16), preferred_element_type=F32)
            up = jnp.dot(x, w13_ref[0, :, D_EXPERT + c:D_EXPERT + c + chunk].astype(BF16),
                         preferred_element_type=F32)
            hid = (gate / (1.0 + jnp.exp(-gate)) * up).astype(BF16)
            part = jnp.dot(hid, w2_ref[0, c:c + chunk, :].astype(BF16), preferred_element_type=F32)
            out = part if out is None else out + part
        _value_to_row_tiles(obuf.at[slot], _pack_halves(out))
        start_scatter(lambda r: dst_ref[0, 0, r], slot)

        @pl.when(i == n_act - 1)
        def _():
            @pl.when(i >= 1)
            def _():
                wait_scatter(1 - slot)
            wait_scatter(slot)


def _moe(blk_e, n_act, rows_tok, rows_dst, h1p, w13, w2, *, te):
    rows = h1p.shape[0] // SLAB_ROWS
    words = SLAB_ROWS * LANES
    d_model = 2 * words
    nb = rows_tok.shape[0]
    kern = functools.partial(_moe_kernel, chunk=1024)
    smem_spec = lambda off: pl.BlockSpec((1, 1, te), lambda i, be, na: (jnp.minimum(i + off, nb - 1), 0, 0),
                                         memory_space=pltpu.SMEM)
    tiles = (2, te // SLAB_ROWS, words // LANES, SLAB_ROWS, LANES)
    y = pl.pallas_call(
        kern,
        grid_spec=pltpu.PrefetchScalarGridSpec(
            num_scalar_prefetch=2,
            grid=(nb,),
            in_specs=[
                smem_spec(0), smem_spec(1), smem_spec(0),
                pl.BlockSpec(memory_space=pl.ANY),
                pl.BlockSpec((1, d_model, 2 * D_EXPERT), lambda i, be, na: (be[i], 0, 0)),
                pl.BlockSpec((1, D_EXPERT, d_model), lambda i, be, na: (be[i], 0, 0)),
            ],
            out_specs=pl.BlockSpec(memory_space=pl.ANY),
            scratch_shapes=[
                pltpu.VMEM(tiles, jnp.uint32),
                pltpu.VMEM(tiles, jnp.uint32),
                pltpu.SemaphoreType.DMA((2,)),
                pltpu.SemaphoreType.DMA((2,)),
            ],
        ),
        out_shape=jax.ShapeDtypeStruct((2 * rows + 2 * te, SLAB_ROWS, LANES), jnp.uint32),
        compiler_params=_params(1, vmem=MOE_VMEM_LIMIT),
        name="moe",
    )(blk_e, n_act, rows_tok, rows_tok, rows_dst, h1p.reshape(rows, SLAB_ROWS, LANES), w13, w2)
    return y.reshape(-1, LANES)


def _final_kernel(h_ref, y0_ref, y1_ref, rg_ref, g_ref, b_ref, o_ref):
    rg = rg_ref[...]
    ffn = _unpack_halves(_load_slabs(y0_ref)) * rg[:, 0:1] + _unpack_halves(_load_slabs(y1_ref)) * rg[:, 1:2]
    o_ref[...] = _layer_norm(DEEPNORM_ALPHA * h_ref[...] + ffn, g_ref[...], b_ref[...])


def _final(h1, y, rg, g, b, *, tm):
    rows, d_model = h1.shape
    nt = rows // tm
    return pl.pallas_call(
        _final_kernel,
        grid=(nt,),
        in_specs=[
            pl.BlockSpec((tm, d_model), lambda i: (i, 0)),
            pl.BlockSpec((tm * SLAB_ROWS, LANES), lambda i: (i, 0)),
            pl.BlockSpec((tm * SLAB_ROWS, LANES), lambda i: (nt + i, 0)),
            pl.BlockSpec((tm, LANES), lambda i: (i, 0)),
            pl.BlockSpec((1, d_model), lambda i: (0, 0)),
            pl.BlockSpec((1, d_model), lambda i: (0, 0)),
        ],
        out_specs=pl.BlockSpec((tm, d_model), lambda i: (i, 0)),
        out_shape=jax.ShapeDtypeStruct((rows, d_model), F32),
        compiler_params=_params(1),
        name="final",
    )(h1, y, y, rg, g, b)


def _dispatch_tables(ri, cnt, *, rows, te):
    experts = ri[:, 0:2]
    rank = ri[:, 2:4]
    counts = cnt[0, :N_EXPERTS].astype(jnp.int32)
    padded = ((counts + te - 1) // te) * te
    pend = jnp.cumsum(padded)
    pstart = pend - padded
    eids = jnp.arange(N_EXPERTS, dtype=jnp.int32)
    seg = jnp.sum(jnp.where(experts[..., None] == eids, pstart, 0), axis=-1)
    dest = (seg + rank).reshape(-1)
    nb = (2 * rows + N_EXPERTS * (te - 1) + te - 1) // te
    tok = jnp.repeat(jnp.arange(rows, dtype=jnp.int32), 2)
    out_row = tok + jnp.tile(jnp.array([0, rows], jnp.int32), rows)
    pos = jnp.arange(nb * te, dtype=jnp.int32)
    rows_dst = (2 * rows + pos % (2 * te)).at[dest].set(out_row)
    rows_tok = jnp.where(rows_dst < rows, rows_dst, jnp.where(rows_dst < 2 * rows, rows_dst - rows, 0))
    blk_e = jnp.minimum(jnp.sum(pend[None, :] <= (jnp.arange(nb, dtype=jnp.int32) * te)[:, None], axis=1),
                        N_EXPERTS - 1).astype(jnp.int32)
    n_act = (pend[-1] // te).astype(jnp.int32).reshape(1)
    return blk_e, n_act, rows_tok.reshape(nb, 1, te), rows_dst.reshape(nb, 1, te)


def kernel(x, meta, ln_in_g, ln_in_b, w_in, b_f, pool_w, pool_scale, w_out, ln_mix_g, ln_mix_b,
           w_router_g, b_router_g, w_router_e, b_router_e, w13, w2, ln_ffn_g, ln_ffn_b):
    batch, seq, d_model = x.shape
    rows = batch * seq
    tile = min(512, seq)
    assert seq % (2 * tile) == 0 and tile % N_META == 0
    assert w_in.shape[0] == 1, "depth-1 trunk"

    x2 = x.reshape(rows, d_model)
    row = lambda v: v.reshape(1, -1).astype(F32)

    wi = w_in[0]
    w_f = jnp.pad(wi[:, 3 * D_FOX:3 * D_FOX + FOX_HEADS], ((0, 0), (0, LANES - FOX_HEADS)))
    w_proj = jnp.concatenate([wi[:, :3 * D_FOX], wi[:, 3 * D_FOX + FOX_HEADS:], w_f], axis=1).astype(BF16)
    bf = jnp.pad(b_f[0], (0, LANES - FOX_HEADS)).reshape(1, LANES)
    gi, bi = row(ln_in_g), row(ln_in_b)

    meta_pad = jnp.pad(meta.astype(F32), ((0, META_PAD - N_META), (0, 0)))
    sel = _bias_placement()
    pw, ps = pool_w[0].astype(BF16), row(pool_scale[0])
    no_halo = jnp.zeros((N_META, D_POOL), F32)
    qk_m, vt_m, u_m, _, c_m, _ = _proj(meta_pad, gi, bi, w_proj, bf, sel, no_halo, pw, ps, tm=META_PAD, tiles_per_seq=1)
    d_meta = (c_m[:N_META, :FOX_HEADS] - c_m[N_META - 1:N_META, :FOX_HEADS]) * LOG2E
    d_meta = jnp.pad(d_meta, ((0, META_PAD - N_META), (0, 0)), constant_values=MASKED_BIAS)
    terms = _bias_terms(-d_meta)
    kaug_m = jnp.zeros((META_PAD, FOX_HEADS, HEAD_DIM), BF16).at[:, :, :BIAS_TERMS].set(terms.transpose(1, 2, 0))
    kaug_m = kaug_m.reshape(META_PAD, D_FOX)

    qk, vt, _, o_pool, _, kaug = _proj(x2, gi, bi, w_proj, bf, sel, u_m[:N_META], pw, ps, tm=tile,
                                       tiles_per_seq=seq // tile)
    o_fox = _attention(qk, kaug, vt, qk_m, kaug_m, vt_m, batch=batch, seq=seq, th=tile, tk=tile)

    w_r = jnp.pad(jnp.concatenate([w_router_g[0], w_router_e[0]], axis=1),
                  ((0, 0), (0, LANES - N_GROUPS - N_EXPERTS)))
    w_r_hi = w_r.astype(BF16)
    w_r = jnp.concatenate([w_r_hi, (w_r - w_r_hi.astype(F32)).astype(BF16)], axis=1)
    b_r = jnp.pad(jnp.concatenate([b_router_g[0], b_router_e[0]]), (0, LANES - N_GROUPS - N_EXPERTS)).reshape(1, LANES)
    h1, h1p, ri, rg, cnt = _mix(x2, o_fox, o_pool, w_out[0].astype(BF16), gi, bi, row(ln_mix_g[0]),
                                row(ln_mix_b[0]), w_r, b_r, tm=tile)

    te = 256
    blk_e, n_act, rows_tok, rows_dst = _dispatch_tables(ri, cnt, rows=rows, te=te)
    y = _moe(blk_e, n_act, rows_tok, rows_dst, h1p, w13[0], w2[0], te=te)

    out = _final(h1, y, rg, row(ln_ffn_g[0]), row(ln_ffn_b[0]), tm=tile)
    return out.reshape(batch, seq, d_model)
```

```python
import functools

import jax
import jax.numpy as jnp
from jax import lax
from jax.experimental import pallas as pl
from jax.experimental.pallas import tpu as pltpu

F32 = jnp.float32
BF16 = jnp.bfloat16

N_META = 16
FOX_HEADS = 8
HEAD_DIM = 128
D_FOX = FOX_HEADS * HEAD_DIM
POOL_WINDOWS = (2, 4, 8, 16)
POOL_GROUP_DIM = 256
D_POOL = len(POOL_WINDOWS) * POOL_GROUP_DIM
N_GROUPS = 4
EXPERTS_PER_GROUP = 8
N_EXPERTS = N_GROUPS * EXPERTS_PER_GROUP
D_EXPERT = 1024
LN_EPS = 1e-5
DEEPNORM_ALPHA = 2.0 ** 0.25

LANES = 128
SLAB_ROWS = 8
META_PAD = 128
MASKED_BIAS = 1e30
LOG2E = 1.4426950408889634
BIAS_TERMS = 3
ROUTE_ROWS = 48
VT_ROWS = 144
VMEM_LIMIT = 56 * 1024 * 1024
MOE_VMEM_LIMIT = 62 * 1024 * 1024


def _params(n_axes, vmem=VMEM_LIMIT):
    return pltpu.CompilerParams(dimension_semantics=("arbitrary",) * n_axes, vmem_limit_bytes=vmem)


def _layer_norm(x, g, b):
    mu = jnp.mean(x, axis=-1, keepdims=True)
    xc = x - mu
    var = jnp.mean(xc * xc, axis=-1, keepdims=True)
    return xc * lax.rsqrt(var + LN_EPS) * g + b


def _pack_halves(x):
    n = x.shape[1] // 2
    bits = lambda v: lax.bitcast_convert_type(v.astype(BF16).astype(F32), jnp.uint32)
    return (bits(x[:, :n]) >> 16) | (bits(x[:, n:]) & jnp.uint32(0xFFFF0000))


def _unpack_halves(w):
    lo = lax.bitcast_convert_type(w << 16, F32)
    hi = lax.bitcast_convert_type(w & jnp.uint32(0xFFFF0000), F32)
    return jnp.concatenate([lo, hi], axis=1)


def _store_slabs(ref, words):
    rows = words.shape[0]
    for k in range(SLAB_ROWS):
        ref[pl.ds(k, rows, stride=SLAB_ROWS), :] = words[:, k * LANES:(k + 1) * LANES]


def _load_slabs(ref):
    rows = ref.shape[0] // SLAB_ROWS
    return jnp.concatenate([ref[pl.ds(k, rows, stride=SLAB_ROWS), :] for k in range(SLAB_ROWS)], axis=1)


def _row_tiles_to_value(ref):
    rt, lt = ref.shape[0], ref.shape[1]
    return jnp.concatenate([ref[:, j].reshape(rt * SLAB_ROWS, LANES) for j in range(lt)], axis=1)


def _value_to_row_tiles(ref, value):
    rt, lt = ref.shape[0], ref.shape[1]
    for j in range(lt):
        ref[:, j] = value[:, j * LANES:(j + 1) * LANES].reshape(rt, SLAB_ROWS, LANES)


def _proj_kernel(x_ref, g_ref, b_ref, w_ref, bf_ref, sel_ref, um_ref, pw_ref, ps_ref,
                 qk_ref, vt_ref, u_ref, op_ref, d_ref, ka_ref, carry_ref, ext_ref, *, tiles_per_seq, chunk):
    i = pl.program_id(0)
    tm = x_ref.shape[0]
    xn = _layer_norm(x_ref[...], g_ref[...], b_ref[...]).astype(BF16)
    first_tile = i % tiles_per_seq == 0
    ext_ref[0:N_META, :] = jnp.where(first_tile, um_ref[...], ext_ref[0:N_META, :])
    for c in range(0, D_POOL, chunk):
        uc = jnp.dot(xn, w_ref[:, 3 * D_FOX + c:3 * D_FOX + c + chunk], preferred_element_type=F32)
        u_ref[:, c:c + chunk] = uc
        ext_ref[N_META:, c:c + chunk] = uc
    for g, w in enumerate(POOL_WINDOWS):
        cols = slice(g * POOL_GROUP_DIM, (g + 1) * POOL_GROUP_DIM)
        tok = ext_ref[N_META:N_META + tm, cols]
        acc = tok
        for j in range(1, w):
            acc = acc + ext_ref[N_META - j:N_META - j + tm, cols]
        pooled = acc * (1.0 / w) - tok
        mixed = jnp.dot(pooled.astype(BF16), pw_ref[g], preferred_element_type=F32)
        op_ref[:, cols] = (mixed * ps_ref[:, cols]).astype(BF16)
    ext_ref[0:N_META, :] = ext_ref[tm:tm + N_META, :]

    scale = HEAD_DIM ** -0.5 * LOG2E
    for c in range(0, 2 * D_FOX, chunk):
        acc = jnp.dot(xn, w_ref[:, c:c + chunk], preferred_element_type=F32)
        if c < D_FOX:
            acc = acc * scale
        qk_ref[:, c:c + chunk] = acc.astype(BF16)
    sub = lax.broadcasted_iota(jnp.int32, (VT_ROWS - HEAD_DIM, tm), 0)
    tail = jnp.where(sub == 0, 1.0, 0.0).astype(BF16)
    for c in range(0, D_FOX, chunk):
        acc = jnp.dot(xn, w_ref[:, 2 * D_FOX + c:2 * D_FOX + c + chunk], preferred_element_type=F32)
        for hd in range(chunk // HEAD_DIM):
            r0 = (c // HEAD_DIM + hd) * VT_ROWS
            vt_ref[r0:r0 + HEAD_DIM, :] = acc[:, hd * HEAD_DIM:(hd + 1) * HEAD_DIM].T.astype(BF16)
            vt_ref[r0 + HEAD_DIM:r0 + VT_ROWS, :] = tail
    fl = jnp.dot(xn, w_ref[:, 3 * D_FOX + D_POOL:], preferred_element_type=F32) + bf_ref[...]
    lf = jnp.minimum(fl, 0.0) - jnp.log1p(jnp.exp(-jnp.abs(fl)))

    row = lax.broadcasted_iota(jnp.int32, lf.shape, 0)
    acc = lf
    k = 1
    while k < tm:
        acc = acc + jnp.where(row >= k, pltpu.roll(acc, k, 0), 0.0)
        k *= 2
    d = acc + jnp.where(first_tile, 0.0, carry_ref[...])
    d_ref[...] = d
    carry_ref[...] = d[tm - 1:tm, :]

    nd = d * (-LOG2E)
    hi = nd.astype(BF16)
    r1 = nd - hi.astype(F32)
    mid = r1.astype(BF16)
    lo = (r1 - mid.astype(F32)).astype(BF16)
    split = jnp.concatenate([hi, mid, lo], axis=1)
    ka_ref[...] = jnp.dot(split, sel_ref[...], preferred_element_type=F32).astype(BF16)


def _bias_placement():
    t, h = jnp.meshgrid(jnp.arange(BIAS_TERMS), jnp.arange(FOX_HEADS), indexing="ij")
    sel = jnp.zeros((BIAS_TERMS * LANES, D_FOX), F32)
    return sel.at[(t * LANES + h).ravel(), (h * HEAD_DIM + t).ravel()].set(1.0).astype(BF16)


def _proj(x2, g, b, w, bf, sel, um, pw, ps, *, tm, tiles_per_seq):
    rows, d_model = x2.shape
    n_proj = w.shape[1]
    kern = functools.partial(_proj_kernel, tiles_per_seq=tiles_per_seq, chunk=512)
    return pl.pallas_call(
        kern,
        grid=(rows // tm,),
        in_specs=[
            pl.BlockSpec((tm, d_model), lambda i: (i, 0)),
            pl.BlockSpec((1, d_model), lambda i: (0, 0)),
            pl.BlockSpec((1, d_model), lambda i: (0, 0)),
            pl.BlockSpec((d_model, n_proj), lambda i: (0, 0)),
            pl.BlockSpec((1, LANES), lambda i: (0, 0)),
            pl.BlockSpec(sel.shape, lambda i: (0, 0)),
            pl.BlockSpec((N_META, D_POOL), lambda i: (0, 0)),
            pl.BlockSpec(pw.shape, lambda i: (0, 0, 0)),
            pl.BlockSpec((1, D_POOL), lambda i: (0, 0)),
        ],
        out_specs=[
            pl.BlockSpec((tm, 2 * D_FOX), lambda i: (i, 0)),
            pl.BlockSpec((FOX_HEADS * VT_ROWS, tm), lambda i: (i // tiles_per_seq, i % tiles_per_seq)),
            pl.BlockSpec((tm, D_POOL), lambda i: (i, 0)),
            pl.BlockSpec((tm, D_POOL), lambda i: (i, 0)),
            pl.BlockSpec((tm, LANES), lambda i: (i, 0)),
            pl.BlockSpec((tm, D_FOX), lambda i: (i, 0)),
        ],
        out_shape=[
            jax.ShapeDtypeStruct((rows, 2 * D_FOX), BF16),
            jax.ShapeDtypeStruct((rows // (tm * tiles_per_seq) * FOX_HEADS * VT_ROWS, tm * tiles_per_seq), BF16),
            jax.ShapeDtypeStruct((rows, D_POOL), F32),
            jax.ShapeDtypeStruct((rows, D_POOL), BF16),
            jax.ShapeDtypeStruct((rows, LANES), F32),
            jax.ShapeDtypeStruct((rows, D_FOX), BF16),
        ],
        scratch_shapes=[pltpu.VMEM((1, LANES), F32), pltpu.VMEM((N_META + tm, D_POOL), F32)],
        compiler_params=_params(1),
        name="proj",
    )(x2, g, b, w, bf, sel, um, pw, ps)


def _attn_kernel(q_ref, k_ref, ka_ref, vt_ref, km_ref, kam_ref, vtm_ref, o_ref, sa_ref, sb_ref, *, th, tk):
    qi = pl.program_id(2)
    nt = (((1,), (1,)), ((), ()))
    lane = lax.broadcasted_iota(jnp.int32, (th, HEAD_DIM), 1)
    ones = jnp.where(lane < BIAS_TERMS, 1.0, 0.0).astype(BF16)
    qa = [jnp.concatenate([q_ref[h * th:(h + 1) * th, :], ones], axis=1) for h in range(2)]
    vrows = vt_ref.shape[0]

    def scores(keys, half):
        return lax.dot_general(keys, qa[half], nt, preferred_element_type=F32)

    def x_keys(j):
        ks = pl.multiple_of(j * tk, tk)
        return jnp.concatenate([k_ref[pl.ds(ks, tk), :], ka_ref[pl.ds(ks, tk), :]], axis=1)

    def update(read_s, vt, carry, masked=False):
        m, acc = carry
        if masked:
            key = lax.broadcasted_iota(jnp.int32, (tk, th), 0)
            qry = lax.broadcasted_iota(jnp.int32, (tk, th), 1)
            read = lambda: jnp.where(key <= qry, read_s(), -jnp.inf)
        else:
            read = read_s
        m_new = jnp.maximum(m, jnp.max(read(), axis=0, keepdims=True))
        a = jnp.exp2(m - m_new)
        p = jnp.exp2(read() - m_new).astype(BF16)
        return m_new, a * acc + jnp.dot(vt, p, preferred_element_type=F32)

    def x_vt(j):
        return vt_ref[:, pl.ds(pl.multiple_of(j * tk, tk), tk)]

    keys_m = jnp.concatenate([km_ref[...], kam_ref[...]], axis=1)
    carry = []
    for h in range(2):
        s_m = scores(keys_m, h)
        init = (jnp.full((1, th), -jnp.inf, F32), jnp.zeros((vrows, th), F32))
        carry.append(update(lambda: s_m, vtm_ref[...], init))

    def fill(buf, j):
        keys = x_keys(j)
        for h in range(2):
            buf[h] = scores(keys, h)

    def drain(buf, j, carry):
        vt = x_vt(j)
        return tuple(update(lambda h=h: buf[h], vt, carry[h]) for h in range(2))

    def body(jj, carry):
        j = 2 * jj
        fill(sb_ref, j + 1)
        carry = drain(sa_ref, j, carry)
        fill(sa_ref, j + 2)
        return drain(sb_ref, j + 1, carry)

    first = 2 * qi
    fill(sa_ref, 0)
    c0, c1 = lax.fori_loop(0, qi, body, tuple(carry))
    s11 = scores(x_keys(first + 1), 1)
    c0 = update(lambda: sa_ref[0], x_vt(first), c0, masked=True)
    c1 = update(lambda: sa_ref[1], x_vt(first), c1)
    c1 = update(lambda: s11, x_vt(first + 1), c1, masked=True)
    for h, (m, acc) in enumerate((c0, c1)):
        out_t = acc[:HEAD_DIM] / acc[HEAD_DIM:HEAD_DIM + 1]
        o_ref[h * th:(h + 1) * th, :] = out_t.T.astype(BF16)


def _attention(qk, kaug, vt, qk_m, kaug_m, vt_m, *, batch, seq, th, tk):
    assert th == tk
    tq = 2 * th
    nq = seq // tq
    vrows = vt.shape[0] // (batch * FOX_HEADS)
    kern = functools.partial(_attn_kernel, th=th, tk=tk)
    return pl.pallas_call(
        kern,
        grid=(batch, FOX_HEADS, nq),
        in_specs=[
            pl.BlockSpec((tq, HEAD_DIM), lambda b, h, i: (b * nq + i, h)),
            pl.BlockSpec((seq, HEAD_DIM), lambda b, h, i: (b, FOX_HEADS + h)),
            pl.BlockSpec((seq, HEAD_DIM), lambda b, h, i: (b, h)),
            pl.BlockSpec((vrows, seq), lambda b, h, i: (b * FOX_HEADS + h, 0)),
            pl.BlockSpec((META_PAD, HEAD_DIM), lambda b, h, i: (0, FOX_HEADS + h)),
            pl.BlockSpec((META_PAD, HEAD_DIM), lambda b, h, i: (0, h)),
            pl.BlockSpec((vrows, META_PAD), lambda b, h, i: (h, 0)),
        ],
        out_specs=pl.BlockSpec((tq, HEAD_DIM), lambda b, h, i: (b * nq + i, h)),
        out_shape=jax.ShapeDtypeStruct((batch * seq, D_FOX), BF16),
        scratch_shapes=[pltpu.VMEM((2, tk, th), F32), pltpu.VMEM((2, tk, th), F32)],
        compiler_params=_params(3),
        name="attn",
    )(qk, qk, kaug, vt, qk_m, kaug_m, vt_m)


def _bias_terms(neg_bias):
    terms, rest = [], neg_bias
    for _ in range(BIAS_TERMS):
        t = rest.astype(BF16)
        terms.append(t)
        rest = rest - t.astype(F32)
    return jnp.stack(terms)


def _mix_kernel(x_ref, of_ref, op_ref, wo_ref, gi_ref, bi_ref, gm_ref, bm_ref, wr_ref, br_ref, tri_ref,
                h_ref, hp_ref, ri_ref, rg_ref, cnt_ref, carry_ref, *, chunk):
    i = pl.program_id(0)
    tm = x_ref.shape[0]
    d_model = x_ref.shape[1]
    h0 = _layer_norm(x_ref[...], gi_ref[...], bi_ref[...])
    of = of_ref[...]
    op = op_ref[...]
    for c in range(0, d_model, chunk):
        mix = jnp.dot(of, wo_ref[0:D_FOX, c:c + chunk], preferred_element_type=F32)
        mix = mix + jnp.dot(op, wo_ref[D_FOX:, c:c + chunk], preferred_element_type=F32)
        h_ref[:, c:c + chunk] = DEEPNORM_ALPHA * h0[:, c:c + chunk] + mix
    h1 = _layer_norm(h_ref[...], gm_ref[...], bm_ref[...])
    h_ref[...] = h1
    _store_slabs(hp_ref, _pack_halves(h1))

    h_hi = h1.astype(BF16)
    h_mid = (h1 - h_hi.astype(F32)).astype(BF16)
    hh = jnp.dot(h_hi, wr_ref[...], preferred_element_type=F32)
    mh = jnp.dot(h_mid, wr_ref[:, :LANES], preferred_element_type=F32)
    logits = hh[:, :LANES] + hh[:, LANES:] + mh + br_ref[...]
    lt = logits.T[:ROUTE_ROWS]
    row = lax.broadcasted_iota(jnp.int32, lt.shape, 0)
    neg = -jnp.inf
    first_min = lambda hit: jnp.min(jnp.where(hit, row, ROUTE_ROWS), axis=0, keepdims=True)
    gl = jnp.where(row < N_GROUPS, lt, neg)
    gmax = jnp.max(gl, axis=0, keepdims=True)
    g_idx = first_min(gl == gmax)
    p_g = 1.0 / jnp.sum(jnp.exp(gl - gmax), axis=0, keepdims=True)
    lo = N_GROUPS + g_idx * EXPERTS_PER_GROUP
    el = jnp.where((row >= lo) & (row < lo + EXPERTS_PER_GROUP), lt, neg)
    v1 = jnp.max(el, axis=0, keepdims=True)
    i1 = first_min(el == v1)
    el2 = jnp.where(row == i1, neg, el)
    v2 = jnp.max(el2, axis=0, keepdims=True)
    i2 = first_min(el2 == v2)
    t = jnp.exp(v2 - v1)
    gate1 = p_g / (1.0 + t)
    gate2 = gate1 * t

    @pl.when(i == 0)
    def _():
        carry_ref[...] = jnp.zeros_like(carry_ref)

    upper = tri_ref[...]
    oh1 = (row == i1).astype(F32)
    oh2 = (row == i2).astype(F32)
    pre1 = jnp.dot(oh1.astype(BF16), upper, preferred_element_type=F32)
    pre2 = jnp.dot(oh2.astype(BF16), upper, preferred_element_type=F32)
    cnt1 = jnp.sum(oh1, axis=1, keepdims=True)
    cnt2 = jnp.sum(oh2, axis=1, keepdims=True)
    base = carry_ref[...]
    rank1 = jnp.sum((pre1 + base) * oh1, axis=0, keepdims=True)
    rank2 = jnp.sum((pre2 + base + cnt1) * oh2, axis=0, keepdims=True)
    total = base + cnt1 + cnt2
    carry_ref[...] = total
    cnt_ref[...] = jnp.broadcast_to(total, cnt_ref.shape)

    r8 = lax.broadcasted_iota(jnp.int32, ri_ref.shape, 0)
    ri_ref[...] = jnp.where(r8 == 0, i1 - N_GROUPS, jnp.where(r8 == 1, i2 - N_GROUPS, jnp.where(
        r8 == 2, rank1.astype(jnp.int32), jnp.where(r8 == 3, rank2.astype(jnp.int32), 0))))
    rl = lax.broadcasted_iota(jnp.int32, (LANES, tm), 0)
    rg_ref[...] = jnp.where(rl == 0, gate1, jnp.where(rl == 1, gate2, 0.0)).T


def _mix(x2, o_fox, o_pool, wo, gi, bi, gm, bm, wr, br, *, tm):
    rows, d_model = x2.shape
    idx = jnp.arange(tm, dtype=jnp.int32)
    tri = (idx[:, None] < idx[None, :]).astype(BF16)
    kern = functools.partial(_mix_kernel, chunk=512)
    row_spec = lambda w: pl.BlockSpec((tm, w), lambda i: (i, 0))
    vec_spec = lambda w: pl.BlockSpec((1, w), lambda i: (0, 0))
    return pl.pallas_call(
        kern,
        grid=(rows // tm,),
        in_specs=[
            row_spec(d_model), row_spec(D_FOX), row_spec(D_POOL),
            pl.BlockSpec(wo.shape, lambda i: (0, 0)),
            vec_spec(d_model), vec_spec(d_model), vec_spec(d_model), vec_spec(d_model),
            pl.BlockSpec(wr.shape, lambda i: (0, 0)),
            vec_spec(LANES),
            pl.BlockSpec((tm, tm), lambda i: (0, 0)),
        ],
        out_specs=[row_spec(d_model), pl.BlockSpec((tm * SLAB_ROWS, LANES), lambda i: (i, 0)),
                   pl.BlockSpec((SLAB_ROWS, tm), lambda i: (0, i)), row_spec(LANES),
                   pl.BlockSpec((ROUTE_ROWS, LANES), lambda i: (0, 0))],
        out_shape=[
            jax.ShapeDtypeStruct((rows, d_model), F32),
            jax.ShapeDtypeStruct((rows * SLAB_ROWS, LANES), jnp.uint32),
            jax.ShapeDtypeStruct((SLAB_ROWS, rows), jnp.int32),
            jax.ShapeDtypeStruct((rows, LANES), F32),
            jax.ShapeDtypeStruct((ROUTE_ROWS, LANES), F32),
        ],
        scratch_shapes=[pltpu.VMEM((ROUTE_ROWS, 1), F32)],
        compiler_params=_params(1),
        name="mix",
    )(x2, o_fox, o_pool, wo, gi, bi, gm, bm, wr, br, tri)


def _moe_kernel(be_ref, na_ref, tok_ref, tokn_ref, dst_ref, h_hbm, w13_ref, w2_ref, y_hbm, xbuf, obuf, gsem, ssem,
                *, chunk):
    i = pl.program_id(0)
    n_act = na_ref[0]
    te = xbuf.shape[1] * SLAB_ROWS
    slot = i % 2

    def row_view(buf, s, r):
        return buf.at[s, r // SLAB_ROWS, :, r % SLAB_ROWS, :]

    def start_gather(idx_ref, s):
        for r in range(te):
            pltpu.make_async_copy(h_hbm.at[idx_ref[0, 0, r]], row_view(xbuf, s, r), gsem.at[s]).start(priority=r % 2)

    def wait_gather(s):
        pltpu.make_async_copy(xbuf.at[1 - s], xbuf.at[s], gsem.at[s]).wait()

    def start_scatter(idx, s):
        for r in range(te):
            pltpu.make_async_copy(row_view(obuf, s, r), y_hbm.at[idx(r)], ssem.at[s]).start(priority=r % 2)

    def wait_scatter(s):
        pltpu.make_async_copy(obuf.at[1 - s], obuf.at[s], ssem.at[s]).wait()

    @pl.when(i < n_act)
    def _():
        @pl.when(i == 0)
        def _():
            start_gather(tok_ref, slot)
            spare = y_hbm.shape[0] - 2 * te
            obuf[...] = jnp.zeros_like(obuf)
            for s in range(2):
                start_scatter(lambda r, s=s: spare + s * te + r, s)
            for s in range(2):
                wait_scatter(s)

        wait_gather(slot)

        @pl.when(i + 1 < n_act)
        def _():
            start_gather(tokn_ref, 1 - slot)

        @pl.when(i >= 2)
        def _():
            wait_scatter(slot)

        x = _unpack_halves(_row_tiles_to_value(xbuf.at[slot])).astype(BF16)
        out = None
        for c in range(0, D_EXPERT, chunk):
            gate = jnp.dot(x, w13_ref[0, :, c:c + chunk].astype(BF16), preferred_element_type=F32)
            up = jnp.dot(x, w13_ref[0, :, D_EXPERT + c:D_EXPERT + c + chunk].astype(BF16),
                         preferred_element_type=F32)
            hid = (gate / (1.0 + jnp.exp(-gate)) * up).astype(BF16)
            part = jnp.dot(hid, w2_ref[0, c:c + chunk, :].astype(BF16), preferred_element_type=F32)
            out = part if out is None else out + part
        _value_to_row_tiles(obuf.at[slot], _pack_halves(out))
        start_scatter(lambda r: dst_ref[0, 0, r], slot)

        @pl.when(i == n_act - 1)
        def _():
            @pl.when(i >= 1)
            def _():
                wait_scatter(1 - slot)
            wait_scatter(slot)


def _moe(blk_e, n_act, rows_tok, rows_dst, h1p, w13, w2, *, te):
    rows = h1p.shape[0] // SLAB_ROWS
    words = SLAB_ROWS * LANES
    d_model = 2 * words
    nb = rows_tok.shape[0]
    kern = functools.partial(_moe_kernel, chunk=1024)
    smem_spec = lambda off: pl.BlockSpec((1, 1, te), lambda i, be, na: (jnp.minimum(i + off, nb - 1), 0, 0),
                                         memory_space=pltpu.SMEM)
    tiles = (2, te // SLAB_ROWS, words // LANES, SLAB_ROWS, LANES)
    y = pl.pallas_call(
        kern,
        grid_spec=pltpu.PrefetchScalarGridSpec(
            num_scalar_prefetch=2,
            grid=(nb,),
            in_specs=[
                smem_spec(0), smem_spec(1), smem_spec(0),
                pl.BlockSpec(memory_space=pl.ANY),
                pl.BlockSpec((1, d_model, 2 * D_EXPERT), lambda i, be, na: (be[i], 0, 0)),
                pl.BlockSpec((1, D_EXPERT, d_model), lambda i, be, na: (be[i], 0, 0)),
            ],
            out_specs=pl.BlockSpec(memory_space=pl.ANY),
            scratch_shapes=[
                pltpu.VMEM(tiles, jnp.uint32),
                pltpu.VMEM(tiles, jnp.uint32),
                pltpu.SemaphoreType.DMA((2,)),
                pltpu.SemaphoreType.DMA((2,)),
            ],
        ),
        out_shape=jax.ShapeDtypeStruct((2 * rows + 2 * te, SLAB_ROWS, LANES), jnp.uint32),
        compiler_params=_params(1, vmem=MOE_VMEM_LIMIT),
        name="moe",
    )(blk_e, n_act, rows_tok, rows_tok, rows_dst, h1p.reshape(rows, SLAB_ROWS, LANES), w13, w2)
    return y.reshape(-1, LANES)


def _final_kernel(h_ref, y0_ref, y1_ref, rg_ref, g_ref, b_ref, o_ref):
    rg = rg_ref[...]
    ffn = _unpack_halves(_load_slabs(y0_ref)) * rg[:, 0:1] + _unpack_halves(_load_slabs(y1_ref)) * rg[:, 1:2]
    o_ref[...] = _layer_norm(DEEPNORM_ALPHA * h_ref[...] + ffn, g_ref[...], b_ref[...])


def _final(h1, y, rg, g, b, *, tm):
    rows, d_model = h1.shape
    nt = rows // tm
    return pl.pallas_call(
        _final_kernel,
        grid=(nt,),
        in_specs=[
            pl.BlockSpec((tm, d_model), lambda i: (i, 0)),
            pl.BlockSpec((tm * SLAB_ROWS, LANES), lambda i: (i, 0)),
            pl.BlockSpec((tm * SLAB_ROWS, LANES), lambda i: (nt + i, 0)),
            pl.BlockSpec((tm, LANES), lambda i: (i, 0)),
            pl.BlockSpec((1, d_model), lambda i: (0, 0)),
            pl.BlockSpec((1, d_model), lambda i: (0, 0)),
        ],
        out_specs=pl.BlockSpec((tm, d_model), lambda i: (i, 0)),
        out_shape=jax.ShapeDtypeStruct((rows, d_model), F32),
        compiler_params=_params(1),
        name="final",
    )(h1, y, y, rg, g, b)


def _dispatch_tables(ri, cnt, *, rows, te):
    experts = ri[0:2]
    rank = ri[2:4]
    counts = cnt[N_GROUPS:N_GROUPS + N_EXPERTS, 0].astype(jnp.int32)
    padded = ((counts + te - 1) // te) * te
    pend = jnp.cumsum(padded)
    pstart = pend - padded
    eids = jnp.arange(N_EXPERTS, dtype=jnp.int32)
    seg = jnp.sum(jnp.where(experts[..., None] == eids, pstart, 0), axis=-1)
    dest = (seg + rank).reshape(-1)
    nb = (2 * rows + N_EXPERTS * (te - 1) + te - 1) // te
    out_row = jnp.arange(2 * rows, dtype=jnp.int32)
    pos = jnp.arange(nb * te, dtype=jnp.int32)
    rows_dst = (2 * rows + pos % (2 * te)).at[dest].set(out_row)
    rows_tok = jnp.where(rows_dst < rows, rows_dst, jnp.where(rows_dst < 2 * rows, rows_dst - rows, 0))
    blk_e = jnp.minimum(jnp.sum(pend[None, :] <= (jnp.arange(nb, dtype=jnp.int32) * te)[:, None], axis=1),
                        N_EXPERTS - 1).astype(jnp.int32)
    n_act = (pend[-1] // te).astype(jnp.int32).reshape(1)
    return blk_e, n_act, rows_tok.reshape(nb, 1, te), rows_dst.reshape(nb, 1, te)


def kernel(x, meta, ln_in_g, ln_in_b, w_in, b_f, pool_w, pool_scale, w_out, ln_mix_g, ln_mix_b,
           w_router_g, b_router_g, w_router_e, b_router_e, w13, w2, ln_ffn_g, ln_ffn_b):
    batch, seq, d_model = x.shape
    rows = batch * seq
    tile = min(512, seq)
    assert seq % (2 * tile) == 0 and tile % N_META == 0
    assert w_in.shape[0] == 1, "depth-1 trunk"

    x2 = x.reshape(rows, d_model)
    row = lambda v: v.reshape(1, -1).astype(F32)

    wi = w_in[0]
    w_f = jnp.pad(wi[:, 3 * D_FOX:3 * D_FOX + FOX_HEADS], ((0, 0), (0, LANES - FOX_HEADS)))
    w_proj = jnp.concatenate([wi[:, :3 * D_FOX], wi[:, 3 * D_FOX + FOX_HEADS:], w_f], axis=1).astype(BF16)
    bf = jnp.pad(b_f[0], (0, LANES - FOX_HEADS)).reshape(1, LANES)
    gi, bi = row(ln_in_g), row(ln_in_b)

    meta_pad = jnp.pad(meta.astype(F32), ((0, META_PAD - N_META), (0, 0)))
    sel = _bias_placement()
    pw, ps = pool_w[0].astype(BF16), row(pool_scale[0])
    no_halo = jnp.zeros((N_META, D_POOL), F32)
    qk_m, vt_m, u_m, _, c_m, _ = _proj(meta_pad, gi, bi, w_proj, bf, sel, no_halo, pw, ps, tm=META_PAD, tiles_per_seq=1)
    d_meta = (c_m[:N_META, :FOX_HEADS] - c_m[N_META - 1:N_META, :FOX_HEADS]) * LOG2E
    d_meta = jnp.pad(d_meta, ((0, META_PAD - N_META), (0, 0)), constant_values=MASKED_BIAS)
    terms = _bias_terms(-d_meta)
    kaug_m = jnp.zeros((META_PAD, FOX_HEADS, HEAD_DIM), BF16).at[:, :, :BIAS_TERMS].set(terms.transpose(1, 2, 0))
    kaug_m = kaug_m.reshape(META_PAD, D_FOX)

    qk, vt, _, o_pool, _, kaug = _proj(x2, gi, bi, w_proj, bf, sel, u_m[:N_META], pw, ps, tm=tile,
                                       tiles_per_seq=seq // tile)
    o_fox = _attention(qk, kaug, vt, qk_m, kaug_m, vt_m, batch=batch, seq=seq, th=tile, tk=tile)

    w_r = jnp.pad(jnp.concatenate([w_router_g[0], w_router_e[0]], axis=1),
                  ((0, 0), (0, LANES - N_GROUPS - N_EXPERTS)))
    w_r_hi = w_r.astype(BF16)
    w_r = jnp.concatenate([w_r_hi, (w_r - w_r_hi.astype(F32)).astype(BF16)], axis=1)
    b_r = jnp.pad(jnp.concatenate([b_router_g[0], b_router_e[0]]), (0, LANES - N_GROUPS - N_EXPERTS)).reshape(1, LANES)
    h1, h1p, ri, rg, cnt = _mix(x2, o_fox, o_pool, w_out[0].astype(BF16), gi, bi, row(ln_mix_g[0]),
                                row(ln_mix_b[0]), w_r, b_r, tm=tile)

    te = 256
    blk_e, n_act, rows_tok, rows_dst = _dispatch_tables(ri, cnt, rows=rows, te=te)
    y = _moe(blk_e, n_act, rows_tok, rows_dst, h1p, w13[0], w2[0], te=te)

    out = _final(h1, y, rg, row(ln_ffn_g[0]), row(ln_ffn_b[0]), tm=tile)
    return out.reshape(batch, seq, d_model)
```

```python
import functools

import jax
import jax.numpy as jnp
from jax import lax
from jax.experimental import pallas as pl
from jax.experimental.pallas import tpu as pltpu

F32 = jnp.float32
BF16 = jnp.bfloat16

N_META = 16
FOX_HEADS = 8
HEAD_DIM = 128
D_FOX = FOX_HEADS * HEAD_DIM
POOL_WINDOWS = (2, 4, 8, 16)
POOL_GROUP_DIM = 256
D_POOL = len(POOL_WINDOWS) * POOL_GROUP_DIM
N_GROUPS = 4
EXPERTS_PER_GROUP = 8
N_EXPERTS = N_GROUPS * EXPERTS_PER_GROUP
D_EXPERT = 1024
LN_EPS = 1e-5
DEEPNORM_ALPHA = 2.0 ** 0.25

LANES = 128
SLAB_ROWS = 8
META_PAD = 128
MASKED_BIAS = 1e30
LOG2E = 1.4426950408889634
BIAS_TERMS = 3
ROUTE_ROWS = 48
VT_ROWS = 144
VMEM_LIMIT = 56 * 1024 * 1024
MOE_VMEM_LIMIT = 62 * 1024 * 1024


def _params(n_axes, vmem=VMEM_LIMIT):
    return pltpu.CompilerParams(dimension_semantics=("arbitrary",) * n_axes, vmem_limit_bytes=vmem)


def _layer_norm(x, g, b):
    mu = jnp.mean(x, axis=-1, keepdims=True)
    xc = x - mu
    var = jnp.mean(xc * xc, axis=-1, keepdims=True)
    return xc * lax.rsqrt(var + LN_EPS) * g + b


def _pack_halves(x):
    n = x.shape[1] // 2
    bits = lambda v: lax.bitcast_convert_type(v.astype(BF16).astype(F32), jnp.uint32)
    return (bits(x[:, :n]) >> 16) | (bits(x[:, n:]) & jnp.uint32(0xFFFF0000))


def _unpack_halves(w):
    lo = lax.bitcast_convert_type(w << 16, F32)
    hi = lax.bitcast_convert_type(w & jnp.uint32(0xFFFF0000), F32)
    return jnp.concatenate([lo, hi], axis=1)


def _store_slabs(ref, words):
    rows = words.shape[0]
    for k in range(SLAB_ROWS):
        ref[pl.ds(k, rows, stride=SLAB_ROWS), :] = words[:, k * LANES:(k + 1) * LANES]


def _load_slabs(ref):
    rows = ref.shape[0] // SLAB_ROWS
    return jnp.concatenate([ref[pl.ds(k, rows, stride=SLAB_ROWS), :] for k in range(SLAB_ROWS)], axis=1)


def _row_tiles_to_value(ref):
    rt, lt = ref.shape[0], ref.shape[1]
    return jnp.concatenate([ref[:, j].reshape(rt * SLAB_ROWS, LANES) for j in range(lt)], axis=1)


def _value_to_row_tiles(ref, value):
    rt, lt = ref.shape[0], ref.shape[1]
    for j in range(lt):
        ref[:, j] = value[:, j * LANES:(j + 1) * LANES].reshape(rt, SLAB_ROWS, LANES)


def _proj_kernel(x_ref, g_ref, b_ref, w_ref, bf_ref, sel_ref, um_ref, pw_ref, ps_ref,
                 qk_ref, vt_ref, u_ref, op_ref, d_ref, ka_ref, carry_ref, ext_ref, *, tiles_per_seq, chunk):
    i = pl.program_id(0)
    tm = x_ref.shape[0]
    xn = _layer_norm(x_ref[...], g_ref[...], b_ref[...]).astype(BF16)
    first_tile = i % tiles_per_seq == 0
    ext_ref[0:N_META, :] = jnp.where(first_tile, um_ref[...], ext_ref[0:N_META, :])
    for c in range(0, D_POOL, chunk):
        uc = jnp.dot(xn, w_ref[:, 3 * D_FOX + c:3 * D_FOX + c + chunk], preferred_element_type=F32)
        u_ref[:, c:c + chunk] = uc
        ext_ref[N_META:, c:c + chunk] = uc
    for g, w in enumerate(POOL_WINDOWS):
        cols = slice(g * POOL_GROUP_DIM, (g + 1) * POOL_GROUP_DIM)
        tok = ext_ref[N_META:N_META + tm, cols]
        acc = tok
        for j in range(1, w):
            acc = acc + ext_ref[N_META - j:N_META - j + tm, cols]
        pooled = acc * (1.0 / w) - tok
        mixed = jnp.dot(pooled.astype(BF16), pw_ref[g], preferred_element_type=F32)
        op_ref[:, cols] = (mixed * ps_ref[:, cols]).astype(BF16)
    ext_ref[0:N_META, :] = ext_ref[tm:tm + N_META, :]

    scale = HEAD_DIM ** -0.5 * LOG2E
    for c in range(0, 2 * D_FOX, chunk):
        acc = jnp.dot(xn, w_ref[:, c:c + chunk], preferred_element_type=F32)
        if c < D_FOX:
            acc = acc * scale
        qk_ref[:, c:c + chunk] = acc.astype(BF16)
    sub = lax.broadcasted_iota(jnp.int32, (VT_ROWS - HEAD_DIM, tm), 0)
    tail = jnp.where(sub == 0, 1.0, 0.0).astype(BF16)
    for c in range(0, D_FOX, chunk):
        acc = jnp.dot(xn, w_ref[:, 2 * D_FOX + c:2 * D_FOX + c + chunk], preferred_element_type=F32)
        for hd in range(chunk // HEAD_DIM):
            r0 = (c // HEAD_DIM + hd) * VT_ROWS
            vt_ref[r0:r0 + HEAD_DIM, :] = acc[:, hd * HEAD_DIM:(hd + 1) * HEAD_DIM].T.astype(BF16)
            vt_ref[r0 + HEAD_DIM:r0 + VT_ROWS, :] = tail
    fl = jnp.dot(xn, w_ref[:, 3 * D_FOX + D_POOL:], preferred_element_type=F32) + bf_ref[...]
    lf = jnp.minimum(fl, 0.0) - jnp.log1p(jnp.exp(-jnp.abs(fl)))

    row = lax.broadcasted_iota(jnp.int32, lf.shape, 0)
    acc = lf
    k = 1
    while k < tm:
        acc = acc + jnp.where(row >= k, pltpu.roll(acc, k, 0), 0.0)
        k *= 2
    d = acc + jnp.where(first_tile, 0.0, carry_ref[...])
    d_ref[...] = d
    carry_ref[...] = d[tm - 1:tm, :]

    nd = d * (-LOG2E)
    hi = nd.astype(BF16)
    r1 = nd - hi.astype(F32)
    mid = r1.astype(BF16)
    lo = (r1 - mid.astype(F32)).astype(BF16)
    split = jnp.concatenate([hi, mid, lo], axis=1)
    ka_ref[...] = jnp.dot(split, sel_ref[...], preferred_element_type=F32).astype(BF16)


def _bias_placement():
    t, h = jnp.meshgrid(jnp.arange(BIAS_TERMS), jnp.arange(FOX_HEADS), indexing="ij")
    sel = jnp.zeros((BIAS_TERMS * LANES, D_FOX), F32)
    return sel.at[(t * LANES + h).ravel(), (h * HEAD_DIM + t).ravel()].set(1.0).astype(BF16)


def _proj(x2, g, b, w, bf, sel, um, pw, ps, *, tm, tiles_per_seq):
    rows, d_model = x2.shape
    n_proj = w.shape[1]
    kern = functools.partial(_proj_kernel, tiles_per_seq=tiles_per_seq, chunk=512)
    return pl.pallas_call(
        kern,
        grid=(rows // tm,),
        in_specs=[
            pl.BlockSpec((tm, d_model), lambda i: (i, 0)),
            pl.BlockSpec((1, d_model), lambda i: (0, 0)),
            pl.BlockSpec((1, d_model), lambda i: (0, 0)),
            pl.BlockSpec((d_model, n_proj), lambda i: (0, 0)),
            pl.BlockSpec((1, LANES), lambda i: (0, 0)),
            pl.BlockSpec(sel.shape, lambda i: (0, 0)),
            pl.BlockSpec((N_META, D_POOL), lambda i: (0, 0)),
            pl.BlockSpec(pw.shape, lambda i: (0, 0, 0)),
            pl.BlockSpec((1, D_POOL), lambda i: (0, 0)),
        ],
        out_specs=[
            pl.BlockSpec((tm, 2 * D_FOX), lambda i: (i, 0)),
            pl.BlockSpec((FOX_HEADS * VT_ROWS, tm), lambda i: (i // tiles_per_seq, i % tiles_per_seq)),
            pl.BlockSpec((tm, D_POOL), lambda i: (i, 0)),
            pl.BlockSpec((tm, D_POOL), lambda i: (i, 0)),
            pl.BlockSpec((tm, LANES), lambda i: (i, 0)),
            pl.BlockSpec((tm, D_FOX), lambda i: (i, 0)),
        ],
        out_shape=[
            jax.ShapeDtypeStruct((rows, 2 * D_FOX), BF16),
            jax.ShapeDtypeStruct((rows // (tm * tiles_per_seq) * FOX_HEADS * VT_ROWS, tm * tiles_per_seq), BF16),
            jax.ShapeDtypeStruct((rows, D_POOL), F32),
            jax.ShapeDtypeStruct((rows, D_POOL), BF16),
            jax.ShapeDtypeStruct((rows, LANES), F32),
            jax.ShapeDtypeStruct((rows, D_FOX), BF16),
        ],
        scratch_shapes=[pltpu.VMEM((1, LANES), F32), pltpu.VMEM((N_META + tm, D_POOL), F32)],
        compiler_params=_params(1),
        name="proj",
    )(x2, g, b, w, bf, sel, um, pw, ps)


def _attn_kernel(q_ref, k_ref, ka_ref, vt_ref, km_ref, kam_ref, vtm_ref, o_ref, sa_ref, sb_ref, *, th, tk):
    qi = pl.program_id(2)
    nt = (((1,), (1,)), ((), ()))
    lane = lax.broadcasted_iota(jnp.int32, (th, HEAD_DIM), 1)
    ones = jnp.where(lane < BIAS_TERMS, 1.0, 0.0).astype(BF16)
    qa = [jnp.concatenate([q_ref[h * th:(h + 1) * th, :], ones], axis=1) for h in range(2)]
    vrows = vt_ref.shape[0]

    def scores(keys, half):
        return lax.dot_general(keys, qa[half], nt, preferred_element_type=F32)

    def x_keys(j):
        ks = pl.multiple_of(j * tk, tk)
        return jnp.concatenate([k_ref[pl.ds(ks, tk), :], ka_ref[pl.ds(ks, tk), :]], axis=1)

    def update(read_s, vt, carry, masked=False):
        m, acc = carry
        if masked:
            key = lax.broadcasted_iota(jnp.int32, (tk, th), 0)
            qry = lax.broadcasted_iota(jnp.int32, (tk, th), 1)
            read = lambda: jnp.where(key <= qry, read_s(), -jnp.inf)
        else:
            read = read_s
        m_new = jnp.maximum(m, jnp.max(read(), axis=0, keepdims=True))
        a = jnp.exp2(m - m_new)
        p = jnp.exp2(read() - m_new).astype(BF16)
        return m_new, a * acc + jnp.dot(vt, p, preferred_element_type=F32)

    def x_vt(j):
        return vt_ref[:, pl.ds(pl.multiple_of(j * tk, tk), tk)]

    keys_m = jnp.concatenate([km_ref[...], kam_ref[...]], axis=1)
    carry = []
    for h in range(2):
        s_m = scores(keys_m, h)
        init = (jnp.full((1, th), -jnp.inf, F32), jnp.zeros((vrows, th), F32))
        carry.append(update(lambda: s_m, vtm_ref[...], init))

    def fill(buf, j):
        keys = x_keys(j)
        for h in range(2):
            buf[h] = scores(keys, h)

    def drain(buf, j, carry):
        vt = x_vt(j)
        return tuple(update(lambda h=h: buf[h], vt, carry[h]) for h in range(2))

    def body(jj, carry):
        j = 2 * jj
        fill(sb_ref, j + 1)
        carry = drain(sa_ref, j, carry)
        fill(sa_ref, j + 2)
        return drain(sb_ref, j + 1, carry)

    first = 2 * qi
    fill(sa_ref, 0)
    c0, c1 = lax.fori_loop(0, qi, body, tuple(carry))
    s11 = scores(x_keys(first + 1), 1)
    c0 = update(lambda: sa_ref[0], x_vt(first), c0, masked=True)
    c1 = update(lambda: sa_ref[1], x_vt(first), c1)
    c1 = update(lambda: s11, x_vt(first + 1), c1, masked=True)
    for h, (m, acc) in enumerate((c0, c1)):
        out_t = acc[:HEAD_DIM] / acc[HEAD_DIM:HEAD_DIM + 1]
        o_ref[h * th:(h + 1) * th, :] = out_t.T.astype(BF16)


def _attention(qk, kaug, vt, qk_m, kaug_m, vt_m, *, batch, seq, th, tk):
    assert th == tk
    tq = 2 * th
    nq = seq // tq
    vrows = vt.shape[0] // (batch * FOX_HEADS)
    kern = functools.partial(_attn_kernel, th=th, tk=tk)
    return pl.pallas_call(
        kern,
        grid=(batch, FOX_HEADS, nq),
        in_specs=[
            pl.BlockSpec((tq, HEAD_DIM), lambda b, h, i: (b * nq + i, h)),
            pl.BlockSpec((seq, HEAD_DIM), lambda b, h, i: (b, FOX_HEADS + h)),
            pl.BlockSpec((seq, HEAD_DIM), lambda b, h, i: (b, h)),
            pl.BlockSpec((vrows, seq), lambda b, h, i: (b * FOX_HEADS + h, 0)),
            pl.BlockSpec((META_PAD, HEAD_DIM), lambda b, h, i: (0, FOX_HEADS + h)),
            pl.BlockSpec((META_PAD, HEAD_DIM), lambda b, h, i: (0, h)),
            pl.BlockSpec((vrows, META_PAD), lambda b, h, i: (h, 0)),
        ],
        out_specs=pl.BlockSpec((tq, HEAD_DIM), lambda b, h, i: (b * nq + i, h)),
        out_shape=jax.ShapeDtypeStruct((batch * seq, D_FOX), BF16),
        scratch_shapes=[pltpu.VMEM((2, tk, th), F32), pltpu.VMEM((2, tk, th), F32)],
        compiler_params=_params(3),
        name="attn",
    )(qk, qk, kaug, vt, qk_m, kaug_m, vt_m)


def _bias_terms(neg_bias):
    terms, rest = [], neg_bias
    for _ in range(BIAS_TERMS):
        t = rest.astype(BF16)
        terms.append(t)
        rest = rest - t.astype(F32)
    return jnp.stack(terms)


def _mix_kernel(x_ref, of_ref, op_ref, wo_ref, gi_ref, bi_ref, gm_ref, bm_ref, wr_ref, br_ref, tri_ref,
                h_ref, hp_ref, ri_ref, rg_ref, cnt_ref, carry_ref, *, chunk):
    i = pl.program_id(0)
    tm = x_ref.shape[0]
    d_model = x_ref.shape[1]
    h0 = _layer_norm(x_ref[...], gi_ref[...], bi_ref[...])
    of = of_ref[...]
    op = op_ref[...]
    for c in range(0, d_model, chunk):
        mix = jnp.dot(of, wo_ref[0:D_FOX, c:c + chunk], preferred_element_type=F32)
        mix = mix + jnp.dot(op, wo_ref[D_FOX:, c:c + chunk], preferred_element_type=F32)
        h_ref[:, c:c + chunk] = DEEPNORM_ALPHA * h0[:, c:c + chunk] + mix
    h1 = _layer_norm(h_ref[...], gm_ref[...], bm_ref[...])
    h_ref[...] = h1
    _store_slabs(hp_ref, _pack_halves(h1))

    h_hi = h1.astype(BF16)
    h_mid = (h1 - h_hi.astype(F32)).astype(BF16)
    hh = jnp.dot(h_hi, wr_ref[...], preferred_element_type=F32)
    mh = jnp.dot(h_mid, wr_ref[:, :LANES], preferred_element_type=F32)
    logits = hh[:, :LANES] + hh[:, LANES:] + mh + br_ref[...]
    lt = logits.T[:ROUTE_ROWS]
    row = lax.broadcasted_iota(jnp.int32, lt.shape, 0)
    neg = -jnp.inf
    first_min = lambda hit: jnp.min(jnp.where(hit, row, ROUTE_ROWS), axis=0, keepdims=True)
    gl = jnp.where(row < N_GROUPS, lt, neg)
    gmax = jnp.max(gl, axis=0, keepdims=True)
    g_idx = first_min(gl == gmax)
    p_g = 1.0 / jnp.sum(jnp.exp(gl - gmax), axis=0, keepdims=True)
    lo = N_GROUPS + g_idx * EXPERTS_PER_GROUP
    el = jnp.where((row >= lo) & (row < lo + EXPERTS_PER_GROUP), lt, neg)
    v1 = jnp.max(el, axis=0, keepdims=True)
    i1 = first_min(el == v1)
    el2 = jnp.where(row == i1, neg, el)
    v2 = jnp.max(el2, axis=0, keepdims=True)
    i2 = first_min(el2 == v2)
    t = jnp.exp(v2 - v1)
    gate1 = p_g / (1.0 + t)
    gate2 = gate1 * t

    @pl.when(i == 0)
    def _():
        carry_ref[...] = jnp.zeros_like(carry_ref)

    upper = tri_ref[...]
    oh1 = (row == i1).astype(F32)
    oh2 = (row == i2).astype(F32)
    pre1 = jnp.dot(oh1.astype(BF16), upper, preferred_element_type=F32)
    pre2 = jnp.dot(oh2.astype(BF16), upper, preferred_element_type=F32)
    cnt1 = jnp.sum(oh1, axis=1, keepdims=True)
    cnt2 = jnp.sum(oh2, axis=1, keepdims=True)
    base = carry_ref[...]
    rank1 = jnp.sum((pre1 + base) * oh1, axis=0, keepdims=True)
    rank2 = jnp.sum((pre2 + base + cnt1) * oh2, axis=0, keepdims=True)
    total = base + cnt1 + cnt2
    carry_ref[...] = total
    cnt_ref[...] = jnp.broadcast_to(total, cnt_ref.shape)

    r8 = lax.broadcasted_iota(jnp.int32, ri_ref.shape, 0)
    ri_ref[...] = jnp.where(r8 == 0, i1 - N_GROUPS, jnp.where(r8 == 1, i2 - N_GROUPS, jnp.where(
        r8 == 2, rank1.astype(jnp.int32), jnp.where(r8 == 3, rank2.astype(jnp.int32), 0))))
    rl = lax.broadcasted_iota(jnp.int32, (LANES, tm), 0)
    rg_ref[...] = jnp.where(rl == 0, gate1, jnp.where(rl == 1, gate2, 0.0)).T


def _mix(x2, o_fox, o_pool, wo, gi, bi, gm, bm, wr, br, *, tm):
    rows, d_model = x2.shape
    idx = jnp.arange(tm, dtype=jnp.int32)
    tri = (idx[:, None] < idx[None, :]).astype(BF16)
    kern = functools.partial(_mix_kernel, chunk=512)
    row_spec = lambda w: pl.BlockSpec((tm, w), lambda i: (i, 0))
    vec_spec = lambda w: pl.BlockSpec((1, w), lambda i: (0, 0))
    return pl.pallas_call(
        kern,
        grid=(rows // tm,),
        in_specs=[
            row_spec(d_model), row_spec(D_FOX), row_spec(D_POOL),
            pl.BlockSpec(wo.shape, lambda i: (0, 0)),
            vec_spec(d_model), vec_spec(d_model), vec_spec(d_model), vec_spec(d_model),
            pl.BlockSpec(wr.shape, lambda i: (0, 0)),
            vec_spec(LANES),
            pl.BlockSpec((tm, tm), lambda i: (0, 0)),
        ],
        out_specs=[row_spec(d_model), pl.BlockSpec((tm * SLAB_ROWS, LANES), lambda i: (i, 0)),
                   pl.BlockSpec((SLAB_ROWS, tm), lambda i: (0, i)), row_spec(LANES),
                   pl.BlockSpec((ROUTE_ROWS, LANES), lambda i: (0, 0))],
        out_shape=[
            jax.ShapeDtypeStruct((rows, d_model), F32),
            jax.ShapeDtypeStruct((rows * SLAB_ROWS, LANES), jnp.uint32),
            jax.ShapeDtypeStruct((SLAB_ROWS, rows), jnp.int32),
            jax.ShapeDtypeStruct((rows, LANES), F32),
            jax.ShapeDtypeStruct((ROUTE_ROWS, LANES), F32),
        ],
        scratch_shapes=[pltpu.VMEM((ROUTE_ROWS, 1), F32)],
        compiler_params=_params(1),
        name="mix",
    )(x2, o_fox, o_pool, wo, gi, bi, gm, bm, wr, br, tri)


def _moe_kernel(be_ref, na_ref, tok_ref, tokn_ref, dst_ref, h_hbm, w13_ref, w2_ref, y_hbm, xbuf, obuf, gsem, ssem,
                *, chunk):
    i = pl.program_id(0)
    n_act = na_ref[0]
    te = xbuf.shape[1] * SLAB_ROWS
    slot = i % 2

    def row_view(buf, s, r):
        return buf.at[s, r // SLAB_ROWS, :, r % SLAB_ROWS, :]

    def start_gather(idx_ref, s):
        for r in range(te):
            pltpu.make_async_copy(h_hbm.at[idx_ref[0, 0, r]], row_view(xbuf, s, r), gsem.at[s]).start(priority=r % 2)

    def wait_gather(s):
        pltpu.make_async_copy(xbuf.at[1 - s], xbuf.at[s], gsem.at[s]).wait()

    def start_scatter(idx, s):
        for r in range(te):
            pltpu.make_async_copy(row_view(obuf, s, r), y_hbm.at[idx(r)], ssem.at[s]).start(priority=r % 2)

    def wait_scatter(s):
        pltpu.make_async_copy(obuf.at[1 - s], obuf.at[s], ssem.at[s]).wait()

    @pl.when(i < n_act)
    def _():
        @pl.when(i == 0)
        def _():
            start_gather(tok_ref, slot)
            spare = y_hbm.shape[0] - 2 * te
            obuf[...] = jnp.zeros_like(obuf)
            for s in range(2):
                start_scatter(lambda r, s=s: spare + s * te + r, s)
            for s in range(2):
                wait_scatter(s)

        wait_gather(slot)

        @pl.when(i + 1 < n_act)
        def _():
            start_gather(tokn_ref, 1 - slot)

        @pl.when(i >= 2)
        def _():
            wait_scatter(slot)

        x = _unpack_halves(_row_tiles_to_value(xbuf.at[slot])).astype(BF16)
        out = None
        for c in range(0, D_EXPERT, chunk):
            gate = jnp.dot(x, w13_ref[0, :, c:c + chunk].astype(BF16), preferred_element_type=F32)
            up = jnp.dot(x, w13_ref[0, :, D_EXPERT + c:D_EXPERT + c + chunk].astype(BF16),
                         preferred_element_type=F32)
            hid = (gate / (1.0 + jnp.exp(-gate)) * up).astype(BF16)
            part = jnp.dot(hid, w2_ref[0, c:c + chunk, :].astype(BF16), preferred_element_type=F32)
            out = part if out is None else out + part
        _value_to_row_tiles(obuf.at[slot], _pack_halves(out))
        start_scatter(lambda r: dst_ref[0, 0, r], slot)

        @pl.when(i == n_act - 1)
        def _():
            @pl.when(i >= 1)
            def _():
                wait_scatter(1 - slot)
            wait_scatter(slot)


def _moe(blk_e, n_act, rows_tok, rows_dst, h1p, w13, w2, *, te):
    rows = h1p.shape[0] // SLAB_ROWS
    words = SLAB_ROWS * LANES
    d_model = 2 * words
    nb = rows_tok.shape[0]
    kern = functools.partial(_moe_kernel, chunk=1024)
    smem_spec = lambda off: pl.BlockSpec((1, 1, te), lambda i, be, na: (jnp.minimum(i + off, nb - 1), 0, 0),
                                         memory_space=pltpu.SMEM)
    tiles = (2, te // SLAB_ROWS, words // LANES, SLAB_ROWS, LANES)
    y = pl.pallas_call(
        kern,
        grid_spec=pltpu.PrefetchScalarGridSpec(
            num_scalar_prefetch=2,
            grid=(nb,),
            in_specs=[
                smem_spec(0), smem_spec(1), smem_spec(0),
                pl.BlockSpec(memory_space=pl.ANY),
                pl.BlockSpec((1, d_model, 2 * D_EXPERT), lambda i, be, na: (be[i], 0, 0)),
                pl.BlockSpec((1, D_EXPERT, d_model), lambda i, be, na: (be[i], 0, 0)),
            ],
            out_specs=pl.BlockSpec(memory_space=pl.ANY),
            scratch_shapes=[
                pltpu.VMEM(tiles, jnp.uint32),
                pltpu.VMEM(tiles, jnp.uint32),
                pltpu.SemaphoreType.DMA((2,)),
                pltpu.SemaphoreType.DMA((2,)),
            ],
        ),
        out_shape=jax.ShapeDtypeStruct((2 * rows + 2 * te, SLAB_ROWS, LANES), jnp.uint32),
        compiler_params=_params(1, vmem=MOE_VMEM_LIMIT),
        name="moe",
    )(blk_e, n_act, rows_tok, rows_tok, rows_dst, h1p.reshape(rows, SLAB_ROWS, LANES), w13, w2)
    return y.reshape(-1, LANES)


def _final_kernel(h_ref, y0_ref, y1_ref, rg_ref, g_ref, b_ref, o_ref):
    rg = rg_ref[...]
    ffn = _unpack_halves(_load_slabs(y0_ref)) * rg[:, 0:1] + _unpack_halves(_load_slabs(y1_ref)) * rg[:, 1:2]
    o_ref[...] = _layer_norm(DEEPNORM_ALPHA * h_ref[...] + ffn, g_ref[...], b_ref[...])


def _final(h1, y, rg, g, b, *, tm):
    rows, d_model = h1.shape
    nt = rows // tm
    return pl.pallas_call(
        _final_kernel,
        grid=(nt,),
        in_specs=[
            pl.BlockSpec((tm, d_model), lambda i: (i, 0)),
            pl.BlockSpec((tm * SLAB_ROWS, LANES), lambda i: (i, 0)),
            pl.BlockSpec((tm * SLAB_ROWS, LANES), lambda i: (nt + i, 0)),
            pl.BlockSpec((tm, LANES), lambda i: (i, 0)),
            pl.BlockSpec((1, d_model), lambda i: (0, 0)),
            pl.BlockSpec((1, d_model), lambda i: (0, 0)),
        ],
        out_specs=pl.BlockSpec((tm, d_model), lambda i: (i, 0)),
        out_shape=jax.ShapeDtypeStruct((rows, d_model), F32),
        compiler_params=_params(1),
        name="final",
    )(h1, y, y, rg, g, b)


def _dispatch_tables(ri, cnt, *, rows, te):
    experts = ri[0:2]
    rank = ri[2:4]
    counts = cnt[N_GROUPS:N_GROUPS + N_EXPERTS, 0].astype(jnp.int32)
    padded = ((counts + te - 1) // te) * te
    pend = jnp.cumsum(padded)
    pstart = pend - padded
    eids = jnp.arange(N_EXPERTS, dtype=jnp.int32)
    seg = jnp.sum(jnp.where(experts[..., None] == eids, pstart, 0), axis=-1)
    dest = (seg + rank).reshape(-1)
    nb = (2 * rows + N_EXPERTS * (te - 1) + te - 1) // te
    out_row = jnp.arange(2 * rows, dtype=jnp.int32)
    pos = jnp.arange(nb * te, dtype=jnp.int32)
    rows_dst = (2 * rows + pos % (2 * te)).at[dest].set(out_row, unique_indices=True, mode="promise_in_bounds")
    rows_tok = jnp.where(rows_dst < rows, rows_dst, jnp.where(rows_dst < 2 * rows, rows_dst - rows, 0))
    blk_e = jnp.minimum(jnp.sum(pend[None, :] <= (jnp.arange(nb, dtype=jnp.int32) * te)[:, None], axis=1),
                        N_EXPERTS - 1).astype(jnp.int32)
    n_act = (pend[-1] // te).astype(jnp.int32).reshape(1)
    return blk_e, n_act, rows_tok.reshape(nb, 1, te), rows_dst.reshape(nb, 1, te)


def kernel(x, meta, ln_in_g, ln_in_b, w_in, b_f, pool_w, pool_scale, w_out, ln_mix_g, ln_mix_b,
           w_router_g, b_router_g, w_router_e, b_router_e, w13, w2, ln_ffn_g, ln_ffn_b):
    batch, seq, d_model = x.shape
    rows = batch * seq
    tile = min(512, seq)
    assert seq % (2 * tile) == 0 and tile % N_META == 0
    assert w_in.shape[0] == 1, "depth-1 trunk"

    x2 = x.reshape(rows, d_model)
    row = lambda v: v.reshape(1, -1).astype(F32)

    wi = w_in[0]
    w_f = jnp.pad(wi[:, 3 * D_FOX:3 * D_FOX + FOX_HEADS], ((0, 0), (0, LANES - FOX_HEADS)))
    w_proj = jnp.concatenate([wi[:, :3 * D_FOX], wi[:, 3 * D_FOX + FOX_HEADS:], w_f], axis=1).astype(BF16)
    bf = jnp.pad(b_f[0], (0, LANES - FOX_HEADS)).reshape(1, LANES)
    gi, bi = row(ln_in_g), row(ln_in_b)

    meta_pad = jnp.pad(meta.astype(F32), ((0, META_PAD - N_META), (0, 0)))
    sel = _bias_placement()
    pw, ps = pool_w[0].astype(BF16), row(pool_scale[0])
    no_halo = jnp.zeros((N_META, D_POOL), F32)
    qk_m, vt_m, u_m, _, c_m, _ = _proj(meta_pad, gi, bi, w_proj, bf, sel, no_halo, pw, ps, tm=META_PAD, tiles_per_seq=1)
    d_meta = (c_m[:N_META, :FOX_HEADS] - c_m[N_META - 1:N_META, :FOX_HEADS]) * LOG2E
    d_meta = jnp.pad(d_meta, ((0, META_PAD - N_META), (0, 0)), constant_values=MASKED_BIAS)
    terms = _bias_terms(-d_meta)
    kaug_m = jnp.zeros((META_PAD, FOX_HEADS, HEAD_DIM), BF16).at[:, :, :BIAS_TERMS].set(terms.transpose(1, 2, 0))
    kaug_m = kaug_m.reshape(META_PAD, D_FOX)

    qk, vt, _, o_pool, _, kaug = _proj(x2, gi, bi, w_proj, bf, sel, u_m[:N_META], pw, ps, tm=tile,
                                       tiles_per_seq=seq // tile)
    o_fox = _attention(qk, kaug, vt, qk_m, kaug_m, vt_m, batch=batch, seq=seq, th=tile, tk=tile)

    w_r = jnp.pad(jnp.concatenate([w_router_g[0], w_router_e[0]], axis=1),
                  ((0, 0), (0, LANES - N_GROUPS - N_EXPERTS)))
    w_r_hi = w_r.astype(BF16)
    w_r = jnp.concatenate([w_r_hi, (w_r - w_r_hi.astype(F32)).astype(BF16)], axis=1)
    b_r = jnp.pad(jnp.concatenate([b_router_g[0], b_router_e[0]]), (0, LANES - N_GROUPS - N_EXPERTS)).reshape(1, LANES)
    h1, h1p, ri, rg, cnt = _mix(x2, o_fox, o_pool, w_out[0].astype(BF16), gi, bi, row(ln_mix_g[0]),
                                row(ln_mix_b[0]), w_r, b_r, tm=tile)

    te = 256
    blk_e, n_act, rows_tok, rows_dst = _dispatch_tables(ri, cnt, rows=rows, te=te)
    y = _moe(blk_e, n_act, rows_tok, rows_dst, h1p, w13[0], w2[0], te=te)

    out = _final(h1, y, rg, row(ln_ffn_g[0]), row(ln_ffn_b[0]), tm=tile)
    return out.reshape(batch, seq, d_model)
```

```python
import functools

import jax
import jax.numpy as jnp
from jax import lax
from jax.experimental import pallas as pl
from jax.experimental.pallas import tpu as pltpu

F32 = jnp.float32
BF16 = jnp.bfloat16

N_META = 16
FOX_HEADS = 8
HEAD_DIM = 128
D_FOX = FOX_HEADS * HEAD_DIM
POOL_WINDOWS = (2, 4, 8, 16)
POOL_GROUP_DIM = 256
D_POOL = len(POOL_WINDOWS) * POOL_GROUP_DIM
N_GROUPS = 4
EXPERTS_PER_GROUP = 8
N_EXPERTS = N_GROUPS * EXPERTS_PER_GROUP
D_EXPERT = 1024
LN_EPS = 1e-5
DEEPNORM_ALPHA = 2.0 ** 0.25

LANES = 128
SLAB_ROWS = 8
META_PAD = 128
MASKED_BIAS = 1e30
LOG2E = 1.4426950408889634
BIAS_TERMS = 3
ROUTE_ROWS = 48
VT_ROWS = 144
VMEM_LIMIT = 56 * 1024 * 1024
MOE_VMEM_LIMIT = 62 * 1024 * 1024


def _params(n_axes, vmem=VMEM_LIMIT):
    return pltpu.CompilerParams(dimension_semantics=("arbitrary",) * n_axes, vmem_limit_bytes=vmem)


def _layer_norm(x, g, b):
    mu = jnp.mean(x, axis=-1, keepdims=True)
    xc = x - mu
    var = jnp.mean(xc * xc, axis=-1, keepdims=True)
    return xc * lax.rsqrt(var + LN_EPS) * g + b


def _pack_halves(x):
    n = x.shape[1] // 2
    bits = lambda v: lax.bitcast_convert_type(v.astype(BF16).astype(F32), jnp.uint32)
    return (bits(x[:, :n]) >> 16) | (bits(x[:, n:]) & jnp.uint32(0xFFFF0000))


def _unpack_halves(w):
    lo = lax.bitcast_convert_type(w << 16, F32)
    hi = lax.bitcast_convert_type(w & jnp.uint32(0xFFFF0000), F32)
    return jnp.concatenate([lo, hi], axis=1)


def _store_slabs(ref, words):
    rows = words.shape[0]
    for k in range(SLAB_ROWS):
        ref[pl.ds(k, rows, stride=SLAB_ROWS), :] = words[:, k * LANES:(k + 1) * LANES]


def _load_slabs(ref):
    rows = ref.shape[0] // SLAB_ROWS
    return jnp.concatenate([ref[pl.ds(k, rows, stride=SLAB_ROWS), :] for k in range(SLAB_ROWS)], axis=1)


def _row_tiles_to_value(ref):
    rt, lt = ref.shape[0], ref.shape[1]
    return jnp.concatenate([ref[:, j].reshape(rt * SLAB_ROWS, LANES) for j in range(lt)], axis=1)


def _value_to_row_tiles(ref, value):
    rt, lt = ref.shape[0], ref.shape[1]
    for j in range(lt):
        ref[:, j] = value[:, j * LANES:(j + 1) * LANES].reshape(rt, SLAB_ROWS, LANES)


def _proj_kernel(x_ref, g_ref, b_ref, w_ref, bf_ref, sel_ref, um_ref, pw_ref, ps_ref,
                 qk_ref, vt_ref, u_ref, op_ref, d_ref, ka_ref, carry_ref, ext_ref, *, tiles_per_seq, chunk):
    i = pl.program_id(0)
    tm = x_ref.shape[0]
    xn = _layer_norm(x_ref[...], g_ref[...], b_ref[...]).astype(BF16)
    first_tile = i % tiles_per_seq == 0
    ext_ref[0:N_META, :] = jnp.where(first_tile, um_ref[...], ext_ref[0:N_META, :])
    for c in range(0, D_POOL, chunk):
        uc = jnp.dot(xn, w_ref[:, 3 * D_FOX + c:3 * D_FOX + c + chunk], preferred_element_type=F32)
        u_ref[:, c:c + chunk] = uc
        ext_ref[N_META:, c:c + chunk] = uc
    for g, w in enumerate(POOL_WINDOWS):
        cols = slice(g * POOL_GROUP_DIM, (g + 1) * POOL_GROUP_DIM)
        tok = ext_ref[N_META:N_META + tm, cols]
        acc = tok
        for j in range(1, w):
            acc = acc + ext_ref[N_META - j:N_META - j + tm, cols]
        pooled = acc * (1.0 / w) - tok
        mixed = jnp.dot(pooled.astype(BF16), pw_ref[g], preferred_element_type=F32)
        op_ref[:, cols] = (mixed * ps_ref[:, cols]).astype(BF16)
    ext_ref[0:N_META, :] = ext_ref[tm:tm + N_META, :]

    scale = HEAD_DIM ** -0.5 * LOG2E
    for c in range(0, 2 * D_FOX, chunk):
        acc = jnp.dot(xn, w_ref[:, c:c + chunk], preferred_element_type=F32)
        if c < D_FOX:
            acc = acc * scale
        qk_ref[:, c:c + chunk] = acc.astype(BF16)
    sub = lax.broadcasted_iota(jnp.int32, (VT_ROWS - HEAD_DIM, tm), 0)
    tail = jnp.where(sub == 0, 1.0, 0.0).astype(BF16)
    for c in range(0, D_FOX, chunk):
        acc = jnp.dot(xn, w_ref[:, 2 * D_FOX + c:2 * D_FOX + c + chunk], preferred_element_type=F32)
        for hd in range(chunk // HEAD_DIM):
            r0 = (c // HEAD_DIM + hd) * VT_ROWS
            vt_ref[r0:r0 + HEAD_DIM, :] = acc[:, hd * HEAD_DIM:(hd + 1) * HEAD_DIM].T.astype(BF16)
            vt_ref[r0 + HEAD_DIM:r0 + VT_ROWS, :] = tail
    fl = jnp.dot(xn, w_ref[:, 3 * D_FOX + D_POOL:], preferred_element_type=F32) + bf_ref[...]
    lf = jnp.minimum(fl, 0.0) - jnp.log1p(jnp.exp(-jnp.abs(fl)))

    row = lax.broadcasted_iota(jnp.int32, lf.shape, 0)
    acc = lf
    k = 1
    while k < tm:
        acc = acc + jnp.where(row >= k, pltpu.roll(acc, k, 0), 0.0)
        k *= 2
    d = acc + jnp.where(first_tile, 0.0, carry_ref[...])
    d_ref[...] = d
    carry_ref[...] = d[tm - 1:tm, :]

    nd = d * (-LOG2E)
    hi = nd.astype(BF16)
    r1 = nd - hi.astype(F32)
    mid = r1.astype(BF16)
    lo = (r1 - mid.astype(F32)).astype(BF16)
    split = jnp.concatenate([hi, mid, lo], axis=1)
    ka_ref[...] = jnp.dot(split, sel_ref[...], preferred_element_type=F32).astype(BF16)


def _bias_placement():
    t, h = jnp.meshgrid(jnp.arange(BIAS_TERMS), jnp.arange(FOX_HEADS), indexing="ij")
    sel = jnp.zeros((BIAS_TERMS * LANES, D_FOX), F32)
    return sel.at[(t * LANES + h).ravel(), (h * HEAD_DIM + t).ravel()].set(1.0).astype(BF16)


def _proj(x2, g, b, w, bf, sel, um, pw, ps, *, tm, tiles_per_seq):
    rows, d_model = x2.shape
    n_proj = w.shape[1]
    kern = functools.partial(_proj_kernel, tiles_per_seq=tiles_per_seq, chunk=512)
    return pl.pallas_call(
        kern,
        grid=(rows // tm,),
        in_specs=[
            pl.BlockSpec((tm, d_model), lambda i: (i, 0)),
            pl.BlockSpec((1, d_model), lambda i: (0, 0)),
            pl.BlockSpec((1, d_model), lambda i: (0, 0)),
            pl.BlockSpec((d_model, n_proj), lambda i: (0, 0)),
            pl.BlockSpec((1, LANES), lambda i: (0, 0)),
            pl.BlockSpec(sel.shape, lambda i: (0, 0)),
            pl.BlockSpec((N_META, D_POOL), lambda i: (0, 0)),
            pl.BlockSpec(pw.shape, lambda i: (0, 0, 0)),
            pl.BlockSpec((1, D_POOL), lambda i: (0, 0)),
        ],
        out_specs=[
            pl.BlockSpec((tm, 2 * D_FOX), lambda i: (i, 0)),
            pl.BlockSpec((FOX_HEADS * VT_ROWS, tm), lambda i: (i // tiles_per_seq, i % tiles_per_seq)),
            pl.BlockSpec((tm, D_POOL), lambda i: (i, 0)),
            pl.BlockSpec((tm, D_POOL), lambda i: (i, 0)),
            pl.BlockSpec((tm, LANES), lambda i: (i, 0)),
            pl.BlockSpec((tm, D_FOX), lambda i: (i, 0)),
        ],
        out_shape=[
            jax.ShapeDtypeStruct((rows, 2 * D_FOX), BF16),
            jax.ShapeDtypeStruct((rows // (tm * tiles_per_seq) * FOX_HEADS * VT_ROWS, tm * tiles_per_seq), BF16),
            jax.ShapeDtypeStruct((rows, D_POOL), F32),
            jax.ShapeDtypeStruct((rows, D_POOL), BF16),
            jax.ShapeDtypeStruct((rows, LANES), F32),
            jax.ShapeDtypeStruct((rows, D_FOX), BF16),
        ],
        scratch_shapes=[pltpu.VMEM((1, LANES), F32), pltpu.VMEM((N_META + tm, D_POOL), F32)],
        compiler_params=_params(1),
        name="proj",
    )(x2, g, b, w, bf, sel, um, pw, ps)


def _attn_kernel(q_ref, k_ref, ka_ref, vt_ref, km_ref, kam_ref, vtm_ref, o_ref, sa_ref, sb_ref, *, th, tk):
    qi = pl.program_id(2)
    nt = (((1,), (1,)), ((), ()))
    lane = lax.broadcasted_iota(jnp.int32, (th, HEAD_DIM), 1)
    ones = jnp.where(lane < BIAS_TERMS, 1.0, 0.0).astype(BF16)
    qa = [jnp.concatenate([q_ref[h * th:(h + 1) * th, :], ones], axis=1) for h in range(2)]
    vrows = vt_ref.shape[0]

    def scores(keys, half):
        return lax.dot_general(keys, qa[half], nt, preferred_element_type=F32)

    def x_keys(j):
        ks = pl.multiple_of(j * tk, tk)
        return jnp.concatenate([k_ref[pl.ds(ks, tk), :], ka_ref[pl.ds(ks, tk), :]], axis=1)

    def update(read_s, vt, carry, masked=False):
        m, acc = carry
        if masked:
            key = lax.broadcasted_iota(jnp.int32, (tk, th), 0)
            qry = lax.broadcasted_iota(jnp.int32, (tk, th), 1)
            read = lambda: jnp.where(key <= qry, read_s(), -jnp.inf)
        else:
            read = read_s
        m_new = jnp.maximum(m, jnp.max(read(), axis=0, keepdims=True))
        a = jnp.exp2(m - m_new)
        p = jnp.exp2(read() - m_new).astype(BF16)
        return m_new, a * acc + jnp.dot(vt, p, preferred_element_type=F32)

    def x_vt(j):
        return vt_ref[:, pl.ds(pl.multiple_of(j * tk, tk), tk)]

    keys_m = jnp.concatenate([km_ref[...], kam_ref[...]], axis=1)
    carry = []
    for h in range(2):
        s_m = scores(keys_m, h)
        init = (jnp.full((1, th), -jnp.inf, F32), jnp.zeros((vrows, th), F32))
        carry.append(update(lambda: s_m, vtm_ref[...], init))

    def fill(buf, j):
        keys = x_keys(j)
        for h in range(2):
            buf[h] = scores(keys, h)

    def drain(buf, j, carry):
        vt = x_vt(j)
        return tuple(update(lambda h=h: buf[h], vt, carry[h]) for h in range(2))

    def body(jj, carry):
        j = 2 * jj
        fill(sb_ref, j + 1)
        carry = drain(sa_ref, j, carry)
        fill(sa_ref, j + 2)
        return drain(sb_ref, j + 1, carry)

    first = 2 * qi
    fill(sa_ref, 0)
    c0, c1 = lax.fori_loop(0, qi, body, tuple(carry))
    s11 = scores(x_keys(first + 1), 1)
    c0 = update(lambda: sa_ref[0], x_vt(first), c0, masked=True)
    c1 = update(lambda: sa_ref[1], x_vt(first), c1)
    c1 = update(lambda: s11, x_vt(first + 1), c1, masked=True)
    for h, (m, acc) in enumerate((c0, c1)):
        out_t = acc[:HEAD_DIM] / acc[HEAD_DIM:HEAD_DIM + 1]
        o_ref[h * th:(h + 1) * th, :] = out_t.T.astype(BF16)


def _attention(qk, kaug, vt, qk_m, kaug_m, vt_m, *, batch, seq, th, tk):
    assert th == tk
    tq = 2 * th
    nq = seq // tq
    vrows = vt.shape[0] // (batch * FOX_HEADS)
    kern = functools.partial(_attn_kernel, th=th, tk=tk)
    return pl.pallas_call(
        kern,
        grid=(batch, FOX_HEADS, nq),
        in_specs=[
            pl.BlockSpec((tq, HEAD_DIM), lambda b, h, i: (b * nq + i, h)),
            pl.BlockSpec((seq, HEAD_DIM), lambda b, h, i: (b, FOX_HEADS + h)),
            pl.BlockSpec((seq, HEAD_DIM), lambda b, h, i: (b, h)),
            pl.BlockSpec((vrows, seq), lambda b, h, i: (b * FOX_HEADS + h, 0)),
            pl.BlockSpec((META_PAD, HEAD_DIM), lambda b, h, i: (0, FOX_HEADS + h)),
            pl.BlockSpec((META_PAD, HEAD_DIM), lambda b, h, i: (0, h)),
            pl.BlockSpec((vrows, META_PAD), lambda b, h, i: (h, 0)),
        ],
        out_specs=pl.BlockSpec((tq, HEAD_DIM), lambda b, h, i: (b * nq + i, h)),
        out_shape=jax.ShapeDtypeStruct((batch * seq, D_FOX), BF16),
        scratch_shapes=[pltpu.VMEM((2, tk, th), F32), pltpu.VMEM((2, tk, th), F32)],
        compiler_params=_params(3),
        name="attn",
    )(qk, qk, kaug, vt, qk_m, kaug_m, vt_m)


def _bias_terms(neg_bias):
    terms, rest = [], neg_bias
    for _ in range(BIAS_TERMS):
        t = rest.astype(BF16)
        terms.append(t)
        rest = rest - t.astype(F32)
    return jnp.stack(terms)


def _mix_kernel(x_ref, of_ref, op_ref, wo_ref, gi_ref, bi_ref, gm_ref, bm_ref, wr_ref, br_ref, tri_ref,
                h_ref, hp_ref, ri_ref, rg_ref, cnt_ref, carry_ref, *, chunk):
    i = pl.program_id(0)
    tm = x_ref.shape[0]
    d_model = x_ref.shape[1]
    h0 = _layer_norm(x_ref[...], gi_ref[...], bi_ref[...])
    of = of_ref[...]
    op = op_ref[...]
    for c in range(0, d_model, chunk):
        mix = jnp.dot(of, wo_ref[0:D_FOX, c:c + chunk], preferred_element_type=F32)
        mix = mix + jnp.dot(op, wo_ref[D_FOX:, c:c + chunk], preferred_element_type=F32)
        h_ref[:, c:c + chunk] = DEEPNORM_ALPHA * h0[:, c:c + chunk] + mix
    h1 = _layer_norm(h_ref[...], gm_ref[...], bm_ref[...])
    h_ref[...] = h1
    _store_slabs(hp_ref, _pack_halves(h1))

    h_hi = h1.astype(BF16)
    h_mid = (h1 - h_hi.astype(F32)).astype(BF16)
    hh = jnp.dot(h_hi, wr_ref[...], preferred_element_type=F32)
    mh = jnp.dot(h_mid, wr_ref[:, :LANES], preferred_element_type=F32)
    logits = hh[:, :LANES] + hh[:, LANES:] + mh + br_ref[...]
    lt = logits.T[:ROUTE_ROWS]
    row = lax.broadcasted_iota(jnp.int32, lt.shape, 0)
    neg = -jnp.inf
    first_min = lambda hit: jnp.min(jnp.where(hit, row, ROUTE_ROWS), axis=0, keepdims=True)
    gl = jnp.where(row < N_GROUPS, lt, neg)
    gmax = jnp.max(gl, axis=0, keepdims=True)
    g_idx = first_min(gl == gmax)
    p_g = 1.0 / jnp.sum(jnp.exp(gl - gmax), axis=0, keepdims=True)
    lo = N_GROUPS + g_idx * EXPERTS_PER_GROUP
    el = jnp.where((row >= lo) & (row < lo + EXPERTS_PER_GROUP), lt, neg)
    v1 = jnp.max(el, axis=0, keepdims=True)
    i1 = first_min(el == v1)
    el2 = jnp.where(row == i1, neg, el)
    v2 = jnp.max(el2, axis=0, keepdims=True)
    i2 = first_min(el2 == v2)
    t = jnp.exp(v2 - v1)
    gate1 = p_g / (1.0 + t)
    gate2 = gate1 * t

    @pl.when(i == 0)
    def _():
        carry_ref[...] = jnp.zeros_like(carry_ref)

    upper = tri_ref[...]
    oh1 = (row == i1).astype(F32)
    oh2 = (row == i2).astype(F32)
    pre1 = jnp.dot(oh1.astype(BF16), upper, preferred_element_type=F32)
    pre2 = jnp.dot(oh2.astype(BF16), upper, preferred_element_type=F32)
    cnt1 = jnp.sum(oh1, axis=1, keepdims=True)
    cnt2 = jnp.sum(oh2, axis=1, keepdims=True)
    base = carry_ref[...]
    rank1 = jnp.sum((pre1 + base) * oh1, axis=0, keepdims=True)
    rank2 = jnp.sum((pre2 + base + cnt1) * oh2, axis=0, keepdims=True)
    total = base + cnt1 + cnt2
    carry_ref[...] = total
    cnt_ref[...] = jnp.broadcast_to(total, cnt_ref.shape)

    r8 = lax.broadcasted_iota(jnp.int32, ri_ref.shape, 0)
    ri_ref[...] = jnp.where(r8 == 0, i1 - N_GROUPS, jnp.where(r8 == 1, i2 - N_GROUPS, jnp.where(
        r8 == 2, rank1.astype(jnp.int32), jnp.where(r8 == 3, rank2.astype(jnp.int32), 0))))
    rl = lax.broadcasted_iota(jnp.int32, (LANES, tm), 0)
    rg_ref[...] = jnp.where(rl == 0, gate1, jnp.where(rl == 1, gate2, 0.0)).T


def _mix(x2, o_fox, o_pool, wo, gi, bi, gm, bm, wr, br, *, tm):
    rows, d_model = x2.shape
    idx = jnp.arange(tm, dtype=jnp.int32)
    tri = (idx[:, None] < idx[None, :]).astype(BF16)
    kern = functools.partial(_mix_kernel, chunk=512)
    row_spec = lambda w: pl.BlockSpec((tm, w), lambda i: (i, 0))
    vec_spec = lambda w: pl.BlockSpec((1, w), lambda i: (0, 0))
    return pl.pallas_call(
        kern,
        grid=(rows // tm,),
        in_specs=[
            row_spec(d_model), row_spec(D_FOX), row_spec(D_POOL),
            pl.BlockSpec(wo.shape, lambda i: (0, 0)),
            vec_spec(d_model), vec_spec(d_model), vec_spec(d_model), vec_spec(d_model),
            pl.BlockSpec(wr.shape, lambda i: (0, 0)),
            vec_spec(LANES),
            pl.BlockSpec((tm, tm), lambda i: (0, 0)),
        ],
        out_specs=[row_spec(d_model), pl.BlockSpec((tm * SLAB_ROWS, LANES), lambda i: (i, 0)),
                   pl.BlockSpec((SLAB_ROWS, tm), lambda i: (0, i)), row_spec(LANES),
                   pl.BlockSpec((ROUTE_ROWS, LANES), lambda i: (0, 0))],
        out_shape=[
            jax.ShapeDtypeStruct((rows, d_model), F32),
            jax.ShapeDtypeStruct((rows * SLAB_ROWS, LANES), jnp.uint32),
            jax.ShapeDtypeStruct((SLAB_ROWS, rows), jnp.int32),
            jax.ShapeDtypeStruct((rows, LANES), F32),
            jax.ShapeDtypeStruct((ROUTE_ROWS, LANES), F32),
        ],
        scratch_shapes=[pltpu.VMEM((ROUTE_ROWS, 1), F32)],
        compiler_params=_params(1),
        name="mix",
    )(x2, o_fox, o_pool, wo, gi, bi, gm, bm, wr, br, tri)


def _moe_kernel(be_ref, na_ref, tok_ref, tokn_ref, dst_ref, h_hbm, w13_ref, w2_ref, y_hbm, xbuf, obuf, gsem, ssem,
                *, chunk):
    i = pl.program_id(0)
    n_act = na_ref[0]
    te = xbuf.shape[1] * SLAB_ROWS
    slot = i % 2

    def row_view(buf, s, r):
        return buf.at[s, r // SLAB_ROWS, :, r % SLAB_ROWS, :]

    def start_gather(idx_ref, s):
        for r in range(te):
            pltpu.make_async_copy(h_hbm.at[idx_ref[0, 0, r]], row_view(xbuf, s, r), gsem.at[s]).start(priority=r % 2)

    def wait_gather(s):
        pltpu.make_async_copy(xbuf.at[1 - s], xbuf.at[s], gsem.at[s]).wait()

    def start_scatter(idx, s):
        for r in range(te):
            pltpu.make_async_copy(row_view(obuf, s, r), y_hbm.at[idx(r)], ssem.at[s]).start(priority=r % 2)

    def wait_scatter(s):
        pltpu.make_async_copy(obuf.at[1 - s], obuf.at[s], ssem.at[s]).wait()

    @pl.when(i < n_act)
    def _():
        @pl.when(i == 0)
        def _():
            start_gather(tok_ref, slot)
            spare = y_hbm.shape[0] - 2 * te
            obuf[...] = jnp.zeros_like(obuf)
            for s in range(2):
                start_scatter(lambda r, s=s: spare + s * te + r, s)
            for s in range(2):
                wait_scatter(s)

        wait_gather(slot)

        @pl.when(i + 1 < n_act)
        def _():
            start_gather(tokn_ref, 1 - slot)

        @pl.when(i >= 2)
        def _():
            wait_scatter(slot)

        x = _unpack_halves(_row_tiles_to_value(xbuf.at[slot])).astype(BF16)
        out = None
        for c in range(0, D_EXPERT, chunk):
            gate = jnp.dot(x, w13_ref[0, :, c:c + chunk].astype(BF16), preferred_element_type=F32)
            up = jnp.dot(x, w13_ref[0, :, D_EXPERT + c:D_EXPERT + c + chunk].astype(BF16),
                         preferred_element_type=F32)
            hid = (gate / (1.0 + jnp.exp(-gate)) * up).astype(BF16)
            part = jnp.dot(hid, w2_ref[0, c:c + chunk, :].astype(BF16), preferred_element_type=F32)
            out = part if out is None else out + part
        _value_to_row_tiles(obuf.at[slot], _pack_halves(out))
        start_scatter(lambda r: dst_ref[0, 0, r], slot)

        @pl.when(i == n_act - 1)
        def _():
            @pl.when(i >= 1)
            def _():
                wait_scatter(1 - slot)
            wait_scatter(slot)


def _moe(blk_e, n_act, rows_tok, rows_dst, h1p, w13, w2, *, te):
    rows = h1p.shape[0] // SLAB_ROWS
    words = SLAB_ROWS * LANES
    d_model = 2 * words
    nb = rows_tok.shape[0]
    kern = functools.partial(_moe_kernel, chunk=1024)
    smem_spec = lambda off: pl.BlockSpec((1, 1, te), lambda i, be, na: (jnp.minimum(i + off, nb - 1), 0, 0),
                                         memory_space=pltpu.SMEM)
    tiles = (2, te // SLAB_ROWS, words // LANES, SLAB_ROWS, LANES)
    y = pl.pallas_call(
        kern,
        grid_spec=pltpu.PrefetchScalarGridSpec(
            num_scalar_prefetch=2,
            grid=(nb,),
            in_specs=[
                smem_spec(0), smem_spec(1), smem_spec(0),
                pl.BlockSpec(memory_space=pl.ANY),
                pl.BlockSpec((1, d_model, 2 * D_EXPERT), lambda i, be, na: (be[i], 0, 0)),
                pl.BlockSpec((1, D_EXPERT, d_model), lambda i, be, na: (be[i], 0, 0)),
            ],
            out_specs=pl.BlockSpec(memory_space=pl.ANY),
            scratch_shapes=[
                pltpu.VMEM(tiles, jnp.uint32),
                pltpu.VMEM(tiles, jnp.uint32),
                pltpu.SemaphoreType.DMA((2,)),
                pltpu.SemaphoreType.DMA((2,)),
            ],
        ),
        out_shape=jax.ShapeDtypeStruct((2 * rows + 2 * te, SLAB_ROWS, LANES), jnp.uint32),
        compiler_params=_params(1, vmem=MOE_VMEM_LIMIT),
        name="moe",
    )(blk_e, n_act, rows_tok, rows_tok, rows_dst, h1p.reshape(rows, SLAB_ROWS, LANES), w13, w2)
    return y.reshape(-1, LANES)


def _final_kernel(h_ref, y0_ref, y1_ref, rg_ref, g_ref, b_ref, o_ref):
    rg = rg_ref[...]
    ffn = _unpack_halves(_load_slabs(y0_ref)) * rg[:, 0:1] + _unpack_halves(_load_slabs(y1_ref)) * rg[:, 1:2]
    o_ref[...] = _layer_norm(DEEPNORM_ALPHA * h_ref[...] + ffn, g_ref[...], b_ref[...])


def _final(h1, y, rg, g, b, *, tm):
    rows, d_model = h1.shape
    nt = rows // tm
    return pl.pallas_call(
        _final_kernel,
        grid=(nt,),
        in_specs=[
            pl.BlockSpec((tm, d_model), lambda i: (i, 0)),
            pl.BlockSpec((tm * SLAB_ROWS, LANES), lambda i: (i, 0)),
            pl.BlockSpec((tm * SLAB_ROWS, LANES), lambda i: (nt + i, 0)),
            pl.BlockSpec((tm, LANES), lambda i: (i, 0)),
            pl.BlockSpec((1, d_model), lambda i: (0, 0)),
            pl.BlockSpec((1, d_model), lambda i: (0, 0)),
        ],
        out_specs=pl.BlockSpec((tm, d_model), lambda i: (i, 0)),
        out_shape=jax.ShapeDtypeStruct((rows, d_model), F32),
        compiler_params=_params(1),
        name="final",
    )(h1, y, y, rg, g, b)


def _dispatch_tables(ri, cnt, *, rows, te):
    experts = ri[0:2]
    rank = ri[2:4]
    counts = cnt[N_GROUPS:N_GROUPS + N_EXPERTS, 0].astype(jnp.int32)
    padded = ((counts + te - 1) // te) * te
    pend = jnp.cumsum(padded)
    pstart = pend - padded
    eids = jnp.arange(N_EXPERTS, dtype=jnp.int32)
    seg = jnp.sum(jnp.where(experts[..., None] == eids, pstart, 0), axis=-1)
    dest = (seg + rank).reshape(-1)
    nb = (2 * rows + N_EXPERTS * (te - 1) + te - 1) // te
    out_row = jnp.arange(2 * rows, dtype=jnp.int32)
    pos = jnp.arange(nb * te, dtype=jnp.int32)
    rows_dst = (2 * rows + pos % (2 * te)).at[dest].set(out_row)
    rows_tok = jnp.where(rows_dst < rows, rows_dst, jnp.where(rows_dst < 2 * rows, rows_dst - rows, 0))
    blk_e = jnp.minimum(jnp.sum(pend[None, :] <= (jnp.arange(nb, dtype=jnp.int32) * te)[:, None], axis=1),
                        N_EXPERTS - 1).astype(jnp.int32)
    n_act = (pend[-1] // te).astype(jnp.int32).reshape(1)
    return blk_e, n_act, rows_tok.reshape(nb, 1, te), rows_dst.reshape(nb, 1, te)


def kernel(x, meta, ln_in_g, ln_in_b, w_in, b_f, pool_w, pool_scale, w_out, ln_mix_g, ln_mix_b,
           w_router_g, b_router_g, w_router_e, b_router_e, w13, w2, ln_ffn_g, ln_ffn_b):
    batch, seq, d_model = x.shape
    rows = batch * seq
    tile = min(512, seq)
    attn_tile = min(2 * tile, seq // 2)
    assert seq % (2 * attn_tile) == 0 and seq % tile == 0 and tile % N_META == 0
    assert w_in.shape[0] == 1, "depth-1 trunk"

    x2 = x.reshape(rows, d_model)
    row = lambda v: v.reshape(1, -1).astype(F32)

    wi = w_in[0]
    w_f = jnp.pad(wi[:, 3 * D_FOX:3 * D_FOX + FOX_HEADS], ((0, 0), (0, LANES - FOX_HEADS)))
    w_proj = jnp.concatenate([wi[:, :3 * D_FOX], wi[:, 3 * D_FOX + FOX_HEADS:], w_f], axis=1).astype(BF16)
    bf = jnp.pad(b_f[0], (0, LANES - FOX_HEADS)).reshape(1, LANES)
    gi, bi = row(ln_in_g), row(ln_in_b)

    meta_pad = jnp.pad(meta.astype(F32), ((0, META_PAD - N_META), (0, 0)))
    sel = _bias_placement()
    pw, ps = pool_w[0].astype(BF16), row(pool_scale[0])
    no_halo = jnp.zeros((N_META, D_POOL), F32)
    qk_m, vt_m, u_m, _, c_m, _ = _proj(meta_pad, gi, bi, w_proj, bf, sel, no_halo, pw, ps, tm=META_PAD, tiles_per_seq=1)
    d_meta = (c_m[:N_META, :FOX_HEADS] - c_m[N_META - 1:N_META, :FOX_HEADS]) * LOG2E
    d_meta = jnp.pad(d_meta, ((0, META_PAD - N_META), (0, 0)), constant_values=MASKED_BIAS)
    terms = _bias_terms(-d_meta)
    kaug_m = jnp.zeros((META_PAD, FOX_HEADS, HEAD_DIM), BF16).at[:, :, :BIAS_TERMS].set(terms.transpose(1, 2, 0))
    kaug_m = kaug_m.reshape(META_PAD, D_FOX)

    qk, vt, _, o_pool, _, kaug = _proj(x2, gi, bi, w_proj, bf, sel, u_m[:N_META], pw, ps, tm=tile,
                                       tiles_per_seq=seq // tile)
    o_fox = _attention(qk, kaug, vt, qk_m, kaug_m, vt_m, batch=batch, seq=seq, th=attn_tile, tk=attn_tile)

    w_r = jnp.pad(jnp.concatenate([w_router_g[0], w_router_e[0]], axis=1),
                  ((0, 0), (0, LANES - N_GROUPS - N_EXPERTS)))
    w_r_hi = w_r.astype(BF16)
    w_r = jnp.concatenate([w_r_hi, (w_r - w_r_hi.astype(F32)).astype(BF16)], axis=1)
    b_r = jnp.pad(jnp.concatenate([b_router_g[0], b_router_e[0]]), (0, LANES - N_GROUPS - N_EXPERTS)).reshape(1, LANES)
    h1, h1p, ri, rg, cnt = _mix(x2, o_fox, o_pool, w_out[0].astype(BF16), gi, bi, row(ln_mix_g[0]),
                                row(ln_mix_b[0]), w_r, b_r, tm=tile)

    te = 256
    blk_e, n_act, rows_tok, rows_dst = _dispatch_tables(ri, cnt, rows=rows, te=te)
    y = _moe(blk_e, n_act, rows_tok, rows_dst, h1p, w13[0], w2[0], te=te)

    out = _final(h1, y, rg, row(ln_ffn_g[0]), row(ln_ffn_b[0]), tm=tile)
    return out.reshape(batch, seq, d_model)
```

```python
import functools

import jax
import jax.numpy as jnp
from jax import lax
from jax.experimental import pallas as pl
from jax.experimental.pallas import tpu as pltpu

F32 = jnp.float32
BF16 = jnp.bfloat16

N_META = 16
FOX_HEADS = 8
HEAD_DIM = 128
D_FOX = FOX_HEADS * HEAD_DIM
POOL_WINDOWS = (2, 4, 8, 16)
POOL_GROUP_DIM = 256
D_POOL = len(POOL_WINDOWS) * POOL_GROUP_DIM
N_GROUPS = 4
EXPERTS_PER_GROUP = 8
N_EXPERTS = N_GROUPS * EXPERTS_PER_GROUP
D_EXPERT = 1024
LN_EPS = 1e-5
DEEPNORM_ALPHA = 2.0 ** 0.25

LANES = 128
SLAB_ROWS = 8
META_PAD = 128
MASKED_BIAS = 1e30
LOG2E = 1.4426950408889634
BIAS_TERMS = 3
ROUTE_ROWS = 48
VT_ROWS = 144
ROW_TILE = 512
ATTN_TILE = 1024
EXPERT_BLOCK_ROWS = 256
MATMUL_CHUNK = 512
VMEM_LIMIT = 56 * 1024 * 1024
MOE_VMEM_LIMIT = 62 * 1024 * 1024


def _params(n_axes, vmem=VMEM_LIMIT):
    return pltpu.CompilerParams(dimension_semantics=("arbitrary",) * n_axes, vmem_limit_bytes=vmem)


def _layer_norm(x, g, b):
    mu = jnp.mean(x, axis=-1, keepdims=True)
    xc = x - mu
    var = jnp.mean(xc * xc, axis=-1, keepdims=True)
    return xc * lax.rsqrt(var + LN_EPS) * g + b


def _pack_halves(x):
    n = x.shape[1] // 2
    bits = lambda v: lax.bitcast_convert_type(v.astype(BF16).astype(F32), jnp.uint32)
    return (bits(x[:, :n]) >> 16) | (bits(x[:, n:]) & jnp.uint32(0xFFFF0000))


def _unpack_halves(w):
    lo = lax.bitcast_convert_type(w << 16, F32)
    hi = lax.bitcast_convert_type(w & jnp.uint32(0xFFFF0000), F32)
    return jnp.concatenate([lo, hi], axis=1)


def _store_slabs(ref, words):
    rows = words.shape[0]
    for k in range(SLAB_ROWS):
        ref[pl.ds(k, rows, stride=SLAB_ROWS), :] = words[:, k * LANES:(k + 1) * LANES]


def _load_slabs(ref):
    rows = ref.shape[0] // SLAB_ROWS
    return jnp.concatenate([ref[pl.ds(k, rows, stride=SLAB_ROWS), :] for k in range(SLAB_ROWS)], axis=1)


def _row_tiles_to_value(ref):
    rt, lt = ref.shape[0], ref.shape[1]
    return jnp.concatenate([ref[:, j].reshape(rt * SLAB_ROWS, LANES) for j in range(lt)], axis=1)


def _value_to_row_tiles(ref, value):
    rt, lt = ref.shape[0], ref.shape[1]
    for j in range(lt):
        ref[:, j] = value[:, j * LANES:(j + 1) * LANES].reshape(rt, SLAB_ROWS, LANES)


def _proj_kernel(x_ref, g_ref, b_ref, w_ref, bf_ref, sel_ref, um_ref, pw_ref, ps_ref,
                 qk_ref, vt_ref, u_ref, op_ref, d_ref, ka_ref, carry_ref, ext_ref, *, tiles_per_seq, chunk):
    i = pl.program_id(0)
    tm = x_ref.shape[0]
    xn = _layer_norm(x_ref[...], g_ref[...], b_ref[...]).astype(BF16)
    first_tile = i % tiles_per_seq == 0
    ext_ref[0:N_META, :] = jnp.where(first_tile, um_ref[...], ext_ref[0:N_META, :])
    for c in range(0, D_POOL, chunk):
        uc = jnp.dot(xn, w_ref[:, 3 * D_FOX + c:3 * D_FOX + c + chunk], preferred_element_type=F32)
        u_ref[:, c:c + chunk] = uc
        ext_ref[N_META:, c:c + chunk] = uc
    for g, w in enumerate(POOL_WINDOWS):
        cols = slice(g * POOL_GROUP_DIM, (g + 1) * POOL_GROUP_DIM)
        tok = ext_ref[N_META:N_META + tm, cols]
        acc = tok
        for j in range(1, w):
            acc = acc + ext_ref[N_META - j:N_META - j + tm, cols]
        pooled = acc * (1.0 / w) - tok
        mixed = jnp.dot(pooled.astype(BF16), pw_ref[g], preferred_element_type=F32)
        op_ref[:, cols] = (mixed * ps_ref[:, cols]).astype(BF16)
    ext_ref[0:N_META, :] = ext_ref[tm:tm + N_META, :]

    scale = HEAD_DIM ** -0.5 * LOG2E
    for c in range(0, 2 * D_FOX, chunk):
        acc = jnp.dot(xn, w_ref[:, c:c + chunk], preferred_element_type=F32)
        if c < D_FOX:
            acc = acc * scale
        qk_ref[:, c:c + chunk] = acc.astype(BF16)
    sub = lax.broadcasted_iota(jnp.int32, (VT_ROWS - HEAD_DIM, tm), 0)
    tail = jnp.where(sub == 0, 1.0, 0.0).astype(BF16)
    for c in range(0, D_FOX, chunk):
        acc = jnp.dot(xn, w_ref[:, 2 * D_FOX + c:2 * D_FOX + c + chunk], preferred_element_type=F32)
        for hd in range(chunk // HEAD_DIM):
            r0 = (c // HEAD_DIM + hd) * VT_ROWS
            vt_ref[r0:r0 + HEAD_DIM, :] = acc[:, hd * HEAD_DIM:(hd + 1) * HEAD_DIM].T.astype(BF16)
            vt_ref[r0 + HEAD_DIM:r0 + VT_ROWS, :] = tail
    fl = jnp.dot(xn, w_ref[:, 3 * D_FOX + D_POOL:], preferred_element_type=F32) + bf_ref[...]
    lf = jnp.minimum(fl, 0.0) - jnp.log1p(jnp.exp(-jnp.abs(fl)))

    row = lax.broadcasted_iota(jnp.int32, lf.shape, 0)
    acc = lf
    k = 1
    while k < tm:
        acc = acc + jnp.where(row >= k, pltpu.roll(acc, k, 0), 0.0)
        k *= 2
    d = acc + jnp.where(first_tile, 0.0, carry_ref[...])
    d_ref[...] = d
    carry_ref[...] = d[tm - 1:tm, :]

    nd = d * (-LOG2E)
    hi = nd.astype(BF16)
    r1 = nd - hi.astype(F32)
    mid = r1.astype(BF16)
    lo = (r1 - mid.astype(F32)).astype(BF16)
    split = jnp.concatenate([hi, mid, lo], axis=1)
    ka_ref[...] = jnp.dot(split, sel_ref[...], preferred_element_type=F32).astype(BF16)


def _bias_placement():
    t, h = jnp.meshgrid(jnp.arange(BIAS_TERMS), jnp.arange(FOX_HEADS), indexing="ij")
    sel = jnp.zeros((BIAS_TERMS * LANES, D_FOX), F32)
    return sel.at[(t * LANES + h).ravel(), (h * HEAD_DIM + t).ravel()].set(1.0).astype(BF16)


def _proj(x2, g, b, w, bf, sel, um, pw, ps, *, tm, tiles_per_seq):
    rows, d_model = x2.shape
    n_proj = w.shape[1]
    kern = functools.partial(_proj_kernel, tiles_per_seq=tiles_per_seq, chunk=MATMUL_CHUNK)
    return pl.pallas_call(
        kern,
        grid=(rows // tm,),
        in_specs=[
            pl.BlockSpec((tm, d_model), lambda i: (i, 0)),
            pl.BlockSpec((1, d_model), lambda i: (0, 0)),
            pl.BlockSpec((1, d_model), lambda i: (0, 0)),
            pl.BlockSpec((d_model, n_proj), lambda i: (0, 0)),
            pl.BlockSpec((1, LANES), lambda i: (0, 0)),
            pl.BlockSpec(sel.shape, lambda i: (0, 0)),
            pl.BlockSpec((N_META, D_POOL), lambda i: (0, 0)),
            pl.BlockSpec(pw.shape, lambda i: (0, 0, 0)),
            pl.BlockSpec((1, D_POOL), lambda i: (0, 0)),
        ],
        out_specs=[
            pl.BlockSpec((tm, 2 * D_FOX), lambda i: (i, 0)),
            pl.BlockSpec((FOX_HEADS * VT_ROWS, tm), lambda i: (i // tiles_per_seq, i % tiles_per_seq)),
            pl.BlockSpec((tm, D_POOL), lambda i: (i, 0)),
            pl.BlockSpec((tm, D_POOL), lambda i: (i, 0)),
            pl.BlockSpec((tm, LANES), lambda i: (i, 0)),
            pl.BlockSpec((tm, D_FOX), lambda i: (i, 0)),
        ],
        out_shape=[
            jax.ShapeDtypeStruct((rows, 2 * D_FOX), BF16),
            jax.ShapeDtypeStruct((rows // (tm * tiles_per_seq) * FOX_HEADS * VT_ROWS, tm * tiles_per_seq), BF16),
            jax.ShapeDtypeStruct((rows, D_POOL), F32),
            jax.ShapeDtypeStruct((rows, D_POOL), BF16),
            jax.ShapeDtypeStruct((rows, LANES), F32),
            jax.ShapeDtypeStruct((rows, D_FOX), BF16),
        ],
        scratch_shapes=[pltpu.VMEM((1, LANES), F32), pltpu.VMEM((N_META + tm, D_POOL), F32)],
        compiler_params=_params(1),
        name="proj",
    )(x2, g, b, w, bf, sel, um, pw, ps)


def _attn_kernel(q_ref, k_ref, ka_ref, vt_ref, km_ref, kam_ref, vtm_ref, o_ref, sa_ref, sb_ref, *, th, tk):
    qi = pl.program_id(2)
    nt = (((1,), (1,)), ((), ()))
    lane = lax.broadcasted_iota(jnp.int32, (th, HEAD_DIM), 1)
    ones = jnp.where(lane < BIAS_TERMS, 1.0, 0.0).astype(BF16)
    qa = [jnp.concatenate([q_ref[h * th:(h + 1) * th, :], ones], axis=1) for h in range(2)]
    vrows = vt_ref.shape[0]

    def scores(keys, half):
        return lax.dot_general(keys, qa[half], nt, preferred_element_type=F32)

    def x_keys(j):
        ks = pl.multiple_of(j * tk, tk)
        return jnp.concatenate([k_ref[pl.ds(ks, tk), :], ka_ref[pl.ds(ks, tk), :]], axis=1)

    def update(read_s, vt, carry, masked=False):
        m, acc = carry
        if masked:
            key = lax.broadcasted_iota(jnp.int32, (tk, th), 0)
            qry = lax.broadcasted_iota(jnp.int32, (tk, th), 1)
            read = lambda: jnp.where(key <= qry, read_s(), -jnp.inf)
        else:
            read = read_s
        m_new = jnp.maximum(m, jnp.max(read(), axis=0, keepdims=True))
        a = jnp.exp2(m - m_new)
        p = jnp.exp2(read() - m_new).astype(BF16)
        return m_new, a * acc + jnp.dot(vt, p, preferred_element_type=F32)

    def x_vt(j):
        return vt_ref[:, pl.ds(pl.multiple_of(j * tk, tk), tk)]

    keys_m = jnp.concatenate([km_ref[...], kam_ref[...]], axis=1)
    carry = []
    for h in range(2):
        s_m = scores(keys_m, h)
        init = (jnp.full((1, th), -jnp.inf, F32), jnp.zeros((vrows, th), F32))
        carry.append(update(lambda: s_m, vtm_ref[...], init))

    def fill(buf, j):
        keys = x_keys(j)
        for h in range(2):
            buf[h] = scores(keys, h)

    def drain(buf, j, carry):
        vt = x_vt(j)
        return tuple(update(lambda h=h: buf[h], vt, carry[h]) for h in range(2))

    def body(jj, carry):
        j = 2 * jj
        fill(sb_ref, j + 1)
        carry = drain(sa_ref, j, carry)
        fill(sa_ref, j + 2)
        return drain(sb_ref, j + 1, carry)

    first = 2 * qi
    fill(sa_ref, 0)
    c0, c1 = lax.fori_loop(0, qi, body, tuple(carry))
    s11 = scores(x_keys(first + 1), 1)
    c0 = update(lambda: sa_ref[0], x_vt(first), c0, masked=True)
    c1 = update(lambda: sa_ref[1], x_vt(first), c1)
    c1 = update(lambda: s11, x_vt(first + 1), c1, masked=True)
    for h, (m, acc) in enumerate((c0, c1)):
        out_t = acc[:HEAD_DIM] / acc[HEAD_DIM:HEAD_DIM + 1]
        o_ref[h * th:(h + 1) * th, :] = out_t.T.astype(BF16)


def _attention(qk, kaug, vt, qk_m, kaug_m, vt_m, *, batch, seq, th, tk):
    assert th == tk
    tq = 2 * th
    nq = seq // tq
    vrows = vt.shape[0] // (batch * FOX_HEADS)
    kern = functools.partial(_attn_kernel, th=th, tk=tk)
    return pl.pallas_call(
        kern,
        grid=(batch, FOX_HEADS, nq),
        in_specs=[
            pl.BlockSpec((tq, HEAD_DIM), lambda b, h, i: (b * nq + i, h)),
            pl.BlockSpec((seq, HEAD_DIM), lambda b, h, i: (b, FOX_HEADS + h)),
            pl.BlockSpec((seq, HEAD_DIM), lambda b, h, i: (b, h)),
            pl.BlockSpec((vrows, seq), lambda b, h, i: (b * FOX_HEADS + h, 0)),
            pl.BlockSpec((META_PAD, HEAD_DIM), lambda b, h, i: (0, FOX_HEADS + h)),
            pl.BlockSpec((META_PAD, HEAD_DIM), lambda b, h, i: (0, h)),
            pl.BlockSpec((vrows, META_PAD), lambda b, h, i: (h, 0)),
        ],
        out_specs=pl.BlockSpec((tq, HEAD_DIM), lambda b, h, i: (b * nq + i, h)),
        out_shape=jax.ShapeDtypeStruct((batch * seq, D_FOX), BF16),
        scratch_shapes=[pltpu.VMEM((2, tk, th), F32), pltpu.VMEM((2, tk, th), F32)],
        compiler_params=_params(3),
        name="attn",
    )(qk, qk, kaug, vt, qk_m, kaug_m, vt_m)


def _bias_terms(neg_bias):
    terms, rest = [], neg_bias
    for _ in range(BIAS_TERMS):
        t = rest.astype(BF16)
        terms.append(t)
        rest = rest - t.astype(F32)
    return jnp.stack(terms)


def _mix_kernel(x_ref, of_ref, op_ref, wo_ref, gi_ref, bi_ref, gm_ref, bm_ref, wr_ref, br_ref, tri_ref,
                h_ref, hp_ref, ri_ref, rg_ref, cnt_ref, carry_ref, *, chunk):
    i = pl.program_id(0)
    tm = x_ref.shape[0]
    d_model = x_ref.shape[1]
    h0 = _layer_norm(x_ref[...], gi_ref[...], bi_ref[...])
    of = of_ref[...]
    op = op_ref[...]
    for c in range(0, d_model, chunk):
        mix = jnp.dot(of, wo_ref[0:D_FOX, c:c + chunk], preferred_element_type=F32)
        mix = mix + jnp.dot(op, wo_ref[D_FOX:, c:c + chunk], preferred_element_type=F32)
        h_ref[:, c:c + chunk] = DEEPNORM_ALPHA * h0[:, c:c + chunk] + mix
    h1 = _layer_norm(h_ref[...], gm_ref[...], bm_ref[...])
    h_ref[...] = h1
    _store_slabs(hp_ref, _pack_halves(h1))

    h_hi = h1.astype(BF16)
    h_mid = (h1 - h_hi.astype(F32)).astype(BF16)
    hh = jnp.dot(h_hi, wr_ref[...], preferred_element_type=F32)
    mh = jnp.dot(h_mid, wr_ref[:, :LANES], preferred_element_type=F32)
    logits = hh[:, :LANES] + hh[:, LANES:] + mh + br_ref[...]
    lt = logits.T[:ROUTE_ROWS]
    row = lax.broadcasted_iota(jnp.int32, lt.shape, 0)
    neg = -jnp.inf
    first_min = lambda hit: jnp.min(jnp.where(hit, row, ROUTE_ROWS), axis=0, keepdims=True)
    gl = jnp.where(row < N_GROUPS, lt, neg)
    gmax = jnp.max(gl, axis=0, keepdims=True)
    g_idx = first_min(gl == gmax)
    p_g = 1.0 / jnp.sum(jnp.exp(gl - gmax), axis=0, keepdims=True)
    lo = N_GROUPS + g_idx * EXPERTS_PER_GROUP
    el = jnp.where((row >= lo) & (row < lo + EXPERTS_PER_GROUP), lt, neg)
    v1 = jnp.max(el, axis=0, keepdims=True)
    i1 = first_min(el == v1)
    el2 = jnp.where(row == i1, neg, el)
    v2 = jnp.max(el2, axis=0, keepdims=True)
    i2 = first_min(el2 == v2)
    t = jnp.exp(v2 - v1)
    gate1 = p_g / (1.0 + t)
    gate2 = gate1 * t

    @pl.when(i == 0)
    def _():
        carry_ref[...] = jnp.zeros_like(carry_ref)

    upper = tri_ref[...]
    oh1 = (row == i1).astype(F32)
    oh2 = (row == i2).astype(F32)
    pre1 = jnp.dot(oh1.astype(BF16), upper, preferred_element_type=F32)
    pre2 = jnp.dot(oh2.astype(BF16), upper, preferred_element_type=F32)
    cnt1 = jnp.sum(oh1, axis=1, keepdims=True)
    cnt2 = jnp.sum(oh2, axis=1, keepdims=True)
    base = carry_ref[...]
    rank1 = jnp.sum((pre1 + base) * oh1, axis=0, keepdims=True)
    rank2 = jnp.sum((pre2 + base + cnt1) * oh2, axis=0, keepdims=True)
    total = base + cnt1 + cnt2
    carry_ref[...] = total
    cnt_ref[...] = jnp.broadcast_to(total, cnt_ref.shape)

    r8 = lax.broadcasted_iota(jnp.int32, ri_ref.shape, 0)
    ri_ref[...] = jnp.where(r8 == 0, i1 - N_GROUPS, jnp.where(r8 == 1, i2 - N_GROUPS, jnp.where(
        r8 == 2, rank1.astype(jnp.int32), jnp.where(r8 == 3, rank2.astype(jnp.int32), 0))))
    rl = lax.broadcasted_iota(jnp.int32, (LANES, tm), 0)
    rg_ref[...] = jnp.where(rl == 0, gate1, jnp.where(rl == 1, gate2, 0.0)).T


def _mix(x2, o_fox, o_pool, wo, gi, bi, gm, bm, wr, br, *, tm):
    rows, d_model = x2.shape
    idx = jnp.arange(tm, dtype=jnp.int32)
    tri = (idx[:, None] < idx[None, :]).astype(BF16)
    kern = functools.partial(_mix_kernel, chunk=MATMUL_CHUNK)
    row_spec = lambda w: pl.BlockSpec((tm, w), lambda i: (i, 0))
    vec_spec = lambda w: pl.BlockSpec((1, w), lambda i: (0, 0))
    return pl.pallas_call(
        kern,
        grid=(rows // tm,),
        in_specs=[
            row_spec(d_model), row_spec(D_FOX), row_spec(D_POOL),
            pl.BlockSpec(wo.shape, lambda i: (0, 0)),
            vec_spec(d_model), vec_spec(d_model), vec_spec(d_model), vec_spec(d_model),
            pl.BlockSpec(wr.shape, lambda i: (0, 0)),
            vec_spec(LANES),
            pl.BlockSpec((tm, tm), lambda i: (0, 0)),
        ],
        out_specs=[row_spec(d_model), pl.BlockSpec((tm * SLAB_ROWS, LANES), lambda i: (i, 0)),
                   pl.BlockSpec((SLAB_ROWS, tm), lambda i: (0, i)), row_spec(LANES),
                   pl.BlockSpec((ROUTE_ROWS, LANES), lambda i: (0, 0))],
        out_shape=[
            jax.ShapeDtypeStruct((rows, d_model), F32),
            jax.ShapeDtypeStruct((rows * SLAB_ROWS, LANES), jnp.uint32),
            jax.ShapeDtypeStruct((SLAB_ROWS, rows), jnp.int32),
            jax.ShapeDtypeStruct((rows, LANES), F32),
            jax.ShapeDtypeStruct((ROUTE_ROWS, LANES), F32),
        ],
        scratch_shapes=[pltpu.VMEM((ROUTE_ROWS, 1), F32)],
        compiler_params=_params(1),
        name="mix",
    )(x2, o_fox, o_pool, wo, gi, bi, gm, bm, wr, br, tri)


def _moe_kernel(be_ref, na_ref, tok_ref, tokn_ref, dst_ref, h_hbm, w13_ref, w2_ref, y_hbm, xbuf, obuf, gsem, ssem):
    i = pl.program_id(0)
    n_act = na_ref[0]
    te = xbuf.shape[1] * SLAB_ROWS
    slot = i % 2

    def row_view(buf, s, r):
        return buf.at[s, r // SLAB_ROWS, :, r % SLAB_ROWS, :]

    def start_gather(idx_ref, s):
        for r in range(te):
            pltpu.make_async_copy(h_hbm.at[idx_ref[0, 0, r]], row_view(xbuf, s, r), gsem.at[s]).start(priority=r % 2)

    def wait_gather(s):
        pltpu.make_async_copy(xbuf.at[1 - s], xbuf.at[s], gsem.at[s]).wait()

    def start_scatter(idx, s):
        for r in range(te):
            pltpu.make_async_copy(row_view(obuf, s, r), y_hbm.at[idx(r)], ssem.at[s]).start(priority=r % 2)

    def wait_scatter(s):
        pltpu.make_async_copy(obuf.at[1 - s], obuf.at[s], ssem.at[s]).wait()

    @pl.when(i < n_act)
    def _():
        @pl.when(i == 0)
        def _():
            start_gather(tok_ref, slot)
            spare = y_hbm.shape[0] - 2 * te
            obuf[...] = jnp.zeros_like(obuf)
            for s in range(2):
                start_scatter(lambda r, s=s: spare + s * te + r, s)
            for s in range(2):
                wait_scatter(s)

        wait_gather(slot)

        @pl.when(i + 1 < n_act)
        def _():
            start_gather(tokn_ref, 1 - slot)

        @pl.when(i >= 2)
        def _():
            wait_scatter(slot)

        x = _unpack_halves(_row_tiles_to_value(xbuf.at[slot])).astype(BF16)
        gate = jnp.dot(x, w13_ref[0, :, :D_EXPERT].astype(BF16), preferred_element_type=F32)
        up = jnp.dot(x, w13_ref[0, :, D_EXPERT:].astype(BF16), preferred_element_type=F32)
        hid = (gate / (1.0 + jnp.exp(-gate)) * up).astype(BF16)
        out = jnp.dot(hid, w2_ref[0].astype(BF16), preferred_element_type=F32)
        _value_to_row_tiles(obuf.at[slot], _pack_halves(out))
        start_scatter(lambda r: dst_ref[0, 0, r], slot)

        @pl.when(i == n_act - 1)
        def _():
            @pl.when(i >= 1)
            def _():
                wait_scatter(1 - slot)
            wait_scatter(slot)


def _moe(blk_e, n_act, rows_tok, rows_dst, h1p, w13, w2, *, te):
    rows = h1p.shape[0] // SLAB_ROWS
    words = SLAB_ROWS * LANES
    d_model = 2 * words
    nb = rows_tok.shape[0]
    smem_spec = lambda off: pl.BlockSpec((1, 1, te), lambda i, be, na: (jnp.minimum(i + off, nb - 1), 0, 0),
                                         memory_space=pltpu.SMEM)
    tiles = (2, te // SLAB_ROWS, words // LANES, SLAB_ROWS, LANES)
    y = pl.pallas_call(
        _moe_kernel,
        grid_spec=pltpu.PrefetchScalarGridSpec(
            num_scalar_prefetch=2,
            grid=(nb,),
            in_specs=[
                smem_spec(0), smem_spec(1), smem_spec(0),
                pl.BlockSpec(memory_space=pl.ANY),
                pl.BlockSpec((1, d_model, 2 * D_EXPERT), lambda i, be, na: (be[i], 0, 0)),
                pl.BlockSpec((1, D_EXPERT, d_model), lambda i, be, na: (be[i], 0, 0)),
            ],
            out_specs=pl.BlockSpec(memory_space=pl.ANY),
            scratch_shapes=[
                pltpu.VMEM(tiles, jnp.uint32),
                pltpu.VMEM(tiles, jnp.uint32),
                pltpu.SemaphoreType.DMA((2,)),
                pltpu.SemaphoreType.DMA((2,)),
            ],
        ),
        out_shape=jax.ShapeDtypeStruct((2 * rows + 2 * te, SLAB_ROWS, LANES), jnp.uint32),
        compiler_params=_params(1, vmem=MOE_VMEM_LIMIT),
        name="moe",
    )(blk_e, n_act, rows_tok, rows_tok, rows_dst, h1p.reshape(rows, SLAB_ROWS, LANES), w13, w2)
    return y.reshape(-1, LANES)


def _final_kernel(h_ref, y0_ref, y1_ref, rg_ref, g_ref, b_ref, o_ref):
    rg = rg_ref[...]
    ffn = _unpack_halves(_load_slabs(y0_ref)) * rg[:, 0:1] + _unpack_halves(_load_slabs(y1_ref)) * rg[:, 1:2]
    o_ref[...] = _layer_norm(DEEPNORM_ALPHA * h_ref[...] + ffn, g_ref[...], b_ref[...])


def _final(h1, y, rg, g, b, *, tm):
    rows, d_model = h1.shape
    nt = rows // tm
    return pl.pallas_call(
        _final_kernel,
        grid=(nt,),
        in_specs=[
            pl.BlockSpec((tm, d_model), lambda i: (i, 0)),
            pl.BlockSpec((tm * SLAB_ROWS, LANES), lambda i: (i, 0)),
            pl.BlockSpec((tm * SLAB_ROWS, LANES), lambda i: (nt + i, 0)),
            pl.BlockSpec((tm, LANES), lambda i: (i, 0)),
            pl.BlockSpec((1, d_model), lambda i: (0, 0)),
            pl.BlockSpec((1, d_model), lambda i: (0, 0)),
        ],
        out_specs=pl.BlockSpec((tm, d_model), lambda i: (i, 0)),
        out_shape=jax.ShapeDtypeStruct((rows, d_model), F32),
        compiler_params=_params(1),
        name="final",
    )(h1, y, y, rg, g, b)


def _dispatch_tables(ri, cnt, *, rows, te):
    experts = ri[0:2]
    rank = ri[2:4]
    counts = cnt[N_GROUPS:N_GROUPS + N_EXPERTS, 0].astype(jnp.int32)
    padded = ((counts + te - 1) // te) * te
    pend = jnp.cumsum(padded)
    pstart = pend - padded
    eids = jnp.arange(N_EXPERTS, dtype=jnp.int32)
    seg = jnp.sum(jnp.where(experts[..., None] == eids, pstart, 0), axis=-1)
    dest = (seg + rank).reshape(-1)
    nb = (2 * rows + N_EXPERTS * (te - 1) + te - 1) // te
    out_row = jnp.arange(2 * rows, dtype=jnp.int32)
    pos = jnp.arange(nb * te, dtype=jnp.int32)
    rows_dst = (2 * rows + pos % (2 * te)).at[dest].set(out_row)
    rows_tok = jnp.where(rows_dst < rows, rows_dst, jnp.where(rows_dst < 2 * rows, rows_dst - rows, 0))
    blk_e = jnp.minimum(jnp.sum(pend[None, :] <= (jnp.arange(nb, dtype=jnp.int32) * te)[:, None], axis=1),
                        N_EXPERTS - 1).astype(jnp.int32)
    n_act = (pend[-1] // te).astype(jnp.int32).reshape(1)
    return blk_e, n_act, rows_tok.reshape(nb, 1, te), rows_dst.reshape(nb, 1, te)


def kernel(x, meta, ln_in_g, ln_in_b, w_in, b_f, pool_w, pool_scale, w_out, ln_mix_g, ln_mix_b,
           w_router_g, b_router_g, w_router_e, b_router_e, w13, w2, ln_ffn_g, ln_ffn_b):
    batch, seq, d_model = x.shape
    rows = batch * seq
    tile = min(ROW_TILE, seq)
    attn_tile = min(ATTN_TILE, seq // 2)
    assert seq % (2 * attn_tile) == 0 and seq % tile == 0 and tile % N_META == 0
    assert w_in.shape[0] == 1, "depth-1 trunk"

    x2 = x.reshape(rows, d_model)
    row = lambda v: v.reshape(1, -1).astype(F32)

    wi = w_in[0]
    w_f = jnp.pad(wi[:, 3 * D_FOX:3 * D_FOX + FOX_HEADS], ((0, 0), (0, LANES - FOX_HEADS)))
    w_proj = jnp.concatenate([wi[:, :3 * D_FOX], wi[:, 3 * D_FOX + FOX_HEADS:], w_f], axis=1).astype(BF16)
    bf = jnp.pad(b_f[0], (0, LANES - FOX_HEADS)).reshape(1, LANES)
    gi, bi = row(ln_in_g), row(ln_in_b)

    meta_pad = jnp.pad(meta.astype(F32), ((0, META_PAD - N_META), (0, 0)))
    sel = _bias_placement()
    pw, ps = pool_w[0].astype(BF16), row(pool_scale[0])
    no_halo = jnp.zeros((N_META, D_POOL), F32)
    qk_m, vt_m, u_m, _, c_m, _ = _proj(meta_pad, gi, bi, w_proj, bf, sel, no_halo, pw, ps, tm=META_PAD, tiles_per_seq=1)
    d_meta = (c_m[:N_META, :FOX_HEADS] - c_m[N_META - 1:N_META, :FOX_HEADS]) * LOG2E
    d_meta = jnp.pad(d_meta, ((0, META_PAD - N_META), (0, 0)), constant_values=MASKED_BIAS)
    terms = _bias_terms(-d_meta)
    kaug_m = jnp.zeros((META_PAD, FOX_HEADS, HEAD_DIM), BF16).at[:, :, :BIAS_TERMS].set(terms.transpose(1, 2, 0))
    kaug_m = kaug_m.reshape(META_PAD, D_FOX)

    qk, vt, _, o_pool, _, kaug = _proj(x2, gi, bi, w_proj, bf, sel, u_m[:N_META], pw, ps, tm=tile,
                                       tiles_per_seq=seq // tile)
    o_fox = _attention(qk, kaug, vt, qk_m, kaug_m, vt_m, batch=batch, seq=seq, th=attn_tile, tk=attn_tile)

    w_r = jnp.pad(jnp.concatenate([w_router_g[0], w_router_e[0]], axis=1),
                  ((0, 0), (0, LANES - N_GROUPS - N_EXPERTS)))
    w_r_hi = w_r.astype(BF16)
    w_r = jnp.concatenate([w_r_hi, (w_r - w_r_hi.astype(F32)).astype(BF16)], axis=1)
    b_r = jnp.pad(jnp.concatenate([b_router_g[0], b_router_e[0]]), (0, LANES - N_GROUPS - N_EXPERTS)).reshape(1, LANES)
    h1, h1p, ri, rg, cnt = _mix(x2, o_fox, o_pool, w_out[0].astype(BF16), gi, bi, row(ln_mix_g[0]),
                                row(ln_mix_b[0]), w_r, b_r, tm=tile)

    te = EXPERT_BLOCK_ROWS
    blk_e, n_act, rows_tok, rows_dst = _dispatch_tables(ri, cnt, rows=rows, te=te)
    y = _moe(blk_e, n_act, rows_tok, rows_dst, h1p, w13[0], w2[0], te=te)

    out = _final(h1, y, rg, row(ln_ffn_g[0]), row(ln_ffn_b[0]), tm=tile)
    return out.reshape(batch, seq, d_model)
```

```python
import functools

import jax
import jax.numpy as jnp
from jax import lax
from jax.experimental import pallas as pl
from jax.experimental.pallas import tpu as pltpu

F32 = jnp.float32
BF16 = jnp.bfloat16

N_META = 16
FOX_HEADS = 8
HEAD_DIM = 128
D_FOX = FOX_HEADS * HEAD_DIM
POOL_WINDOWS = (2, 4, 8, 16)
POOL_GROUP_DIM = 256
D_POOL = len(POOL_WINDOWS) * POOL_GROUP_DIM
N_GROUPS = 4
EXPERTS_PER_GROUP = 8
N_EXPERTS = N_GROUPS * EXPERTS_PER_GROUP
D_EXPERT = 1024
LN_EPS = 1e-5
DEEPNORM_ALPHA = 2.0 ** 0.25

LANES = 128
SLAB_ROWS = 8
META_PAD = 128
MASKED_BIAS = 1e30
LOG2E = 1.4426950408889634
BIAS_TERMS = 3
ROUTE_ROWS = 48
VT_ROWS = 144
ROW_TILE = 512
ATTN_TILE = 1024
EXPERT_BLOCK_ROWS = 256
MATMUL_CHUNK = 512
VMEM_LIMIT = 56 * 1024 * 1024
MOE_VMEM_LIMIT = 62 * 1024 * 1024


def _params(n_axes, vmem=VMEM_LIMIT):
    return pltpu.CompilerParams(dimension_semantics=("arbitrary",) * n_axes, vmem_limit_bytes=vmem)


def _layer_norm(x, g, b):
    mu = jnp.mean(x, axis=-1, keepdims=True)
    xc = x - mu
    var = jnp.mean(xc * xc, axis=-1, keepdims=True)
    return xc * lax.rsqrt(var + LN_EPS) * g + b


def _pack_halves(x):
    n = x.shape[1] // 2
    bits = lambda v: lax.bitcast_convert_type(v.astype(BF16).astype(F32), jnp.uint32)
    return (bits(x[:, :n]) >> 16) | (bits(x[:, n:]) & jnp.uint32(0xFFFF0000))


def _unpack_halves(w):
    lo = lax.bitcast_convert_type(w << 16, F32)
    hi = lax.bitcast_convert_type(w & jnp.uint32(0xFFFF0000), F32)
    return jnp.concatenate([lo, hi], axis=1)


def _store_slabs(ref, words):
    rows = words.shape[0]
    for k in range(SLAB_ROWS):
        ref[pl.ds(k, rows, stride=SLAB_ROWS), :] = words[:, k * LANES:(k + 1) * LANES]


def _load_slabs(ref):
    rows = ref.shape[0] // SLAB_ROWS
    return jnp.concatenate([ref[pl.ds(k, rows, stride=SLAB_ROWS), :] for k in range(SLAB_ROWS)], axis=1)


def _row_tiles_to_value(ref):
    rt, lt = ref.shape[0], ref.shape[1]
    return jnp.concatenate([ref[:, j].reshape(rt * SLAB_ROWS, LANES) for j in range(lt)], axis=1)


def _value_to_row_tiles(ref, value):
    rt, lt = ref.shape[0], ref.shape[1]
    for j in range(lt):
        ref[:, j] = value[:, j * LANES:(j + 1) * LANES].reshape(rt, SLAB_ROWS, LANES)


def _proj_kernel(x_ref, g_ref, b_ref, w_ref, bf_ref, sel_ref, um_ref, pw_ref, ps_ref,
                 qk_ref, vt_ref, u_ref, op_ref, d_ref, ka_ref, carry_ref, ext_ref, *, tiles_per_seq, chunk):
    i = pl.program_id(0)
    tm = x_ref.shape[0]
    xn = _layer_norm(x_ref[...], g_ref[...], b_ref[...]).astype(BF16)
    first_tile = i % tiles_per_seq == 0
    ext_ref[0:N_META, :] = jnp.where(first_tile, um_ref[...], ext_ref[0:N_META, :])
    for c in range(0, D_POOL, chunk):
        uc = jnp.dot(xn, w_ref[:, 3 * D_FOX + c:3 * D_FOX + c + chunk], preferred_element_type=F32)
        u_ref[:, c:c + chunk] = uc
        ext_ref[N_META:, c:c + chunk] = uc
    for g, w in enumerate(POOL_WINDOWS):
        cols = slice(g * POOL_GROUP_DIM, (g + 1) * POOL_GROUP_DIM)
        tok = ext_ref[N_META:N_META + tm, cols]
        acc = tok
        for j in range(1, w):
            acc = acc + ext_ref[N_META - j:N_META - j + tm, cols]
        pooled = acc * (1.0 / w) - tok
        mixed = jnp.dot(pooled.astype(BF16), pw_ref[g], preferred_element_type=F32)
        op_ref[:, cols] = (mixed * ps_ref[:, cols]).astype(BF16)
    ext_ref[0:N_META, :] = ext_ref[tm:tm + N_META, :]

    scale = HEAD_DIM ** -0.5 * LOG2E
    for c in range(0, 2 * D_FOX, chunk):
        acc = jnp.dot(xn, w_ref[:, c:c + chunk], preferred_element_type=F32)
        if c < D_FOX:
            acc = acc * scale
        qk_ref[:, c:c + chunk] = acc.astype(BF16)
    sub = lax.broadcasted_iota(jnp.int32, (VT_ROWS - HEAD_DIM, tm), 0)
    tail = jnp.where(sub == 0, 1.0, 0.0).astype(BF16)
    for c in range(0, D_FOX, chunk):
        acc = jnp.dot(xn, w_ref[:, 2 * D_FOX + c:2 * D_FOX + c + chunk], preferred_element_type=F32)
        for hd in range(chunk // HEAD_DIM):
            r0 = (c // HEAD_DIM + hd) * VT_ROWS
            vt_ref[r0:r0 + HEAD_DIM, :] = acc[:, hd * HEAD_DIM:(hd + 1) * HEAD_DIM].T.astype(BF16)
            vt_ref[r0 + HEAD_DIM:r0 + VT_ROWS, :] = tail
    fl = jnp.dot(xn, w_ref[:, 3 * D_FOX + D_POOL:], preferred_element_type=F32) + bf_ref[...]
    lf = jnp.minimum(fl, 0.0) - jnp.log1p(jnp.exp(-jnp.abs(fl)))

    row = lax.broadcasted_iota(jnp.int32, lf.shape, 0)
    acc = lf
    k = 1
    while k < tm:
        acc = acc + jnp.where(row >= k, pltpu.roll(acc, k, 0), 0.0)
        k *= 2
    d = acc + jnp.where(first_tile, 0.0, carry_ref[...])
    d_ref[...] = d
    carry_ref[...] = d[tm - 1:tm, :]

    nd = d * (-LOG2E)
    hi = nd.astype(BF16)
    r1 = nd - hi.astype(F32)
    mid = r1.astype(BF16)
    lo = (r1 - mid.astype(F32)).astype(BF16)
    split = jnp.concatenate([hi, mid, lo], axis=1)
    ka_ref[...] = jnp.dot(split, sel_ref[...], preferred_element_type=F32).astype(BF16)


def _bias_placement():
    t, h = jnp.meshgrid(jnp.arange(BIAS_TERMS), jnp.arange(FOX_HEADS), indexing="ij")
    sel = jnp.zeros((BIAS_TERMS * LANES, D_FOX), F32)
    return sel.at[(t * LANES + h).ravel(), (h * HEAD_DIM + t).ravel()].set(1.0).astype(BF16)


def _proj(x2, g, b, w, bf, sel, um, pw, ps, *, tm, tiles_per_seq):
    rows, d_model = x2.shape
    n_proj = w.shape[1]
    kern = functools.partial(_proj_kernel, tiles_per_seq=tiles_per_seq, chunk=MATMUL_CHUNK)
    return pl.pallas_call(
        kern,
        grid=(rows // tm,),
        in_specs=[
            pl.BlockSpec((tm, d_model), lambda i: (i, 0)),
            pl.BlockSpec((1, d_model), lambda i: (0, 0)),
            pl.BlockSpec((1, d_model), lambda i: (0, 0)),
            pl.BlockSpec((d_model, n_proj), lambda i: (0, 0)),
            pl.BlockSpec((1, LANES), lambda i: (0, 0)),
            pl.BlockSpec(sel.shape, lambda i: (0, 0)),
            pl.BlockSpec((N_META, D_POOL), lambda i: (0, 0)),
            pl.BlockSpec(pw.shape, lambda i: (0, 0, 0)),
            pl.BlockSpec((1, D_POOL), lambda i: (0, 0)),
        ],
        out_specs=[
            pl.BlockSpec((tm, 2 * D_FOX), lambda i: (i, 0)),
            pl.BlockSpec((FOX_HEADS * VT_ROWS, tm), lambda i: (i // tiles_per_seq, i % tiles_per_seq)),
            pl.BlockSpec((tm, D_POOL), lambda i: (i, 0)),
            pl.BlockSpec((tm, D_POOL), lambda i: (i, 0)),
            pl.BlockSpec((tm, LANES), lambda i: (i, 0)),
            pl.BlockSpec((tm, D_FOX), lambda i: (i, 0)),
        ],
        out_shape=[
            jax.ShapeDtypeStruct((rows, 2 * D_FOX), BF16),
            jax.ShapeDtypeStruct((rows // (tm * tiles_per_seq) * FOX_HEADS * VT_ROWS, tm * tiles_per_seq), BF16),
            jax.ShapeDtypeStruct((rows, D_POOL), F32),
            jax.ShapeDtypeStruct((rows, D_POOL), BF16),
            jax.ShapeDtypeStruct((rows, LANES), F32),
            jax.ShapeDtypeStruct((rows, D_FOX), BF16),
        ],
        scratch_shapes=[pltpu.VMEM((1, LANES), F32), pltpu.VMEM((N_META + tm, D_POOL), F32)],
        compiler_params=_params(1),
        name="proj",
    )(x2, g, b, w, bf, sel, um, pw, ps)


def _attn_kernel(q_ref, k_ref, ka_ref, vt_ref, km_ref, kam_ref, vtm_ref, o_ref, sa_ref, sb_ref, *, th, tk):
    qi = pl.program_id(2)
    nt = (((1,), (1,)), ((), ()))
    lane = lax.broadcasted_iota(jnp.int32, (th, HEAD_DIM), 1)
    ones = jnp.where(lane < BIAS_TERMS, 1.0, 0.0).astype(BF16)
    qa = [jnp.concatenate([q_ref[h * th:(h + 1) * th, :], ones], axis=1) for h in range(2)]
    vrows = vt_ref.shape[0]

    def scores(keys, half):
        return lax.dot_general(keys, qa[half], nt, preferred_element_type=F32)

    def x_keys(j):
        ks = pl.multiple_of(j * tk, tk)
        return jnp.concatenate([k_ref[pl.ds(ks, tk), :], ka_ref[pl.ds(ks, tk), :]], axis=1)

    def update(read_s, vt, carry, masked=False):
        m, acc = carry
        if masked:
            key = lax.broadcasted_iota(jnp.int32, (tk, th), 0)
            qry = lax.broadcasted_iota(jnp.int32, (tk, th), 1)
            read = lambda: jnp.where(key <= qry, read_s(), -jnp.inf)
        else:
            read = read_s
        m_new = jnp.maximum(m, jnp.max(read(), axis=0, keepdims=True))
        a = jnp.exp2(m - m_new)
        p = jnp.exp2(read() - m_new).astype(BF16)
        return m_new, a * acc + jnp.dot(vt, p, preferred_element_type=F32)

    def x_vt(j):
        return vt_ref[:, pl.ds(pl.multiple_of(j * tk, tk), tk)]

    keys_m = jnp.concatenate([km_ref[...], kam_ref[...]], axis=1)
    carry = []
    for h in range(2):
        s_m = scores(keys_m, h)
        init = (jnp.full((1, th), -jnp.inf, F32), jnp.zeros((vrows, th), F32))
        carry.append(update(lambda: s_m, vtm_ref[...], init))

    def fill(buf, j):
        keys = x_keys(j)
        for h in range(2):
            buf[h] = scores(keys, h)

    def drain(buf, j, carry):
        vt = x_vt(j)
        return tuple(update(lambda h=h: buf[h], vt, carry[h]) for h in range(2))

    def body(jj, carry):
        j = 2 * jj
        fill(sb_ref, j + 1)
        carry = drain(sa_ref, j, carry)
        fill(sa_ref, j + 2)
        return drain(sb_ref, j + 1, carry)

    first = 2 * qi
    fill(sa_ref, 0)
    c0, c1 = lax.fori_loop(0, qi, body, tuple(carry))
    s11 = scores(x_keys(first + 1), 1)
    c0 = update(lambda: sa_ref[0], x_vt(first), c0, masked=True)
    c1 = update(lambda: sa_ref[1], x_vt(first), c1)
    c1 = update(lambda: s11, x_vt(first + 1), c1, masked=True)
    for h, (m, acc) in enumerate((c0, c1)):
        out_t = acc[:HEAD_DIM] / acc[HEAD_DIM:HEAD_DIM + 1]
        o_ref[h * th:(h + 1) * th, :] = out_t.T.astype(BF16)


def _attention(qk, kaug, vt, qk_m, kaug_m, vt_m, *, batch, seq, th, tk):
    assert th == tk
    tq = 2 * th
    nq = seq // tq
    vrows = vt.shape[0] // (batch * FOX_HEADS)
    kern = functools.partial(_attn_kernel, th=th, tk=tk)
    return pl.pallas_call(
        kern,
        grid=(batch, FOX_HEADS, nq),
        in_specs=[
            pl.BlockSpec((tq, HEAD_DIM), lambda b, h, i: (b * nq + i, h)),
            pl.BlockSpec((seq, HEAD_DIM), lambda b, h, i: (b, FOX_HEADS + h)),
            pl.BlockSpec((seq, HEAD_DIM), lambda b, h, i: (b, h)),
            pl.BlockSpec((vrows, seq), lambda b, h, i: (b * FOX_HEADS + h, 0)),
            pl.BlockSpec((META_PAD, HEAD_DIM), lambda b, h, i: (0, FOX_HEADS + h)),
            pl.BlockSpec((META_PAD, HEAD_DIM), lambda b, h, i: (0, h)),
            pl.BlockSpec((vrows, META_PAD), lambda b, h, i: (h, 0)),
        ],
        out_specs=pl.BlockSpec((tq, HEAD_DIM), lambda b, h, i: (b * nq + i, h)),
        out_shape=jax.ShapeDtypeStruct((batch * seq, D_FOX), BF16),
        scratch_shapes=[pltpu.VMEM((2, tk, th), F32), pltpu.VMEM((2, tk, th), F32)],
        compiler_params=_params(3),
        name="attn",
    )(qk, qk, kaug, vt, qk_m, kaug_m, vt_m)


def _bias_terms(neg_bias):
    terms, rest = [], neg_bias
    for _ in range(BIAS_TERMS):
        t = rest.astype(BF16)
        terms.append(t)
        rest = rest - t.astype(F32)
    return jnp.stack(terms)


def _mix_kernel(x_ref, of_ref, op_ref, wo_ref, gi_ref, bi_ref, gm_ref, bm_ref, wr_ref, br_ref, tri_ref,
                h_ref, hp_ref, ri_ref, rg_ref, cnt_ref, carry_ref, *, chunk):
    i = pl.program_id(0)
    tm = x_ref.shape[0]
    d_model = x_ref.shape[1]
    h0 = _layer_norm(x_ref[...], gi_ref[...], bi_ref[...])
    of = of_ref[...]
    op = op_ref[...]
    for c in range(0, d_model, chunk):
        mix = jnp.dot(of, wo_ref[0:D_FOX, c:c + chunk], preferred_element_type=F32)
        mix = mix + jnp.dot(op, wo_ref[D_FOX:, c:c + chunk], preferred_element_type=F32)
        h_ref[:, c:c + chunk] = DEEPNORM_ALPHA * h0[:, c:c + chunk] + mix
    h1 = _layer_norm(h_ref[...], gm_ref[...], bm_ref[...])
    h_ref[...] = h1
    _store_slabs(hp_ref, _pack_halves(h1))

    h_hi = h1.astype(BF16)
    h_mid = (h1 - h_hi.astype(F32)).astype(BF16)
    hh = jnp.dot(h_hi, wr_ref[...], preferred_element_type=F32)
    mh = jnp.dot(h_mid, wr_ref[:, :LANES], preferred_element_type=F32)
    logits = hh[:, :LANES] + hh[:, LANES:] + mh + br_ref[...]
    lt = logits.T[:ROUTE_ROWS]
    row = lax.broadcasted_iota(jnp.int32, lt.shape, 0)
    neg = -jnp.inf
    first_min = lambda hit: jnp.min(jnp.where(hit, row, ROUTE_ROWS), axis=0, keepdims=True)
    gl = jnp.where(row < N_GROUPS, lt, neg)
    gmax = jnp.max(gl, axis=0, keepdims=True)
    g_idx = first_min(gl == gmax)
    p_g = 1.0 / jnp.sum(jnp.exp(gl - gmax), axis=0, keepdims=True)
    lo = N_GROUPS + g_idx * EXPERTS_PER_GROUP
    el = jnp.where((row >= lo) & (row < lo + EXPERTS_PER_GROUP), lt, neg)
    v1 = jnp.max(el, axis=0, keepdims=True)
    i1 = first_min(el == v1)
    el2 = jnp.where(row == i1, neg, el)
    v2 = jnp.max(el2, axis=0, keepdims=True)
    i2 = first_min(el2 == v2)
    t = jnp.exp(v2 - v1)
    gate1 = p_g / (1.0 + t)
    gate2 = gate1 * t

    @pl.when(i == 0)
    def _():
        carry_ref[...] = jnp.zeros_like(carry_ref)

    upper = tri_ref[...]
    oh1 = (row == i1).astype(F32)
    oh2 = (row == i2).astype(F32)
    pre1 = jnp.dot(oh1.astype(BF16), upper, preferred_element_type=F32)
    pre2 = jnp.dot(oh2.astype(BF16), upper, preferred_element_type=F32)
    cnt1 = jnp.sum(oh1, axis=1, keepdims=True)
    cnt2 = jnp.sum(oh2, axis=1, keepdims=True)
    base = carry_ref[...]
    rank1 = jnp.sum((pre1 + base) * oh1, axis=0, keepdims=True)
    rank2 = jnp.sum((pre2 + base + cnt1) * oh2, axis=0, keepdims=True)
    total = base + cnt1 + cnt2
    carry_ref[...] = total
    cnt_ref[...] = jnp.broadcast_to(total, cnt_ref.shape)

    r8 = lax.broadcasted_iota(jnp.int32, ri_ref.shape, 0)
    ri_ref[...] = jnp.where(r8 == 0, i1 - N_GROUPS, jnp.where(r8 == 1, i2 - N_GROUPS, jnp.where(
        r8 == 2, rank1.astype(jnp.int32), jnp.where(r8 == 3, rank2.astype(jnp.int32), 0))))
    rl = lax.broadcasted_iota(jnp.int32, (LANES, tm), 0)
    rg_ref[...] = jnp.where(rl == 0, gate1, jnp.where(rl == 1, gate2, 0.0)).T


def _mix(x2, o_fox, o_pool, wo, gi, bi, gm, bm, wr, br, *, tm):
    rows, d_model = x2.shape
    idx = jnp.arange(tm, dtype=jnp.int32)
    tri = (idx[:, None] < idx[None, :]).astype(BF16)
    kern = functools.partial(_mix_kernel, chunk=MATMUL_CHUNK)
    row_spec = lambda w: pl.BlockSpec((tm, w), lambda i: (i, 0))
    vec_spec = lambda w: pl.BlockSpec((1, w), lambda i: (0, 0))
    return pl.pallas_call(
        kern,
        grid=(rows // tm,),
        in_specs=[
            row_spec(d_model), row_spec(D_FOX), row_spec(D_POOL),
            pl.BlockSpec(wo.shape, lambda i: (0, 0)),
            vec_spec(d_model), vec_spec(d_model), vec_spec(d_model), vec_spec(d_model),
            pl.BlockSpec(wr.shape, lambda i: (0, 0)),
            vec_spec(LANES),
            pl.BlockSpec((tm, tm), lambda i: (0, 0)),
        ],
        out_specs=[row_spec(d_model), pl.BlockSpec((tm * SLAB_ROWS, LANES), lambda i: (i, 0)),
                   pl.BlockSpec((SLAB_ROWS, tm), lambda i: (0, i)), row_spec(LANES),
                   pl.BlockSpec((ROUTE_ROWS, LANES), lambda i: (0, 0))],
        out_shape=[
            jax.ShapeDtypeStruct((rows, d_model), F32),
            jax.ShapeDtypeStruct((rows * SLAB_ROWS, LANES), jnp.uint32),
            jax.ShapeDtypeStruct((SLAB_ROWS, rows), jnp.int32),
            jax.ShapeDtypeStruct((rows, LANES), F32),
            jax.ShapeDtypeStruct((ROUTE_ROWS, LANES), F32),
        ],
        scratch_shapes=[pltpu.VMEM((ROUTE_ROWS, 1), F32)],
        compiler_params=_params(1),
        name="mix",
    )(x2, o_fox, o_pool, wo, gi, bi, gm, bm, wr, br, tri)


def _moe_kernel(be_ref, na_ref, tok_ref, tokn_ref, dst_ref, h_hbm, w13_ref, w2_ref, y_hbm, xbuf, obuf, gsem, ssem):
    i = pl.program_id(0)
    n_act = na_ref[0]
    te = xbuf.shape[1] * SLAB_ROWS
    slot = i % 2

    def row_view(buf, s, r):
        return buf.at[s, r // SLAB_ROWS, :, r % SLAB_ROWS, :]

    def start_gather(idx_ref, s):
        for r in range(te):
            pltpu.make_async_copy(h_hbm.at[idx_ref[0, 0, r]], row_view(xbuf, s, r), gsem.at[s]).start(priority=r % 2)

    def wait_gather(s):
        pltpu.make_async_copy(xbuf.at[1 - s], xbuf.at[s], gsem.at[s]).wait()

    def start_scatter(idx, s):
        for r in range(te):
            pltpu.make_async_copy(row_view(obuf, s, r), y_hbm.at[idx(r)], ssem.at[s]).start(priority=r % 2)

    def wait_scatter(s):
        pltpu.make_async_copy(obuf.at[1 - s], obuf.at[s], ssem.at[s]).wait()

    @pl.when(i < n_act)
    def _():
        @pl.when(i == 0)
        def _():
            start_gather(tok_ref, slot)
            spare = y_hbm.shape[0] - 2 * te
            obuf[...] = jnp.zeros_like(obuf)
            for s in range(2):
                start_scatter(lambda r, s=s: spare + s * te + r, s)
            for s in range(2):
                wait_scatter(s)

        wait_gather(slot)

        @pl.when(i + 1 < n_act)
        def _():
            start_gather(tokn_ref, 1 - slot)

        @pl.when(i >= 2)
        def _():
            wait_scatter(slot)

        x = _unpack_halves(_row_tiles_to_value(xbuf.at[slot])).astype(BF16)
        gate = jnp.dot(x, w13_ref[0, :, :D_EXPERT].astype(BF16), preferred_element_type=F32)
        up = jnp.dot(x, w13_ref[0, :, D_EXPERT:].astype(BF16), preferred_element_type=F32)
        hid = (gate / (1.0 + jnp.exp(-gate)) * up).astype(BF16)
        out = jnp.dot(hid, w2_ref[0].astype(BF16), preferred_element_type=F32)
        _value_to_row_tiles(obuf.at[slot], _pack_halves(out))
        start_scatter(lambda r: dst_ref[0, 0, r], slot)

        @pl.when(i == n_act - 1)
        def _():
            @pl.when(i >= 1)
            def _():
                wait_scatter(1 - slot)
            wait_scatter(slot)


def _moe(blk_e, n_act, rows_tok, rows_dst, h1p, w13, w2, *, te):
    rows = h1p.shape[0] // SLAB_ROWS
    words = SLAB_ROWS * LANES
    d_model = 2 * words
    nb = rows_tok.shape[0]
    smem_spec = lambda off: pl.BlockSpec((1, 1, te), lambda i, be, na: (jnp.minimum(i + off, nb - 1), 0, 0),
                                         memory_space=pltpu.SMEM)
    tiles = (2, te // SLAB_ROWS, words // LANES, SLAB_ROWS, LANES)
    y = pl.pallas_call(
        _moe_kernel,
        grid_spec=pltpu.PrefetchScalarGridSpec(
            num_scalar_prefetch=2,
            grid=(nb,),
            in_specs=[
                smem_spec(0), smem_spec(1), smem_spec(0),
                pl.BlockSpec(memory_space=pl.ANY),
                pl.BlockSpec((1, d_model, 2 * D_EXPERT), lambda i, be, na: (be[i], 0, 0)),
                pl.BlockSpec((1, D_EXPERT, d_model), lambda i, be, na: (be[i], 0, 0)),
            ],
            out_specs=pl.BlockSpec(memory_space=pl.ANY),
            scratch_shapes=[
                pltpu.VMEM(tiles, jnp.uint32),
                pltpu.VMEM(tiles, jnp.uint32),
                pltpu.SemaphoreType.DMA((2,)),
                pltpu.SemaphoreType.DMA((2,)),
            ],
        ),
        out_shape=jax.ShapeDtypeStruct((2 * rows + 2 * te, SLAB_ROWS, LANES), jnp.uint32),
        compiler_params=_params(1, vmem=MOE_VMEM_LIMIT),
        name="moe",
    )(blk_e, n_act, rows_tok, rows_tok, rows_dst, h1p.reshape(rows, SLAB_ROWS, LANES), w13, w2)
    return y.reshape(-1, LANES)


def _final_kernel(h_ref, y0_ref, y1_ref, rg_ref, g_ref, b_ref, o_ref):
    rg = rg_ref[...]
    ffn = _unpack_halves(_load_slabs(y0_ref)) * rg[:, 0:1] + _unpack_halves(_load_slabs(y1_ref)) * rg[:, 1:2]
    o_ref[...] = _layer_norm(DEEPNORM_ALPHA * h_ref[...] + ffn, g_ref[...], b_ref[...])


def _final(h1, y, rg, g, b, *, tm):
    rows, d_model = h1.shape
    nt = rows // tm
    return pl.pallas_call(
        _final_kernel,
        grid=(nt,),
        in_specs=[
            pl.BlockSpec((tm, d_model), lambda i: (i, 0)),
            pl.BlockSpec((tm * SLAB_ROWS, LANES), lambda i: (i, 0)),
            pl.BlockSpec((tm * SLAB_ROWS, LANES), lambda i: (nt + i, 0)),
            pl.BlockSpec((tm, LANES), lambda i: (i, 0)),
            pl.BlockSpec((1, d_model), lambda i: (0, 0)),
            pl.BlockSpec((1, d_model), lambda i: (0, 0)),
        ],
        out_specs=pl.BlockSpec((tm, d_model), lambda i: (i, 0)),
        out_shape=jax.ShapeDtypeStruct((rows, d_model), F32),
        compiler_params=_params(1),
        name="final",
    )(h1, y, y, rg, g, b)


def _invert_kernel(dest_ref, fill_hbm, out_ref):
    pltpu.sync_copy(fill_hbm, out_ref)

    def place(a, _):
        out_ref[dest_ref[a]] = a
        return 0

    lax.fori_loop(0, dest_ref.shape[0], place, 0, unroll=8)


def _invert_dispatch(dest, *, n_out, spare, spare_rows):
    smem = pl.BlockSpec(memory_space=pltpu.SMEM)
    fill = spare + jnp.arange(n_out, dtype=jnp.int32) % spare_rows
    return pl.pallas_call(
        _invert_kernel,
        in_specs=[smem, pl.BlockSpec(memory_space=pl.ANY)],
        out_specs=smem,
        out_shape=jax.ShapeDtypeStruct((n_out,), jnp.int32),
        name="invert",
    )(dest, fill)


def _dispatch_tables(ri, cnt, *, rows, te):
    experts = ri[0:2]
    rank = ri[2:4]
    counts = cnt[N_GROUPS:N_GROUPS + N_EXPERTS, 0].astype(jnp.int32)
    padded = ((counts + te - 1) // te) * te
    pend = jnp.cumsum(padded)
    pstart = pend - padded
    eids = jnp.arange(N_EXPERTS, dtype=jnp.int32)
    seg = jnp.sum(jnp.where(experts[..., None] == eids, pstart, 0), axis=-1)
    dest = (seg + rank).reshape(-1)
    nb = (2 * rows + N_EXPERTS * (te - 1) + te - 1) // te
    rows_dst = _invert_dispatch(dest, n_out=nb * te, spare=2 * rows, spare_rows=2 * te)
    rows_tok = jnp.where(rows_dst < rows, rows_dst, jnp.where(rows_dst < 2 * rows, rows_dst - rows, 0))
    blk_e = jnp.minimum(jnp.sum(pend[None, :] <= (jnp.arange(nb, dtype=jnp.int32) * te)[:, None], axis=1),
                        N_EXPERTS - 1).astype(jnp.int32)
    n_act = (pend[-1] // te).astype(jnp.int32).reshape(1)
    return blk_e, n_act, rows_tok.reshape(nb, 1, te), rows_dst.reshape(nb, 1, te)


def kernel(x, meta, ln_in_g, ln_in_b, w_in, b_f, pool_w, pool_scale, w_out, ln_mix_g, ln_mix_b,
           w_router_g, b_router_g, w_router_e, b_router_e, w13, w2, ln_ffn_g, ln_ffn_b):
    batch, seq, d_model = x.shape
    rows = batch * seq
    tile = min(ROW_TILE, seq)
    attn_tile = min(ATTN_TILE, seq // 2)
    assert seq % (2 * attn_tile) == 0 and seq % tile == 0 and tile % N_META == 0
    assert w_in.shape[0] == 1, "depth-1 trunk"

    x2 = x.reshape(rows, d_model)
    row = lambda v: v.reshape(1, -1).astype(F32)

    wi = w_in[0]
    w_f = jnp.pad(wi[:, 3 * D_FOX:3 * D_FOX + FOX_HEADS], ((0, 0), (0, LANES - FOX_HEADS)))
    w_proj = jnp.concatenate([wi[:, :3 * D_FOX], wi[:, 3 * D_FOX + FOX_HEADS:], w_f], axis=1).astype(BF16)
    bf = jnp.pad(b_f[0], (0, LANES - FOX_HEADS)).reshape(1, LANES)
    gi, bi = row(ln_in_g), row(ln_in_b)

    meta_pad = jnp.pad(meta.astype(F32), ((0, META_PAD - N_META), (0, 0)))
    sel = _bias_placement()
    pw, ps = pool_w[0].astype(BF16), row(pool_scale[0])
    no_halo = jnp.zeros((N_META, D_POOL), F32)
    qk_m, vt_m, u_m, _, c_m, _ = _proj(meta_pad, gi, bi, w_proj, bf, sel, no_halo, pw, ps, tm=META_PAD, tiles_per_seq=1)
    d_meta = (c_m[:N_META, :FOX_HEADS] - c_m[N_META - 1:N_META, :FOX_HEADS]) * LOG2E
    d_meta = jnp.pad(d_meta, ((0, META_PAD - N_META), (0, 0)), constant_values=MASKED_BIAS)
    terms = _bias_terms(-d_meta)
    kaug_m = jnp.zeros((META_PAD, FOX_HEADS, HEAD_DIM), BF16).at[:, :, :BIAS_TERMS].set(terms.transpose(1, 2, 0))
    kaug_m = kaug_m.reshape(META_PAD, D_FOX)

    qk, vt, _, o_pool, _, kaug = _proj(x2, gi, bi, w_proj, bf, sel, u_m[:N_META], pw, ps, tm=tile,
                                       tiles_per_seq=seq // tile)
    o_fox = _attention(qk, kaug, vt, qk_m, kaug_m, vt_m, batch=batch, seq=seq, th=attn_tile, tk=attn_tile)

    w_r = jnp.pad(jnp.concatenate([w_router_g[0], w_router_e[0]], axis=1),
                  ((0, 0), (0, LANES - N_GROUPS - N_EXPERTS)))
    w_r_hi = w_r.astype(BF16)
    w_r = jnp.concatenate([w_r_hi, (w_r - w_r_hi.astype(F32)).astype(BF16)], axis=1)
    b_r = jnp.pad(jnp.concatenate([b_router_g[0], b_router_e[0]]), (0, LANES - N_GROUPS - N_EXPERTS)).reshape(1, LANES)
    h1, h1p, ri, rg, cnt = _mix(x2, o_fox, o_pool, w_out[0].astype(BF16), gi, bi, row(ln_mix_g[0]),
                                row(ln_mix_b[0]), w_r, b_r, tm=tile)

    te = EXPERT_BLOCK_ROWS
    blk_e, n_act, rows_tok, rows_dst = _dispatch_tables(ri, cnt, rows=rows, te=te)
    y = _moe(blk_e, n_act, rows_tok, rows_dst, h1p, w13[0], w2[0], te=te)

    out = _final(h1, y, rg, row(ln_ffn_g[0]), row(ln_ffn_b[0]), tm=tile)
    return out.reshape(batch, seq, d_model)
```

```python
import functools

import jax
import jax.numpy as jnp
from jax import lax
from jax.experimental import pallas as pl
from jax.experimental.pallas import tpu as pltpu

F32 = jnp.float32
BF16 = jnp.bfloat16

N_META = 16
FOX_HEADS = 8
HEAD_DIM = 128
D_FOX = FOX_HEADS * HEAD_DIM
POOL_WINDOWS = (2, 4, 8, 16)
POOL_GROUP_DIM = 256
D_POOL = len(POOL_WINDOWS) * POOL_GROUP_DIM
N_GROUPS = 4
EXPERTS_PER_GROUP = 8
N_EXPERTS = N_GROUPS * EXPERTS_PER_GROUP
D_EXPERT = 1024
LN_EPS = 1e-5
DEEPNORM_ALPHA = 2.0 ** 0.25

LANES = 128
SLAB_ROWS = 8
META_PAD = 128
MASKED_BIAS = 1e30
LOG2E = 1.4426950408889634
BIAS_TERMS = 3
ROUTE_ROWS = 48
VT_ROWS = 144
ROW_TILE = 512
ATTN_TILE = 1024
EXPERT_BLOCK_ROWS = 256
MATMUL_CHUNK = 512
VMEM_LIMIT = 56 * 1024 * 1024
MOE_VMEM_LIMIT = 62 * 1024 * 1024


def _params(n_axes, vmem=VMEM_LIMIT):
    return pltpu.CompilerParams(dimension_semantics=("arbitrary",) * n_axes, vmem_limit_bytes=vmem)


def _layer_norm(x, g, b):
    mu = jnp.mean(x, axis=-1, keepdims=True)
    xc = x - mu
    var = jnp.mean(xc * xc, axis=-1, keepdims=True)
    return xc * lax.rsqrt(var + LN_EPS) * g + b


def _pack_halves(x):
    n = x.shape[1] // 2
    bits = lambda v: lax.bitcast_convert_type(v.astype(BF16).astype(F32), jnp.uint32)
    return (bits(x[:, :n]) >> 16) | (bits(x[:, n:]) & jnp.uint32(0xFFFF0000))


def _unpack_halves(w):
    lo = lax.bitcast_convert_type(w << 16, F32)
    hi = lax.bitcast_convert_type(w & jnp.uint32(0xFFFF0000), F32)
    return jnp.concatenate([lo, hi], axis=1)


def _store_slabs(ref, words):
    rows = words.shape[0]
    for k in range(SLAB_ROWS):
        ref[pl.ds(k, rows, stride=SLAB_ROWS), :] = words[:, k * LANES:(k + 1) * LANES]


def _load_slabs(ref):
    rows = ref.shape[0] // SLAB_ROWS
    return jnp.concatenate([ref[pl.ds(k, rows, stride=SLAB_ROWS), :] for k in range(SLAB_ROWS)], axis=1)


def _row_tiles_to_value(ref):
    rt, lt = ref.shape[0], ref.shape[1]
    return jnp.concatenate([ref[:, j].reshape(rt * SLAB_ROWS, LANES) for j in range(lt)], axis=1)


def _value_to_row_tiles(ref, value):
    rt, lt = ref.shape[0], ref.shape[1]
    for j in range(lt):
        ref[:, j] = value[:, j * LANES:(j + 1) * LANES].reshape(rt, SLAB_ROWS, LANES)


def _proj_kernel(x_ref, g_ref, b_ref, w_ref, bf_ref, sel_ref, um_ref, pw_ref, ps_ref,
                 qk_ref, vt_ref, u_ref, op_ref, d_ref, ka_ref, carry_ref, ext_ref, *, tiles_per_seq, chunk):
    i = pl.program_id(0)
    tm = x_ref.shape[0]
    xn = _layer_norm(x_ref[...], g_ref[...], b_ref[...]).astype(BF16)
    first_tile = i % tiles_per_seq == 0
    ext_ref[0:N_META, :] = jnp.where(first_tile, um_ref[...], ext_ref[0:N_META, :])
    for c in range(0, D_POOL, chunk):
        uc = jnp.dot(xn, w_ref[:, 3 * D_FOX + c:3 * D_FOX + c + chunk], preferred_element_type=F32)
        u_ref[:, c:c + chunk] = uc
        ext_ref[N_META:, c:c + chunk] = uc
    for g, w in enumerate(POOL_WINDOWS):
        cols = slice(g * POOL_GROUP_DIM, (g + 1) * POOL_GROUP_DIM)
        tok = ext_ref[N_META:N_META + tm, cols]
        acc = tok
        for j in range(1, w):
            acc = acc + ext_ref[N_META - j:N_META - j + tm, cols]
        pooled = acc * (1.0 / w) - tok
        mixed = jnp.dot(pooled.astype(BF16), pw_ref[g], preferred_element_type=F32)
        op_ref[:, cols] = (mixed * ps_ref[:, cols]).astype(BF16)
    ext_ref[0:N_META, :] = ext_ref[tm:tm + N_META, :]

    scale = HEAD_DIM ** -0.5 * LOG2E
    for c in range(0, 2 * D_FOX, chunk):
        acc = jnp.dot(xn, w_ref[:, c:c + chunk], preferred_element_type=F32)
        if c < D_FOX:
            acc = acc * scale
        qk_ref[:, c:c + chunk] = acc.astype(BF16)
    sub = lax.broadcasted_iota(jnp.int32, (VT_ROWS - HEAD_DIM, tm), 0)
    tail = jnp.where(sub == 0, 1.0, 0.0).astype(BF16)
    for c in range(0, D_FOX, chunk):
        acc = jnp.dot(xn, w_ref[:, 2 * D_FOX + c:2 * D_FOX + c + chunk], preferred_element_type=F32)
        for hd in range(chunk // HEAD_DIM):
            r0 = (c // HEAD_DIM + hd) * VT_ROWS
            vt_ref[r0:r0 + HEAD_DIM, :] = acc[:, hd * HEAD_DIM:(hd + 1) * HEAD_DIM].T.astype(BF16)
            vt_ref[r0 + HEAD_DIM:r0 + VT_ROWS, :] = tail
    fl = jnp.dot(xn, w_ref[:, 3 * D_FOX + D_POOL:], preferred_element_type=F32) + bf_ref[...]
    lf = jnp.minimum(fl, 0.0) - jnp.log1p(jnp.exp(-jnp.abs(fl)))

    row = lax.broadcasted_iota(jnp.int32, lf.shape, 0)
    acc = lf
    k = 1
    while k < tm:
        acc = acc + jnp.where(row >= k, pltpu.roll(acc, k, 0), 0.0)
        k *= 2
    d = acc + jnp.where(first_tile, 0.0, carry_ref[...])
    d_ref[...] = d
    carry_ref[...] = d[tm - 1:tm, :]

    nd = d * (-LOG2E)
    hi = nd.astype(BF16)
    r1 = nd - hi.astype(F32)
    mid = r1.astype(BF16)
    lo = (r1 - mid.astype(F32)).astype(BF16)
    split = jnp.concatenate([hi, mid, lo], axis=1)
    ka_ref[...] = jnp.dot(split, sel_ref[...], preferred_element_type=F32).astype(BF16)


def _bias_placement():
    t, h = jnp.meshgrid(jnp.arange(BIAS_TERMS), jnp.arange(FOX_HEADS), indexing="ij")
    sel = jnp.zeros((BIAS_TERMS * LANES, D_FOX), F32)
    return sel.at[(t * LANES + h).ravel(), (h * HEAD_DIM + t).ravel()].set(1.0).astype(BF16)


def _proj(x2, g, b, w, bf, sel, um, pw, ps, *, tm, tiles_per_seq):
    rows, d_model = x2.shape
    n_proj = w.shape[1]
    kern = functools.partial(_proj_kernel, tiles_per_seq=tiles_per_seq, chunk=MATMUL_CHUNK)
    return pl.pallas_call(
        kern,
        grid=(rows // tm,),
        in_specs=[
            pl.BlockSpec((tm, d_model), lambda i: (i, 0)),
            pl.BlockSpec((1, d_model), lambda i: (0, 0)),
            pl.BlockSpec((1, d_model), lambda i: (0, 0)),
            pl.BlockSpec((d_model, n_proj), lambda i: (0, 0)),
            pl.BlockSpec((1, LANES), lambda i: (0, 0)),
            pl.BlockSpec(sel.shape, lambda i: (0, 0)),
            pl.BlockSpec((N_META, D_POOL), lambda i: (0, 0)),
            pl.BlockSpec(pw.shape, lambda i: (0, 0, 0)),
            pl.BlockSpec((1, D_POOL), lambda i: (0, 0)),
        ],
        out_specs=[
            pl.BlockSpec((tm, 2 * D_FOX), lambda i: (i, 0)),
            pl.BlockSpec((FOX_HEADS * VT_ROWS, tm), lambda i: (i // tiles_per_seq, i % tiles_per_seq)),
            pl.BlockSpec((tm, D_POOL), lambda i: (i, 0)),
            pl.BlockSpec((tm, D_POOL), lambda i: (i, 0)),
            pl.BlockSpec((tm, LANES), lambda i: (i, 0)),
            pl.BlockSpec((tm, D_FOX), lambda i: (i, 0)),
        ],
        out_shape=[
            jax.ShapeDtypeStruct((rows, 2 * D_FOX), BF16),
            jax.ShapeDtypeStruct((rows // (tm * tiles_per_seq) * FOX_HEADS * VT_ROWS, tm * tiles_per_seq), BF16),
            jax.ShapeDtypeStruct((rows, D_POOL), F32),
            jax.ShapeDtypeStruct((rows, D_POOL), BF16),
            jax.ShapeDtypeStruct((rows, LANES), F32),
            jax.ShapeDtypeStruct((rows, D_FOX), BF16),
        ],
        scratch_shapes=[pltpu.VMEM((1, LANES), F32), pltpu.VMEM((N_META + tm, D_POOL), F32)],
        compiler_params=_params(1),
        name="proj",
    )(x2, g, b, w, bf, sel, um, pw, ps)


def _attn_kernel(q_ref, k_ref, ka_ref, vt_ref, km_ref, kam_ref, vtm_ref, o_ref, sa_ref, sb_ref, *, th, tk):
    qi = pl.program_id(2)
    nt = (((1,), (1,)), ((), ()))
    lane = lax.broadcasted_iota(jnp.int32, (th, HEAD_DIM), 1)
    ones = jnp.where(lane < BIAS_TERMS, 1.0, 0.0).astype(BF16)
    qa = [jnp.concatenate([q_ref[h * th:(h + 1) * th, :], ones], axis=1) for h in range(2)]
    vrows = vt_ref.shape[0]

    def scores(keys, half):
        return lax.dot_general(keys, qa[half], nt, preferred_element_type=F32)

    def x_keys(j):
        ks = pl.multiple_of(j * tk, tk)
        return jnp.concatenate([k_ref[pl.ds(ks, tk), :], ka_ref[pl.ds(ks, tk), :]], axis=1)

    def update(read_s, vt, carry, masked=False):
        m, acc = carry
        if masked:
            key = lax.broadcasted_iota(jnp.int32, (tk, th), 0)
            qry = lax.broadcasted_iota(jnp.int32, (tk, th), 1)
            read = lambda: jnp.where(key <= qry, read_s(), -jnp.inf)
        else:
            read = read_s
        m_new = jnp.maximum(m, jnp.max(read(), axis=0, keepdims=True))
        a = jnp.exp2(m - m_new)
        p = jnp.exp2(read() - m_new).astype(BF16)
        return m_new, a * acc + jnp.dot(vt, p, preferred_element_type=F32)

    def x_vt(j):
        return vt_ref[:, pl.ds(pl.multiple_of(j * tk, tk), tk)]

    keys_m = jnp.concatenate([km_ref[...], kam_ref[...]], axis=1)
    carry = []
    for h in range(2):
        s_m = scores(keys_m, h)
        init = (jnp.full((1, th), -jnp.inf, F32), jnp.zeros((vrows, th), F32))
        carry.append(update(lambda: s_m, vtm_ref[...], init))

    def fill(buf, j):
        keys = x_keys(j)
        for h in range(2):
            buf[h] = scores(keys, h)

    def drain(buf, j, carry):
        vt = x_vt(j)
        return tuple(update(lambda h=h: buf[h], vt, carry[h]) for h in range(2))

    def body(jj, carry):
        j = 2 * jj
        fill(sb_ref, j + 1)
        carry = drain(sa_ref, j, carry)
        fill(sa_ref, j + 2)
        return drain(sb_ref, j + 1, carry)

    first = 2 * qi
    fill(sa_ref, 0)
    c0, c1 = lax.fori_loop(0, qi, body, tuple(carry))
    s11 = scores(x_keys(first + 1), 1)
    c0 = update(lambda: sa_ref[0], x_vt(first), c0, masked=True)
    c1 = update(lambda: sa_ref[1], x_vt(first), c1)
    c1 = update(lambda: s11, x_vt(first + 1), c1, masked=True)
    for h, (m, acc) in enumerate((c0, c1)):
        out_t = acc[:HEAD_DIM] / acc[HEAD_DIM:HEAD_DIM + 1]
        o_ref[h * th:(h + 1) * th, :] = out_t.T.astype(BF16)


def _attention(qk, kaug, vt, qk_m, kaug_m, vt_m, *, batch, seq, th, tk):
    assert th == tk
    tq = 2 * th
    nq = seq // tq
    vrows = vt.shape[0] // (batch * FOX_HEADS)
    kern = functools.partial(_attn_kernel, th=th, tk=tk)
    return pl.pallas_call(
        kern,
        grid=(batch, FOX_HEADS, nq),
        in_specs=[
            pl.BlockSpec((tq, HEAD_DIM), lambda b, h, i: (b * nq + i, h)),
            pl.BlockSpec((seq, HEAD_DIM), lambda b, h, i: (b, FOX_HEADS + h)),
            pl.BlockSpec((seq, HEAD_DIM), lambda b, h, i: (b, h)),
            pl.BlockSpec((vrows, seq), lambda b, h, i: (b * FOX_HEADS + h, 0)),
            pl.BlockSpec((META_PAD, HEAD_DIM), lambda b, h, i: (0, FOX_HEADS + h)),
            pl.BlockSpec((META_PAD, HEAD_DIM), lambda b, h, i: (0, h)),
            pl.BlockSpec((vrows, META_PAD), lambda b, h, i: (h, 0)),
        ],
        out_specs=pl.BlockSpec((tq, HEAD_DIM), lambda b, h, i: (b * nq + i, h)),
        out_shape=jax.ShapeDtypeStruct((batch * seq, D_FOX), BF16),
        scratch_shapes=[pltpu.VMEM((2, tk, th), F32), pltpu.VMEM((2, tk, th), F32)],
        compiler_params=_params(3),
        name="attn",
    )(qk, qk, kaug, vt, qk_m, kaug_m, vt_m)


def _bias_terms(neg_bias):
    terms, rest = [], neg_bias
    for _ in range(BIAS_TERMS):
        t = rest.astype(BF16)
        terms.append(t)
        rest = rest - t.astype(F32)
    return jnp.stack(terms)


def _mix_kernel(x_ref, of_ref, op_ref, wo_ref, gi_ref, bi_ref, gm_ref, bm_ref, wr_ref, br_ref, tri_ref,
                h_ref, hp_ref, ri_ref, rg_ref, cnt_ref, carry_ref, *, chunk):
    i = pl.program_id(0)
    tm = x_ref.shape[0]
    d_model = x_ref.shape[1]
    h0 = _layer_norm(x_ref[...], gi_ref[...], bi_ref[...])
    of = of_ref[...]
    op = op_ref[...]
    for c in range(0, d_model, chunk):
        mix = jnp.dot(of, wo_ref[0:D_FOX, c:c + chunk], preferred_element_type=F32)
        mix = mix + jnp.dot(op, wo_ref[D_FOX:, c:c + chunk], preferred_element_type=F32)
        h_ref[:, c:c + chunk] = DEEPNORM_ALPHA * h0[:, c:c + chunk] + mix
    h1 = _layer_norm(h_ref[...], gm_ref[...], bm_ref[...])
    h_ref[...] = h1
    _store_slabs(hp_ref, _pack_halves(h1))

    h_hi = h1.astype(BF16)
    h_mid = (h1 - h_hi.astype(F32)).astype(BF16)
    hh = jnp.dot(h_hi, wr_ref[...], preferred_element_type=F32)
    mh = jnp.dot(h_mid, wr_ref[:, :LANES], preferred_element_type=F32)
    logits = hh[:, :LANES] + hh[:, LANES:] + mh + br_ref[...]
    lt = logits.T[:ROUTE_ROWS]
    row = lax.broadcasted_iota(jnp.int32, lt.shape, 0)
    neg = -jnp.inf
    first_min = lambda hit: jnp.min(jnp.where(hit, row, ROUTE_ROWS), axis=0, keepdims=True)
    gl = jnp.where(row < N_GROUPS, lt, neg)
    gmax = jnp.max(gl, axis=0, keepdims=True)
    g_idx = first_min(gl == gmax)
    p_g = 1.0 / jnp.sum(jnp.exp(gl - gmax), axis=0, keepdims=True)
    lo = N_GROUPS + g_idx * EXPERTS_PER_GROUP
    el = jnp.where((row >= lo) & (row < lo + EXPERTS_PER_GROUP), lt, neg)
    v1 = jnp.max(el, axis=0, keepdims=True)
    i1 = first_min(el == v1)
    el2 = jnp.where(row == i1, neg, el)
    v2 = jnp.max(el2, axis=0, keepdims=True)
    i2 = first_min(el2 == v2)
    t = jnp.exp(v2 - v1)
    gate1 = p_g / (1.0 + t)
    gate2 = gate1 * t

    @pl.when(i == 0)
    def _():
        carry_ref[...] = jnp.zeros_like(carry_ref)

    upper = tri_ref[...]
    oh1 = (row == i1).astype(F32)
    oh2 = (row == i2).astype(F32)
    pre1 = jnp.dot(oh1.astype(BF16), upper, preferred_element_type=F32)
    pre2 = jnp.dot(oh2.astype(BF16), upper, preferred_element_type=F32)
    cnt1 = jnp.sum(oh1, axis=1, keepdims=True)
    cnt2 = jnp.sum(oh2, axis=1, keepdims=True)
    base = carry_ref[...]
    rank1 = jnp.sum((pre1 + base) * oh1, axis=0, keepdims=True)
    rank2 = jnp.sum((pre2 + base + cnt1) * oh2, axis=0, keepdims=True)
    total = base + cnt1 + cnt2
    carry_ref[...] = total
    cnt_ref[...] = jnp.broadcast_to(total, cnt_ref.shape)

    r8 = lax.broadcasted_iota(jnp.int32, ri_ref.shape, 0)
    ri_ref[...] = jnp.where(r8 == 0, i1 - N_GROUPS, jnp.where(r8 == 1, i2 - N_GROUPS, jnp.where(
        r8 == 2, rank1.astype(jnp.int32), jnp.where(r8 == 3, rank2.astype(jnp.int32), 0))))
    rl = lax.broadcasted_iota(jnp.int32, (LANES, tm), 0)
    rg_ref[...] = jnp.where(rl == 0, gate1, jnp.where(rl == 1, gate2, 0.0)).T


def _mix(x2, o_fox, o_pool, wo, gi, bi, gm, bm, wr, br, *, tm):
    rows, d_model = x2.shape
    idx = jnp.arange(tm, dtype=jnp.int32)
    tri = (idx[:, None] < idx[None, :]).astype(BF16)
    kern = functools.partial(_mix_kernel, chunk=MATMUL_CHUNK)
    row_spec = lambda w: pl.BlockSpec((tm, w), lambda i: (i, 0))
    vec_spec = lambda w: pl.BlockSpec((1, w), lambda i: (0, 0))
    return pl.pallas_call(
        kern,
        grid=(rows // tm,),
        in_specs=[
            row_spec(d_model), row_spec(D_FOX), row_spec(D_POOL),
            pl.BlockSpec(wo.shape, lambda i: (0, 0)),
            vec_spec(d_model), vec_spec(d_model), vec_spec(d_model), vec_spec(d_model),
            pl.BlockSpec(wr.shape, lambda i: (0, 0)),
            vec_spec(LANES),
            pl.BlockSpec((tm, tm), lambda i: (0, 0)),
        ],
        out_specs=[row_spec(d_model), pl.BlockSpec((tm * SLAB_ROWS, LANES), lambda i: (i, 0)),
                   pl.BlockSpec((SLAB_ROWS, tm), lambda i: (0, i)), row_spec(LANES),
                   pl.BlockSpec((ROUTE_ROWS, LANES), lambda i: (0, 0))],
        out_shape=[
            jax.ShapeDtypeStruct((rows, d_model), F32),
            jax.ShapeDtypeStruct((rows * SLAB_ROWS, LANES), jnp.uint32),
            jax.ShapeDtypeStruct((SLAB_ROWS, rows), jnp.int32),
            jax.ShapeDtypeStruct((rows, LANES), F32),
            jax.ShapeDtypeStruct((ROUTE_ROWS, LANES), F32),
        ],
        scratch_shapes=[pltpu.VMEM((ROUTE_ROWS, 1), F32)],
        compiler_params=_params(1),
        name="mix",
    )(x2, o_fox, o_pool, wo, gi, bi, gm, bm, wr, br, tri)


def _moe_kernel(be_ref, na_ref, tok_ref, tokn_ref, dst_ref, h_hbm, w13_ref, w2_ref, y_hbm, xbuf, obuf, gsem, ssem):
    i = pl.program_id(0)
    n_act = na_ref[0]
    te = xbuf.shape[1] * SLAB_ROWS
    slot = i % 2

    def row_view(buf, s, r):
        return buf.at[s, r // SLAB_ROWS, :, r % SLAB_ROWS, :]

    def start_gather(idx_ref, s):
        for r in range(te):
            pltpu.make_async_copy(h_hbm.at[idx_ref[0, 0, r]], row_view(xbuf, s, r), gsem.at[s]).start(priority=1)

    def wait_gather(s):
        pltpu.make_async_copy(xbuf.at[1 - s], xbuf.at[s], gsem.at[s]).wait()

    def start_scatter(idx, s):
        for r in range(te):
            pltpu.make_async_copy(row_view(obuf, s, r), y_hbm.at[idx(r)], ssem.at[s]).start(priority=r % 2)

    def wait_scatter(s):
        pltpu.make_async_copy(obuf.at[1 - s], obuf.at[s], ssem.at[s]).wait()

    @pl.when(i < n_act)
    def _():
        @pl.when(i == 0)
        def _():
            start_gather(tok_ref, slot)
            spare = y_hbm.shape[0] - 2 * te
            obuf[...] = jnp.zeros_like(obuf)
            for s in range(2):
                start_scatter(lambda r, s=s: spare + s * te + r, s)
            for s in range(2):
                wait_scatter(s)

        wait_gather(slot)

        @pl.when(i + 1 < n_act)
        def _():
            start_gather(tokn_ref, 1 - slot)

        @pl.when(i >= 2)
        def _():
            wait_scatter(slot)

        x = _unpack_halves(_row_tiles_to_value(xbuf.at[slot])).astype(BF16)
        gate = jnp.dot(x, w13_ref[0, :, :D_EXPERT].astype(BF16), preferred_element_type=F32)
        up = jnp.dot(x, w13_ref[0, :, D_EXPERT:].astype(BF16), preferred_element_type=F32)
        hid = (gate / (1.0 + jnp.exp(-gate)) * up).astype(BF16)
        out = jnp.dot(hid, w2_ref[0].astype(BF16), preferred_element_type=F32)
        _value_to_row_tiles(obuf.at[slot], _pack_halves(out))
        start_scatter(lambda r: dst_ref[0, 0, r], slot)

        @pl.when(i == n_act - 1)
        def _():
            @pl.when(i >= 1)
            def _():
                wait_scatter(1 - slot)
            wait_scatter(slot)


def _moe(blk_e, n_act, rows_tok, rows_dst, h1p, w13, w2, *, te):
    rows = h1p.shape[0] // SLAB_ROWS
    words = SLAB_ROWS * LANES
    d_model = 2 * words
    nb = rows_tok.shape[0]
    smem_spec = lambda off: pl.BlockSpec((1, 1, te), lambda i, be, na: (jnp.minimum(i + off, nb - 1), 0, 0),
                                         memory_space=pltpu.SMEM)
    tiles = (2, te // SLAB_ROWS, words // LANES, SLAB_ROWS, LANES)
    y = pl.pallas_call(
        _moe_kernel,
        grid_spec=pltpu.PrefetchScalarGridSpec(
            num_scalar_prefetch=2,
            grid=(nb,),
            in_specs=[
                smem_spec(0), smem_spec(1), smem_spec(0),
                pl.BlockSpec(memory_space=pl.ANY),
                pl.BlockSpec((1, d_model, 2 * D_EXPERT), lambda i, be, na: (be[i], 0, 0)),
                pl.BlockSpec((1, D_EXPERT, d_model), lambda i, be, na: (be[i], 0, 0)),
            ],
            out_specs=pl.BlockSpec(memory_space=pl.ANY),
            scratch_shapes=[
                pltpu.VMEM(tiles, jnp.uint32),
                pltpu.VMEM(tiles, jnp.uint32),
                pltpu.SemaphoreType.DMA((2,)),
                pltpu.SemaphoreType.DMA((2,)),
            ],
        ),
        out_shape=jax.ShapeDtypeStruct((2 * rows + 2 * te, SLAB_ROWS, LANES), jnp.uint32),
        compiler_params=_params(1, vmem=MOE_VMEM_LIMIT),
        name="moe",
    )(blk_e, n_act, rows_tok, rows_tok, rows_dst, h1p.reshape(rows, SLAB_ROWS, LANES), w13, w2)
    return y.reshape(-1, LANES)


def _final_kernel(h_ref, y0_ref, y1_ref, rg_ref, g_ref, b_ref, o_ref):
    rg = rg_ref[...]
    ffn = _unpack_halves(_load_slabs(y0_ref)) * rg[:, 0:1] + _unpack_halves(_load_slabs(y1_ref)) * rg[:, 1:2]
    o_ref[...] = _layer_norm(DEEPNORM_ALPHA * h_ref[...] + ffn, g_ref[...], b_ref[...])


def _final(h1, y, rg, g, b, *, tm):
    rows, d_model = h1.shape
    nt = rows // tm
    return pl.pallas_call(
        _final_kernel,
        grid=(nt,),
        in_specs=[
            pl.BlockSpec((tm, d_model), lambda i: (i, 0)),
            pl.BlockSpec((tm * SLAB_ROWS, LANES), lambda i: (i, 0)),
            pl.BlockSpec((tm * SLAB_ROWS, LANES), lambda i: (nt + i, 0)),
            pl.BlockSpec((tm, LANES), lambda i: (i, 0)),
            pl.BlockSpec((1, d_model), lambda i: (0, 0)),
            pl.BlockSpec((1, d_model), lambda i: (0, 0)),
        ],
        out_specs=pl.BlockSpec((tm, d_model), lambda i: (i, 0)),
        out_shape=jax.ShapeDtypeStruct((rows, d_model), F32),
        compiler_params=_params(1),
        name="final",
    )(h1, y, y, rg, g, b)


def _invert_kernel(dest_ref, fill_hbm, out_ref):
    pltpu.sync_copy(fill_hbm, out_ref)

    def place(a, _):
        out_ref[dest_ref[a]] = a
        return 0

    lax.fori_loop(0, dest_ref.shape[0], place, 0, unroll=8)


def _invert_dispatch(dest, *, n_out, spare, spare_rows):
    smem = pl.BlockSpec(memory_space=pltpu.SMEM)
    fill = spare + jnp.arange(n_out, dtype=jnp.int32) % spare_rows
    return pl.pallas_call(
        _invert_kernel,
        in_specs=[smem, pl.BlockSpec(memory_space=pl.ANY)],
        out_specs=smem,
        out_shape=jax.ShapeDtypeStruct((n_out,), jnp.int32),
        name="invert",
    )(dest, fill)


def _dispatch_tables(ri, cnt, *, rows, te):
    experts = ri[0:2]
    rank = ri[2:4]
    counts = cnt[N_GROUPS:N_GROUPS + N_EXPERTS, 0].astype(jnp.int32)
    padded = ((counts + te - 1) // te) * te
    pend = jnp.cumsum(padded)
    pstart = pend - padded
    eids = jnp.arange(N_EXPERTS, dtype=jnp.int32)
    seg = jnp.sum(jnp.where(experts[..., None] == eids, pstart, 0), axis=-1)
    dest = (seg + rank).reshape(-1)
    nb = (2 * rows + N_EXPERTS * (te - 1) + te - 1) // te
    rows_dst = _invert_dispatch(dest, n_out=nb * te, spare=2 * rows, spare_rows=2 * te)
    rows_tok = jnp.where(rows_dst < rows, rows_dst, jnp.where(rows_dst < 2 * rows, rows_dst - rows, 0))
    blk_e = jnp.minimum(jnp.sum(pend[None, :] <= (jnp.arange(nb, dtype=jnp.int32) * te)[:, None], axis=1),
                        N_EXPERTS - 1).astype(jnp.int32)
    n_act = (pend[-1] // te).astype(jnp.int32).reshape(1)
    return blk_e, n_act, rows_tok.reshape(nb, 1, te), rows_dst.reshape(nb, 1, te)


def kernel(x, meta, ln_in_g, ln_in_b, w_in, b_f, pool_w, pool_scale, w_out, ln_mix_g, ln_mix_b,
           w_router_g, b_router_g, w_router_e, b_router_e, w13, w2, ln_ffn_g, ln_ffn_b):
    batch, seq, d_model = x.shape
    rows = batch * seq
    tile = min(ROW_TILE, seq)
    attn_tile = min(ATTN_TILE, seq // 2)
    assert seq % (2 * attn_tile) == 0 and seq % tile == 0 and tile % N_META == 0
    assert w_in.shape[0] == 1, "depth-1 trunk"

    x2 = x.reshape(rows, d_model)
    row = lambda v: v.reshape(1, -1).astype(F32)

    wi = w_in[0]
    w_f = jnp.pad(wi[:, 3 * D_FOX:3 * D_FOX + FOX_HEADS], ((0, 0), (0, LANES - FOX_HEADS)))
    w_proj = jnp.concatenate([wi[:, :3 * D_FOX], wi[:, 3 * D_FOX + FOX_HEADS:], w_f], axis=1).astype(BF16)
    bf = jnp.pad(b_f[0], (0, LANES - FOX_HEADS)).reshape(1, LANES)
    gi, bi = row(ln_in_g), row(ln_in_b)

    meta_pad = jnp.pad(meta.astype(F32), ((0, META_PAD - N_META), (0, 0)))
    sel = _bias_placement()
    pw, ps = pool_w[0].astype(BF16), row(pool_scale[0])
    no_halo = jnp.zeros((N_META, D_POOL), F32)
    qk_m, vt_m, u_m, _, c_m, _ = _proj(meta_pad, gi, bi, w_proj, bf, sel, no_halo, pw, ps, tm=META_PAD, tiles_per_seq=1)
    d_meta = (c_m[:N_META, :FOX_HEADS] - c_m[N_META - 1:N_META, :FOX_HEADS]) * LOG2E
    d_meta = jnp.pad(d_meta, ((0, META_PAD - N_META), (0, 0)), constant_values=MASKED_BIAS)
    terms = _bias_terms(-d_meta)
    kaug_m = jnp.zeros((META_PAD, FOX_HEADS, HEAD_DIM), BF16).at[:, :, :BIAS_TERMS].set(terms.transpose(1, 2, 0))
    kaug_m = kaug_m.reshape(META_PAD, D_FOX)

    qk, vt, _, o_pool, _, kaug = _proj(x2, gi, bi, w_proj, bf, sel, u_m[:N_META], pw, ps, tm=tile,
                                       tiles_per_seq=seq // tile)
    o_fox = _attention(qk, kaug, vt, qk_m, kaug_m, vt_m, batch=batch, seq=seq, th=attn_tile, tk=attn_tile)

    w_r = jnp.pad(jnp.concatenate([w_router_g[0], w_router_e[0]], axis=1),
                  ((0, 0), (0, LANES - N_GROUPS - N_EXPERTS)))
    w_r_hi = w_r.astype(BF16)
    w_r = jnp.concatenate([w_r_hi, (w_r - w_r_hi.astype(F32)).astype(BF16)], axis=1)
    b_r = jnp.pad(jnp.concatenate([b_router_g[0], b_router_e[0]]), (0, LANES - N_GROUPS - N_EXPERTS)).reshape(1, LANES)
    h1, h1p, ri, rg, cnt = _mix(x2, o_fox, o_pool, w_out[0].astype(BF16), gi, bi, row(ln_mix_g[0]),
                                row(ln_mix_b[0]), w_r, b_r, tm=tile)

    te = EXPERT_BLOCK_ROWS
    blk_e, n_act, rows_tok, rows_dst = _dispatch_tables(ri, cnt, rows=rows, te=te)
    y = _moe(blk_e, n_act, rows_tok, rows_dst, h1p, w13[0], w2[0], te=te)

    out = _final(h1, y, rg, row(ln_ffn_g[0]), row(ln_ffn_b[0]), tm=tile)
    return out.reshape(batch, seq, d_model)
```
